```python
import jax, jax.numpy as jnp
from jax import lax
import numpy as np

D_MODEL = 1024
BATCH = 8
SEQ = 2048
DEPTH = 4
DEC_BATCH = 128
DEC_SEQ = 1
PAST_LEN = 2048
PAGE_SIZE = 128

HEAD_DIM = 64
FOX_W = D_MODEL // 2
FOX_HEADS = FOX_W // HEAD_DIM
Q_BLOCK = 128
LRU_W = D_MODEL // 4
LRU_BLOCKS = LRU_W // HEAD_DIM
LRU_BW = LRU_W // LRU_BLOCKS
LRU_C = 8.0
CONV_W = 4
GLA_W = D_MODEL // 4
GLA_HEADS = GLA_W // HEAD_DIM
GLA_DK = HEAD_DIM
GLA_DV = HEAD_DIM
GLA_RANK = 16
GLA_TAU = 16.0
GLA_CHUNK = 64
MIX_W = FOX_W + LRU_W + GLA_W
IN_SPLITS = (FOX_W, FOX_W, FOX_W, FOX_HEADS, LRU_W, LRU_W, GLA_HEADS * GLA_DK, GLA_HEADS * GLA_DK, GLA_W, GLA_RANK, GLA_W)
IN_W = sum(IN_SPLITS)
N_EXPERTS = 32
TOP_K = 4
D_FF = D_MODEL
SWIGLU_LIMIT = 7.0
SWIGLU_ALPHA = 1.702
GROUP_BLOCK = 128
NORM_EPS = 1e-6

kernel_name = 'hybrid_fox_rglru_gla_moe_adaln_step'


def _split_points():
    return [int(v) for v in np.cumsum(IN_SPLITS)[:-1]]


def rmsnorm(x, g):
    xf = x.astype(jnp.float32)
    y = xf * lax.rsqrt(jnp.mean(xf * xf, axis=-1, keepdims=True) + NORM_EPS)
    return (y * g.astype(jnp.float32)).astype(x.dtype)


def fox_prompt(q, k, v, logf):
    B, T, H, Dh = q.shape
    nq = T // Q_BLOCK
    cum = jnp.cumsum(logf, axis=1).transpose(0, 2, 1)
    qb = q.reshape(B, nq, Q_BLOCK, H, Dh).transpose(1, 0, 2, 3, 4)
    cb = cum.reshape(B, H, nq, Q_BLOCK).transpose(2, 0, 1, 3)
    kpos = jnp.arange(T)
    scale = Dh ** -0.5

    def one_block(args):
        qi, ci, i = args
        s = jnp.einsum('bqhd,bkhd->bhqk', qi, k).astype(jnp.float32) * scale
        s = s + ci[..., :, None] - cum[:, :, None, :]
        qpos = i * Q_BLOCK + jnp.arange(Q_BLOCK)
        s = jnp.where(kpos[None, :] <= qpos[:, None], s, -jnp.inf)
        p = jax.nn.softmax(s, axis=-1)
        return jnp.einsum('bhqk,bkhd->bqhd', p.astype(v.dtype), v)

    o = lax.map(one_block, (qb, cb, jnp.arange(nq)))
    return o.transpose(1, 0, 2, 3, 4).reshape(B, T, H, Dh)


def fox_sample(q, k_new, v_new, logf_new, k_past, v_past, logf_past):
    S, Dh = q.shape[1], q.shape[-1]
    P = k_past.shape[1]
    scale = Dh ** -0.5
    f32 = jnp.float32
    pc = jnp.cumsum(logf_past.astype(f32), axis=1)
    suffix = (pc[:, -1:, :] - pc).transpose(0, 2, 1)
    dn = jnp.cumsum(logf_new, axis=1).transpose(0, 2, 1)
    s_past = jnp.einsum('bqhd,bkhd->bhqk', q, k_past).astype(f32) * scale + dn[..., :, None] + suffix[:, :, None, :]
    s_new = jnp.einsum('bqhd,bkhd->bhqk', q, k_new).astype(f32) * scale + dn[..., :, None] - dn[:, :, None, :]
    s_new = jnp.where(jnp.tril(jnp.ones((S, S), bool)), s_new, -jnp.inf)
    p = jax.nn.softmax(jnp.concatenate([s_past, s_new], axis=-1), axis=-1).astype(v_new.dtype)
    return jnp.einsum('bhqk,bkhd->bqhd', p[..., :P], v_past) + jnp.einsum('bhqk,bkhd->bqhd', p[..., P:], v_new)


def causal_conv(x, prev, w, b):
    T = x.shape[1]
    full = jnp.concatenate([prev.astype(x.dtype), x], axis=1)
    y = b + sum(full[:, j:j + T] * w[j] for j in range(CONV_W))
    return y, full[:, T:]


def block_diag(x, w, b):
    xb = x.reshape(*x.shape[:-1], LRU_BLOCKS, LRU_BW)
    return jnp.einsum('btnc,ncd->btnd', xb, w).reshape(x.shape) + b


def _linear_combine(left, right):
    a1, b1 = left
    a2, b2 = right
    return a1 * a2, a2 * b1 + b2


def rglru(x, lam, wa, ba, wx, bx, h0):
    f32 = jnp.float32
    xf = x.astype(f32)
    r = jax.nn.sigmoid(block_diag(x, wa, ba).astype(f32))
    gi = jax.nn.sigmoid(block_diag(x, wx, bx).astype(f32))
    log_a = -LRU_C * r * jax.nn.softplus(-lam.astype(f32))
    a = jnp.exp(log_a)
    mult = jnp.sqrt(-jnp.expm1(2.0 * log_a))
    if h0 is None:
        mult = mult.at[:, 0].set(1.0)
    bterm = mult * gi * xf
    if h0 is not None:
        bterm = bterm.at[:, 0].add(a[:, 0] * h0.astype(f32))
    _, h = lax.associative_scan(_linear_combine, (a, bterm), axis=1)
    return h, h[:, -1]


def gla_chunked(q, k, v, g):
    B, T, H, DK = q.shape
    DV = v.shape[-1]
    C = GLA_CHUNK
    N = T // C
    f32 = jnp.float32

    def chunks(t):
        return t.astype(f32).reshape(B, N, C, H, t.shape[-1])

    qc = chunks(q) * DK ** -0.5
    kc, vc, gc = chunks(k), chunks(v), chunks(g)
    bc = jnp.cumsum(gc, axis=2)
    b_last = bc[:, :, -1:]
    q_dec = qc * jnp.exp(bc)
    att = jnp.einsum('bnthk,bnshk->bnhts', q_dec, kc * jnp.exp(-bc))
    att = jnp.where(jnp.tril(jnp.ones((C, C), bool)), att, 0.0)
    o_intra = jnp.einsum('bnhts,bnshv->bnthv', att, vc)
    u = jnp.einsum('bnshk,bnshv->bnhkv', kc * jnp.exp(b_last - bc), vc)
    decay = jnp.exp(b_last[:, :, 0])

    def step(s, inp):
        d, ui = inp
        return d[..., None] * s + ui, s

    s_final, s_before = lax.scan(step, jnp.zeros((B, H, DK, DV), f32),
                                 (jnp.moveaxis(decay, 1, 0), jnp.moveaxis(u, 1, 0)))
    o_inter = jnp.einsum('bnthk,bnhkv->bnthv', q_dec, jnp.moveaxis(s_before, 0, 1))
    return (o_intra + o_inter).reshape(B, T, H, DV), s_final


def gla_recurrent(q, k, v, g, s0):
    f32 = jnp.float32
    qs = q.astype(f32) * q.shape[-1] ** -0.5

    def step(s, inp):
        qt, kt, vt, gt = inp
        s = jnp.exp(gt)[..., None] * s + kt[..., :, None] * vt[..., None, :]
        return s, jnp.einsum('bhk,bhkv->bhv', qt, s)

    xs = tuple(jnp.moveaxis(t.astype(f32), 1, 0) for t in (qs, k, v, g))
    s_final, o = lax.scan(step, s0.astype(f32), xs)
    return jnp.moveaxis(o, 0, 1), s_final


def head_rmsnorm(o, g):
    y = o * lax.rsqrt(jnp.mean(o * o, axis=-1, keepdims=True) + NORM_EPS)
    return y * g.astype(jnp.float32).reshape(GLA_HEADS, GLA_DV)


def token_mixers(h, lp, past):
    B, T, _ = h.shape
    z = h @ lp['w_in']
    fq, fk, fv, ff, lx, lg, gq, gk, gv, ga, gog = jnp.split(z, _split_points(), axis=-1)
    fq = fq.reshape(B, T, FOX_HEADS, HEAD_DIM)
    fk = fk.reshape(B, T, FOX_HEADS, HEAD_DIM)
    fv = fv.reshape(B, T, FOX_HEADS, HEAD_DIM)
    logf = jax.nn.log_sigmoid((ff + lp['b_forget']).astype(jnp.float32))
    if past is None:
        fo = fox_prompt(fq, fk, fv, logf)
        conv_prev = jnp.zeros((B, CONV_W - 1, LRU_W), lx.dtype)
        h0, s0 = None, None
    else:
        k_past, v_past, logf_past, conv_prev, h0, s0 = past
        fo = fox_sample(fq, fk, fv, logf, k_past, v_past, logf_past)
    xc, conv_new = causal_conv(lx, conv_prev, lp['conv_w'], lp['conv_b'])
    hl, h_last = rglru(xc, lp['lam'], lp['wa'], lp['ba'], lp['wx'], lp['bx'], h0)
    lo = hl.astype(h.dtype) * jax.nn.gelu(lg)
    g_log = jax.nn.log_sigmoid((ga @ lp['gla_w2'] + lp['gla_b2']).astype(jnp.float32)) / GLA_TAU
    gq = gq.reshape(B, T, GLA_HEADS, GLA_DK)
    gk = gk.reshape(B, T, GLA_HEADS, GLA_DK)
    gv = gv.reshape(B, T, GLA_HEADS, GLA_DV)
    g_log = g_log.reshape(B, T, GLA_HEADS, GLA_DK)
    if past is None:
        go, s_new = gla_chunked(gq, gk, gv, g_log)
    else:
        go, s_new = gla_recurrent(gq, gk, gv, g_log, s0)
    go = head_rmsnorm(go, lp['gla_g']).reshape(B, T, GLA_W).astype(h.dtype) * jax.nn.silu(gog)
    mix = jnp.concatenate([fo.reshape(B, T, FOX_W), lo, go], axis=-1)
    new_state = (fk, fv, logf, conv_new, h_last.astype(h.dtype), s_new.astype(h.dtype))
    return mix @ lp['w_out'], new_state


def expert_ffn(x, w_gu, b_gu, w_dn, b_dn):
    gu = x @ w_gu + b_gu
    g = jnp.minimum(gu[..., :D_FF], SWIGLU_LIMIT)
    u = jnp.clip(gu[..., D_FF:], -SWIGLU_LIMIT, SWIGLU_LIMIT)
    act = g * jax.nn.sigmoid(SWIGLU_ALPHA * g)
    return ((u + 1.0) * act) @ w_dn + b_dn


def grouped_experts(t, idx, wts, w_gu, b_gu, w_dn, b_dn):
    n_tok = t.shape[0]
    n_asg = n_tok * TOP_K
    flat_e = idx.reshape(-1)
    order = jnp.argsort(flat_e)
    e_sorted = flat_e[order]
    counts = jnp.bincount(flat_e, length=N_EXPERTS)
    padded = (counts + GROUP_BLOCK - 1) // GROUP_BLOCK * GROUP_BLOCK
    pad_end = jnp.cumsum(padded)
    pad_start = pad_end - padded
    grp_start = jnp.cumsum(counts) - counts
    dest = pad_start[e_sorted] + jnp.arange(n_asg) - grp_start[e_sorted]
    n_blocks = -(-(n_asg + N_EXPERTS * (GROUP_BLOCK - 1)) // GROUP_BLOCK)
    n_rows = n_blocks * GROUP_BLOCK
    row_tok = jnp.zeros((n_rows,), jnp.int32).at[dest].set((order // TOP_K).astype(jnp.int32))
    row_w = jnp.zeros((n_rows,), wts.dtype).at[dest].set(wts.reshape(-1)[order])
    blk_e = jnp.minimum(jnp.searchsorted(pad_end, jnp.arange(n_blocks) * GROUP_BLOCK, side='right'), N_EXPERTS - 1)
    xb = t[row_tok].reshape(n_blocks, GROUP_BLOCK, t.shape[-1])

    def run(args):
        xe, e = args
        return expert_ffn(xe, w_gu[e], b_gu[e], w_dn[e], b_dn[e])

    y = lax.map(run, (xb, blk_e)).reshape(n_rows, -1)
    return jnp.zeros_like(t).at[row_tok].add(row_w[:, None] * y)


def moe(h, lp):
    B, T, D = h.shape
    t = h.reshape(B * T, D)
    logits = (t @ lp['w_router'] + lp['b_router']).astype(jnp.float32)
    top_v, top_i = lax.top_k(logits, TOP_K)
    wts = jax.nn.softmax(top_v, axis=-1).astype(t.dtype)
    out = grouped_experts(t, top_i, wts, lp['w_gu'], lp['b_gu'], lp['w_down'], lp['b_down'])
    return out.reshape(B, T, D)


def trunk_layer(y, c, lp, past):
    mod = (jax.nn.silu(c) @ lp['w_ada'] + lp['b_ada'])[:, None, :]
    sh1, sc1, gt1, sh2, sc2, gt2 = jnp.split(mod, 6, axis=-1)
    h = rmsnorm(y, lp['g1']) * (1.0 + sc1) + sh1
    m, st = token_mixers(h, lp, past)
    y = y + gt1 * m
    h = rmsnorm(y, lp['g2']) * (1.0 + sc2) + sh2
    y = y + gt2 * moe(h, lp)
    return y, st


def setup_inputs(seed: int = 0) -> dict:
    key = jax.random.key(seed)
    ks = iter(jax.random.split(key, 48))

    def nrm(shape, s=1.0):
        return s * jax.random.normal(next(ks), shape, jnp.float32)

    L, D = DEPTH, D_MODEL
    n_pages = PAST_LEN // PAGE_SIZE
    n_pool = (5 * DEC_BATCH * n_pages + 3) // 4
    a0 = jax.random.uniform(next(ks), (L, LRU_W), jnp.float32, 0.9, 0.999)
    s0 = a0 ** (1.0 / LRU_C)
    lam = jnp.log(s0) - jnp.log1p(-s0)
    page_table = jax.random.permutation(next(ks), n_pool)[:DEC_BATCH * n_pages].reshape(DEC_BATCH, n_pages).astype(jnp.int32)
    return {
        'x_prompt': nrm((BATCH, SEQ, D)),
        'x_sample': nrm((DEC_BATCH, DEC_SEQ, D)),
        'cache_fox_k': nrm((L, n_pool, PAGE_SIZE, FOX_HEADS, HEAD_DIM)),
        'cache_fox_v': nrm((L, n_pool, PAGE_SIZE, FOX_HEADS, HEAD_DIM)),
        'cache_fox_logf': jax.nn.log_sigmoid(nrm((L, n_pool, PAGE_SIZE, FOX_HEADS)) + 3.0),
        'state_conv': nrm((L, DEC_BATCH, CONV_W - 1, LRU_W)),
        'state_lru': nrm((L, DEC_BATCH, LRU_W), 0.5),
        'state_gla': nrm((L, DEC_BATCH, GLA_HEADS, GLA_DK, GLA_DV), 0.3),
        'page_table': page_table,
        'c_prompt': nrm((BATCH, D)),
        'c_sample': nrm((DEC_BATCH, D)),
        'w_ada': nrm((L, D, 6 * D), 0.5 * D ** -0.5),
        'b_ada': nrm((L, 6 * D), 0.02),
        'g_norm1': 1.0 + nrm((L, D), 0.05),
        'g_norm2': 1.0 + nrm((L, D), 0.05),
        'w_in': nrm((L, D, IN_W), D ** -0.5),
        'b_forget': 3.0 + nrm((L, FOX_HEADS), 0.5),
        'conv_w': nrm((L, CONV_W, LRU_W), CONV_W ** -0.5),
        'conv_b': nrm((L, LRU_W), 0.02),
        'lru_lambda': lam,
        'lru_wa': nrm((L, LRU_BLOCKS, LRU_BW, LRU_BW), LRU_BW ** -0.5),
        'lru_ba': nrm((L, LRU_W), 0.02),
        'lru_wx': nrm((L, LRU_BLOCKS, LRU_BW, LRU_BW), LRU_BW ** -0.5),
        'lru_bx': nrm((L, LRU_W), 0.02),
        'gla_w2': nrm((L, GLA_RANK, GLA_HEADS * GLA_DK), GLA_RANK ** -0.5),
        'gla_b2': 1.0 + nrm((L, GLA_HEADS * GLA_DK), 0.5),
        'gla_gnorm': 1.0 + nrm((L, GLA_W), 0.05),
        'w_out': nrm((L, MIX_W, D), MIX_W ** -0.5),
        'w_router': nrm((L, D, N_EXPERTS), D ** -0.5),
        'b_router': nrm((L, N_EXPERTS), 0.01),
        'w_gu': nrm((L, N_EXPERTS, D, 2 * D_FF), D ** -0.5),
        'b_gu': nrm((L, N_EXPERTS, 2 * D_FF), 0.02),
        'w_down': nrm((L, N_EXPERTS, D_FF, D), D_FF ** -0.5),
        'b_down': nrm((L, N_EXPERTS, D), 0.02),
        'g_final': 1.0 + nrm((D,), 0.05),
    }


def reference(x_prompt, x_sample, cache_fox_k, cache_fox_v, cache_fox_logf, state_conv, state_lru, state_gla,
              page_table, c_prompt, c_sample, w_ada, b_ada, g_norm1, g_norm2, w_in, b_forget, conv_w, conv_b,
              lru_lambda, lru_wa, lru_ba, lru_wx, lru_bx, gla_w2, gla_b2, gla_gnorm, w_out, w_router, b_router,
              w_gu, b_gu, w_down, b_down, g_final):
    db = x_sample.shape[0]
    n_past = page_table.shape[1] * PAGE_SIZE
    yp, ys = x_prompt, x_sample
    st_p, st_s = [], []
    for l in range(DEPTH):
        lp = {
            'w_ada': w_ada[l], 'b_ada': b_ada[l], 'g1': g_norm1[l], 'g2': g_norm2[l],
            'w_in': w_in[l], 'b_forget': b_forget[l], 'conv_w': conv_w[l], 'conv_b': conv_b[l],
            'lam': lru_lambda[l], 'wa': lru_wa[l], 'ba': lru_ba[l], 'wx': lru_wx[l], 'bx': lru_bx[l],
            'gla_w2': gla_w2[l], 'gla_b2': gla_b2[l], 'gla_g': gla_gnorm[l], 'w_out': w_out[l],
            'w_router': w_router[l], 'b_router': b_router[l], 'w_gu': w_gu[l], 'b_gu': b_gu[l],
            'w_down': w_down[l], 'b_down': b_down[l],
        }
        yp, sp = trunk_layer(yp, c_prompt, lp, None)
        past = (
            cache_fox_k[l][page_table].reshape(db, n_past, FOX_HEADS, HEAD_DIM),
            cache_fox_v[l][page_table].reshape(db, n_past, FOX_HEADS, HEAD_DIM),
            cache_fox_logf[l][page_table].reshape(db, n_past, FOX_HEADS),
            state_conv[l], state_lru[l], state_gla[l],
        )
        ys, ss = trunk_layer(ys, c_sample, lp, past)
        st_p.append(sp)
        st_s.append(ss)
    yp = rmsnorm(yp, g_final)
    ys = rmsnorm(ys, g_final)

    def stack(sts, i):
        return jnp.stack([s[i] for s in sts])

    return (yp, ys,
            stack(st_p, 0), stack(st_p, 1), stack(st_p, 2), stack(st_p, 3), stack(st_p, 4), stack(st_p, 5),
            stack(st_s, 0), stack(st_s, 1), stack(st_s, 2), stack(st_s, 3), stack(st_s, 4), stack(st_s, 5))
```

```python
import functools

import jax
import jax.numpy as jnp
from jax import lax
from jax.experimental import pallas as pl
from jax.experimental.pallas import tpu as pltpu

F32 = jnp.float32
BF16 = jnp.bfloat16

D_MODEL = 1024
HEAD_DIM = 64
FOX_W = 512
FOX_HEADS = 8
LRU_W = 256
LRU_BLOCKS = 4
GLA_W = 256
GLA_HEADS = 4
GLA_RANK = 16
GLA_CHUNK = 64
GLA_TAU = 16.0
LRU_C = 8.0
CONV_W = 4
N_EXPERTS = 32
TOP_K = 4
D_FF = 1024
SWIGLU_LIMIT = 7.0
SWIGLU_ALPHA = 1.702
NORM_EPS = 1e-6
PAGE_SIZE = 128
QK_SCALE = HEAD_DIM ** -0.5

_O_FQ, _O_FK, _O_FV, _O_FF = 0, 512, 1024, 1536
_O_LX, _O_LG, _O_GQ, _O_GK, _O_GV, _O_GA, _O_GOG = 1544, 1800, 2056, 2312, 2568, 2824, 2840
IN_W = 3096
REC_W = 1792
SMALL_W = 128

VMEM_LIMIT = 56 * 1024 * 1024
MOE_BLOCK = 256

_NT = (((1,), (1,)), ((), ()))
_TN = (((0,), (0,)), ((), ()))


def _cparams(sem, vmem=VMEM_LIMIT):
    return pltpu.CompilerParams(dimension_semantics=sem, vmem_limit_bytes=vmem)


def _log_sigmoid(x):
    return jnp.minimum(x, 0.0) - jnp.log1p(jnp.exp(-jnp.abs(x)))


def _softplus(x):
    return jnp.maximum(x, 0.0) + jnp.log1p(jnp.exp(-jnp.abs(x)))


def _cumsum(x, axis):
    n = x.shape[axis]
    idx = lax.broadcasted_iota(jnp.int32, x.shape, axis)
    s = 1
    while s < n:
        x = x + jnp.where(idx >= s, pltpu.roll(x, s, axis), 0.0)
        s *= 2
    return x


def _ada_kernel(c_ref, w_ref, b_ref, o_ref):
    c = c_ref[...]
    a = (c * jax.nn.sigmoid(c)).astype(BF16)
    o_ref[...] = jnp.dot(a, w_ref[...].astype(BF16), preferred_element_type=F32) + b_ref[...]


def _ada_call(c_all, w_ada, b_ada):
    n_layers, d, w = w_ada.shape
    r = c_all.shape[0]
    tn = 1536
    return pl.pallas_call(
        _ada_kernel,
        grid=(n_layers, w // tn),
        in_specs=[
            pl.BlockSpec((r, d), lambda l, j: (0, 0)),
            pl.BlockSpec((None, d, tn), lambda l, j: (l, 0, j)),
            pl.BlockSpec((None, 1, tn), lambda l, j: (l, 0, j)),
        ],
        out_specs=pl.BlockSpec((None, r, tn), lambda l, j: (l, 0, j)),
        out_shape=jax.ShapeDtypeStruct((n_layers, r, w), F32),
        compiler_params=_cparams(("arbitrary", "arbitrary")),
        name="ada_mod",
    )(c_all, w_ada, b_ada.reshape(n_layers, 1, w))


def _inproj_kernel(y_ref, mod_ref, g_ref, wqkv_ref, wrec_ref, wsm_ref, wfft_ref, bsm_ref, bfc_ref, w2_ref, b2_ref,
                   q_ref, k_ref, v_ref, logf_ref, cum_ref, cumt_ref, rz_ref, carry_c, carry_r, *, tiles_per_seq):
    i = pl.program_id(0)

    @pl.when(i % tiles_per_seq == 0)
    def _():
        carry_c[...] = jnp.zeros_like(carry_c)
        carry_r[...] = jnp.zeros_like(carry_r)

    x = y_ref[...]
    xn = x * lax.rsqrt(jnp.mean(x * x, axis=-1, keepdims=True) + NORM_EPS) * g_ref[...]
    h = (xn * (1.0 + mod_ref[1]) + mod_ref[0]).astype(BF16)

    q_ref[...] = (jnp.dot(h, wqkv_ref[:, 0:FOX_W], preferred_element_type=F32) * QK_SCALE).astype(BF16)
    k_ref[...] = jnp.dot(h, wqkv_ref[:, FOX_W:2 * FOX_W], preferred_element_type=F32)
    v_ref[...] = jnp.dot(h, wqkv_ref[:, 2 * FOX_W:3 * FOX_W], preferred_element_type=F32)
    rz_ref[:, 0:REC_W - GLA_W] = jnp.dot(h, wrec_ref[...], preferred_element_type=F32)

    sm = jnp.dot(h, wsm_ref[...], preferred_element_type=F32)
    lane = lax.broadcasted_iota(jnp.int32, sm.shape, 1)
    logf = jnp.where(lane < FOX_HEADS, _log_sigmoid(sm + bsm_ref[...]), 0.0)
    cum = _cumsum(logf, 0) + carry_c[...]
    carry_c[...] = cum[cum.shape[0] - 1:, :]
    logf_ref[...] = logf[:, 0:FOX_HEADS]
    cum_ref[...] = cum[:, 0:FOX_HEADS]

    zt = lax.dot_general(wfft_ref[...], h, _NT, preferred_element_type=F32)
    cumt = _cumsum(_log_sigmoid(zt + bfc_ref[...]), 1) + carry_r[...]
    carry_r[...] = cumt[:, cumt.shape[1] - 1:]
    cumt_ref[...] = cumt

    glin = jnp.dot(sm.astype(BF16), w2_ref[...], preferred_element_type=F32) + b2_ref[...]
    rz_ref[:, REC_W - GLA_W:REC_W] = _log_sigmoid(glin) * (1.0 / GLA_TAU)


def _inproj_call(y, mod, g, wts, seq_len, tm):
    n, d = y.shape
    tiles_per_seq = seq_len // tm
    n_seq = n // seq_len
    per_row = mod.shape[2] != 1
    if per_row:
        mod_spec = pl.BlockSpec((6, None, tm, d), lambda i: (0, 0, i, 0))
    else:
        mod_spec = pl.BlockSpec((6, None, 1, d), lambda i: (0, i // tiles_per_seq, 0, 0))
    const = lambda i: (0, 0)
    row = lambda i: (i, 0)
    out_shapes = (
        jax.ShapeDtypeStruct((n, FOX_W), BF16),
        jax.ShapeDtypeStruct((n, FOX_W), F32),
        jax.ShapeDtypeStruct((n, FOX_W), F32),
        jax.ShapeDtypeStruct((n, FOX_HEADS), F32),
        jax.ShapeDtypeStruct((n, FOX_HEADS), F32),
        jax.ShapeDtypeStruct((n_seq, FOX_HEADS, seq_len), F32),
        jax.ShapeDtypeStruct((n, REC_W), F32),
    )
    return pl.pallas_call(
        functools.partial(_inproj_kernel, tiles_per_seq=tiles_per_seq),
        grid=(n // tm,),
        in_specs=[
            pl.BlockSpec((tm, d), row),
            mod_spec,
            pl.BlockSpec((1, d), const),
            pl.BlockSpec((d, 3 * FOX_W), const),
            pl.BlockSpec((d, REC_W - GLA_W), const),
            pl.BlockSpec((d, SMALL_W), const),
            pl.BlockSpec((FOX_HEADS, d), const),
            pl.BlockSpec((1, SMALL_W), const),
            pl.BlockSpec((FOX_HEADS, 1), const),
            pl.BlockSpec((SMALL_W, GLA_W), const),
            pl.BlockSpec((1, GLA_W), const),
        ],
        out_specs=(
            pl.BlockSpec((tm, FOX_W), row),
            pl.BlockSpec((tm, FOX_W), row),
            pl.BlockSpec((tm, FOX_W), row),
            pl.BlockSpec((tm, FOX_HEADS), row),
            pl.BlockSpec((tm, FOX_HEADS), row),
            pl.BlockSpec((None, FOX_HEADS, tm), lambda i: (i // tiles_per_seq, 0, i % tiles_per_seq)),
            pl.BlockSpec((tm, REC_W), row),
        ),
        out_shape=out_shapes,
        scratch_shapes=[pltpu.VMEM((1, SMALL_W), F32), pltpu.VMEM((FOX_HEADS, 1), F32)],
        compiler_params=_cparams(("arbitrary",)),
        name="in_proj",
    )(y, mod, g, wts["w_qkv"], wts["w_rec"], wts["w_sm"], wts["w_fft"], wts["b_sm"], wts["b_fc"], wts["w2"], wts["b2"])


def _fox_kernel(q_ref, k_ref, v_ref, cq_ref, ck_ref, o_ref, m_sc, l_sc, acc_sc, *, tq, tk):
    qi = pl.program_id(2)
    ki = pl.program_id(3)

    @pl.when(ki == 0)
    def _():
        m_sc[...] = jnp.full_like(m_sc, -jnp.inf)
        l_sc[...] = jnp.zeros_like(l_sc)
        acc_sc[...] = jnp.zeros_like(acc_sc)

    @pl.when(ki <= qi)
    def _():
        q = q_ref[...]
        k = k_ref[...].astype(BF16)
        v = v_ref[...].astype(BF16)
        lane = lax.broadcasted_iota(jnp.int32, q.shape, 1)
        row = lax.broadcasted_iota(jnp.int32, (tq, tk), 0) + qi * tq
        col = lax.broadcasted_iota(jnp.int32, (tq, tk), 1) + ki * tk
        causal = col <= row
        for j in range(2):
            qm = jnp.where(lane // HEAD_DIM == j, q, jnp.zeros_like(q))
            s = lax.dot_general(qm, k, _NT, preferred_element_type=F32)
            s = s + cq_ref[:, j:j + 1] - ck_ref[j:j + 1, :]
            s = jnp.where(causal, s, -jnp.inf)
            m_prev = m_sc[j]
            m_new = jnp.maximum(m_prev, jnp.max(s, axis=1, keepdims=True))
            p = jnp.exp(s - m_new)
            alpha = jnp.exp(m_prev - m_new)
            l_sc[j] = alpha * l_sc[j] + jnp.sum(p, axis=1, keepdims=True)
            acc_sc[j] = alpha * acc_sc[j] + jnp.dot(p.astype(BF16), v, preferred_element_type=F32)
            m_sc[j] = m_new

    @pl.when(ki == qi)
    def _():
        lane = lax.broadcasted_iota(jnp.int32, (tq, 2 * HEAD_DIM), 1)
        o0 = acc_sc[0] / l_sc[0]
        o1 = acc_sc[1] / l_sc[1]
        o_ref[...] = jnp.where(lane < HEAD_DIM, o0, o1).astype(BF16)


def _fox_call(q, k, v, cum2, cumt2, n_seq, seq_len, tq):
    n = q.shape[0]
    nt = seq_len // tq
    tk = tq
    hp = FOX_HEADS // 2
    w = 2 * HEAD_DIM
    return pl.pallas_call(
        functools.partial(_fox_kernel, tq=tq, tk=tk),
        grid=(n_seq, hp, nt, nt),
        in_specs=[
            pl.BlockSpec((tq, w), lambda b, h, i, j: (b * nt + i, h)),
            pl.BlockSpec((tk, w), lambda b, h, i, j: (b * nt + jnp.minimum(i, j), h)),
            pl.BlockSpec((tk, w), lambda b, h, i, j: (b * nt + jnp.minimum(i, j), h)),
            pl.BlockSpec((None, tq, 2), lambda b, h, i, j: (h, b * nt + i, 0)),
            pl.BlockSpec((None, None, 2, tk), lambda b, h, i, j: (b, h, 0, jnp.minimum(i, j))),
        ],
        out_specs=pl.BlockSpec((tq, w), lambda b, h, i, j: (b * nt + i, h)),
        out_shape=jax.ShapeDtypeStruct((n, FOX_W), BF16),
        scratch_shapes=[pltpu.VMEM((2, tq, 1), F32), pltpu.VMEM((2, tq, 1), F32), pltpu.VMEM((2, tq, w), F32)],
        compiler_params=_cparams(("arbitrary", "arbitrary", "arbitrary", "arbitrary")),
        name="fox_prompt",
    )(q, k, v, cum2, cumt2)


def _lru_gates(xc, wa_ref, ba_ref, wx_ref, bx_ref, lam_ref):
    xb = xc.astype(BF16)
    r = jax.nn.sigmoid(jnp.dot(xb, wa_ref[...], preferred_element_type=F32) + ba_ref[...])
    gi = jax.nn.sigmoid(jnp.dot(xb, wx_ref[...], preferred_element_type=F32) + bx_ref[...])
    log_a = -LRU_C * r * _softplus(-lam_ref[...])
    a = jnp.exp(log_a)
    mult = jnp.sqrt(-jnp.tanh(log_a) * (a * a + 1.0))
    return a, mult, gi


def _lru_kernel(lx_ref, lg_ref, cw_ref, cb_ref, wa_ref, ba_ref, wx_ref, bx_ref, lam_ref,
                lo_ref, conv_ref, hlast_ref, xbuf, hcar, *, tt):
    ti = pl.program_id(1)
    nt = pl.num_programs(1)

    @pl.when(ti == 0)
    def _():
        xbuf[0:8, :] = jnp.zeros((8, LRU_W), F32)
        hcar[...] = jnp.zeros_like(hcar)

    x = lx_ref[...]
    xbuf[8:8 + tt, :] = x
    xc = cb_ref[...] + cw_ref[3:4, :] * x
    for j in range(CONV_W - 1):
        xc = xc + cw_ref[j:j + 1, :] * xbuf[5 + j:5 + j + tt, :]
    xbuf[0:8, :] = x[tt - 8:tt, :]

    a, mult, gi = _lru_gates(xc, wa_ref, ba_ref, wx_ref, bx_ref, lam_ref)
    row = lax.broadcasted_iota(jnp.int32, (tt, LRU_W), 0)
    mult = jnp.where((row == 0) & (ti == 0), 1.0, mult)
    b = mult * gi * xc
    s = 1
    while s < tt:
        keep = row >= s
        a_sh = jnp.where(keep, pltpu.roll(a, s, 0), 1.0)
        b_sh = jnp.where(keep, pltpu.roll(b, s, 0), 0.0)
        b = a * b_sh + b
        a = a * a_sh
        s *= 2
    h = a * hcar[...] + b
    hcar[...] = h[tt - 1:tt, :]
    lo_ref[...] = (h * jax.nn.gelu(lg_ref[...])).astype(BF16)

    @pl.when(ti == nt - 1)
    def _():
        conv_ref[...] = x[tt - (CONV_W - 1):tt, :]
        hlast_ref[...] = h[tt - 1:tt, :]


def _lru_call(rz, lw, n_seq, seq_len, tt):
    n = rz.shape[0]
    nt = seq_len // tt
    const = lambda b, t: (0, 0)
    return pl.pallas_call(
        functools.partial(_lru_kernel, tt=tt),
        grid=(n_seq, nt),
        in_specs=[
            pl.BlockSpec((tt, LRU_W), lambda b, t: (b * nt + t, 0)),
            pl.BlockSpec((tt, LRU_W), lambda b, t: (b * nt + t, 1)),
            pl.BlockSpec((CONV_W, LRU_W), const),
            pl.BlockSpec((1, LRU_W), const),
            pl.BlockSpec((LRU_W, LRU_W), const),
            pl.BlockSpec((1, LRU_W), const),
            pl.BlockSpec((LRU_W, LRU_W), const),
            pl.BlockSpec((1, LRU_W), const),
            pl.BlockSpec((1, LRU_W), const),
        ],
        out_specs=(
            pl.BlockSpec((tt, LRU_W), lambda b, t: (b * nt + t, 0)),
            pl.BlockSpec((None, CONV_W - 1, LRU_W), lambda b, t: (b, 0, 0)),
            pl.BlockSpec((None, 1, LRU_W), lambda b, t: (b, 0, 0)),
        ),
        out_shape=(
            jax.ShapeDtypeStruct((n, LRU_W), BF16),
            jax.ShapeDtypeStruct((n_seq, CONV_W - 1, LRU_W), F32),
            jax.ShapeDtypeStruct((n_seq, 1, LRU_W), F32),
        ),
        scratch_shapes=[pltpu.VMEM((tt + 8, LRU_W), F32), pltpu.VMEM((1, LRU_W), F32)],
        compiler_params=_cparams(("arbitrary", "arbitrary")),
        name="lru_prompt",
    )(rz, rz, lw["conv_w"], lw["conv_b"], lw["wa"], lw["ba"], lw["wx"], lw["bx"], lw["lam"])


def _head_rms_gate(o, gg_ref, gog):
    lane = lax.broadcasted_iota(jnp.int32, o.shape, 1)
    o2 = o * o
    rs = jnp.zeros_like(o)
    for hh in range(GLA_HEADS):
        mh = lane // HEAD_DIM == hh
        ms = jnp.sum(jnp.where(mh, o2, 0.0), axis=1, keepdims=True) * (1.0 / HEAD_DIM)
        rs = jnp.where(mh, lax.rsqrt(ms + NORM_EPS), rs)
    return o * rs * gg_ref[...] * (gog * jax.nn.sigmoid(gog))


def _gla_kernel(gq_ref, gk_ref, gv_ref, gog_ref, gl_ref, gg_ref, go_ref, st_ref, s_sc, *, tt):
    ti = pl.program_id(1)
    nt = pl.num_programs(1)
    c = GLA_CHUNK

    @pl.when(ti == 0)
    def _():
        s_sc[...] = jnp.zeros_like(s_sc)

    lane = lax.broadcasted_iota(jnp.int32, (c, GLA_W), 1)
    r2 = lax.broadcasted_iota(jnp.int32, (GLA_W, GLA_W), 0)
    c2 = lax.broadcasted_iota(jnp.int32, (GLA_W, GLA_W), 1)
    same_head = (r2 // HEAD_DIM) == (c2 // HEAD_DIM)
    tril = lax.broadcasted_iota(jnp.int32, (c, c), 1) <= lax.broadcasted_iota(jnp.int32, (c, c), 0)

    for ci in range(tt // c):
        sl = slice(ci * c, (ci + 1) * c)
        q = gq_ref[sl, :] * QK_SCALE
        k = gk_ref[sl, :]
        v = gv_ref[sl, :].astype(BF16)
        bc = _cumsum(gl_ref[sl, :], 0)
        b_last = bc[c - 1:c, :]
        qd = (q * jnp.exp(bc)).astype(BF16)
        kinv = (k * jnp.exp(-bc)).astype(BF16)
        kdec = (k * jnp.exp(b_last - bc)).astype(BF16)
        s_prev = s_sc[...]
        o = lax.dot_general(qd, s_prev.astype(BF16), _NT, preferred_element_type=F32)
        for hh in range(GLA_HEADS):
            mh = lane // HEAD_DIM == hh
            att = lax.dot_general(jnp.where(mh, qd, jnp.zeros_like(qd)), kinv, _NT, preferred_element_type=F32)
            att = jnp.where(tril, att, 0.0).astype(BF16)
            o = o + jnp.dot(att, jnp.where(mh, v, jnp.zeros_like(v)), preferred_element_type=F32)
        ut = lax.dot_general(v, kdec, _TN, preferred_element_type=F32)
        s_sc[...] = s_prev * jnp.exp(b_last) + jnp.where(same_head, ut, 0.0)
        go_ref[sl, :] = _head_rms_gate(o, gg_ref, gog_ref[sl, :]).astype(BF16)

    @pl.when(ti == nt - 1)
    def _():
        st_ref[...] = s_sc[...]


def _gla_call(rz, gg, n_seq, seq_len, tt):
    n = rz.shape[0]
    nt = seq_len // tt

    def col(j):
        return pl.BlockSpec((tt, GLA_W), lambda b, t: (b * nt + t, j))

    return pl.pallas_call(
        functools.partial(_gla_kernel, tt=tt),
        grid=(n_seq, nt),
        in_specs=[col(2), col(3), col(4), col(5), col(6), pl.BlockSpec((1, GLA_W), lambda b, t: (0, 0))],
        out_specs=(
            pl.BlockSpec((tt, GLA_W), lambda b, t: (b * nt + t, 0)),
            pl.BlockSpec((None, GLA_W, GLA_W), lambda b, t: (b, 0, 0)),
        ),
        out_shape=(
            jax.ShapeDtypeStruct((n, GLA_W), BF16),
            jax.ShapeDtypeStruct((n_seq, GLA_W, GLA_W), F32),
        ),
        scratch_shapes=[pltpu.VMEM((GLA_W, GLA_W), F32)],
        compiler_params=_cparams(("arbitrary", "arbitrary")),
        name="gla_prompt",
    )(rz, rz, rz, rz, rz, gg)


def _fox_dec_kernel(pt_ref, q_ref, kn_ref, vn_ref, dn_ref, *refs, n_pages):
    del pt_ref
    k_refs = refs[0:n_pages]
    v_refs = refs[n_pages:2 * n_pages]
    f_refs = refs[2 * n_pages:3 * n_pages]
    o_ref = refs[3 * n_pages]
    w = FOX_W
    hrow = lax.broadcasted_iota(jnp.int32, (FOX_HEADS, w), 0)
    hlane = lax.broadcasted_iota(jnp.int32, (FOX_HEADS, w), 1) // HEAD_DIM
    diag = hrow == hlane
    q = q_ref[...].astype(F32)
    qbd = jnp.where(diag, jnp.broadcast_to(q, (FOX_HEADS, w)), 0.0).astype(BF16)
    eye = (lax.broadcasted_iota(jnp.int32, (FOX_HEADS, FOX_HEADS), 0)
           == lax.broadcasted_iota(jnp.int32, (FOX_HEADS, FOX_HEADS), 1)).astype(F32)
    s_parts = []
    f_parts = []
    for p in range(n_pages):
        s_parts.append(lax.dot_general(qbd, k_refs[p][...].astype(BF16), _NT, preferred_element_type=F32))
        f_parts.append(lax.dot_general(eye, f_refs[p][...], _NT, preferred_element_type=F32,
                                       precision=lax.Precision.HIGHEST))
    s = jnp.concatenate(s_parts, axis=1)
    lf = jnp.concatenate(f_parts, axis=1)
    cs = _cumsum(lf, 1)
    suffix = cs[:, cs.shape[1] - 1:] - cs
    s = s + dn_ref[...] + suffix
    s_new = jnp.sum(qbd.astype(F32) * kn_ref[...], axis=1, keepdims=True)
    m = jnp.maximum(jnp.max(s, axis=1, keepdims=True), s_new)
    p_past = jnp.exp(s - m)
    p_new = jnp.exp(s_new - m)
    denom = jnp.sum(p_past, axis=1, keepdims=True) + p_new
    acc = p_new * vn_ref[...]
    pb = p_past.astype(BF16)
    for p in range(n_pages):
        acc = acc + jnp.dot(pb[:, p * PAGE_SIZE:(p + 1) * PAGE_SIZE], v_refs[p][...].astype(BF16),
                            preferred_element_type=F32)
    out = jnp.where(diag, acc / denom, 0.0)
    o_ref[...] = jnp.sum(out, axis=0, keepdims=True).astype(BF16)


def _fox_dec_call(layer, page_table, q, k_new, v_new, logf_new, cache_k, cache_v, cache_f):
    bd, n_pages = page_table.shape
    w = FOX_W

    def page_spec(width, j):
        return pl.BlockSpec((None, None, PAGE_SIZE, width), lambda b, pt, j=j: (layer, pt[b, j], 0, 0))

    row = lambda b, pt: (b, 0, 0)
    in_specs = [
        pl.BlockSpec((None, 1, w), row),
        pl.BlockSpec((None, 1, w), row),
        pl.BlockSpec((None, 1, w), row),
        pl.BlockSpec((None, FOX_HEADS, 1), row),
    ]
    in_specs += [page_spec(w, j) for j in range(n_pages)]
    in_specs += [page_spec(w, j) for j in range(n_pages)]
    in_specs += [page_spec(FOX_HEADS, j) for j in range(n_pages)]
    grid_spec = pltpu.PrefetchScalarGridSpec(
        num_scalar_prefetch=1,
        grid=(bd,),
        in_specs=in_specs,
        out_specs=pl.BlockSpec((None, 1, w), row),
    )
    out = pl.pallas_call(
        functools.partial(_fox_dec_kernel, n_pages=n_pages),
        grid_spec=grid_spec,
        out_shape=jax.ShapeDtypeStruct((bd, 1, w), BF16),
        compiler_params=_cparams(("arbitrary",)),
        name="fox_sample",
    )(page_table, q.reshape(bd, 1, w), k_new.reshape(bd, 1, w), v_new.reshape(bd, 1, w),
      logf_new.reshape(bd, FOX_HEADS, 1),
      *([cache_k] * n_pages), *([cache_v] * n_pages), *([cache_f] * n_pages))
    return out.reshape(bd, w)


def _rec_step_kernel(lx_ref, lg_ref, conv_ref, h0_ref, gt_ref, kt_ref, qt_ref, v2_ref, gog2_ref, s_ref,
                     cw_ref, cb_ref, wa_ref, ba_ref, wx_ref, bx_ref, lam_ref, gg2_ref,
                     lo_ref, convn_ref, hn_ref, go_ref, sn_ref, *, tb):
    x = lx_ref[...]
    xc = cb_ref[...] + cw_ref[3:4, :] * x
    for j in range(CONV_W - 1):
        xc = xc + cw_ref[j:j + 1, :] * conv_ref[j]
    convn_ref[0] = conv_ref[1]
    convn_ref[1] = conv_ref[2]
    convn_ref[2] = x
    a, mult, gi = _lru_gates(xc, wa_ref, ba_ref, wx_ref, bx_ref, lam_ref)
    h = a * h0_ref[...] + mult * gi * xc
    hn_ref[...] = h
    lo_ref[...] = (h * jax.nn.gelu(lg_ref[...])).astype(BF16)

    half = HEAD_DIM
    rows_per_head = HEAD_DIM // 2
    lane = lax.broadcasted_iota(jnp.int32, (2 * HEAD_DIM, 2 * HEAD_DIM), 1)
    lo_half = lane < half
    lane4 = lax.broadcasted_iota(jnp.int32, (GLA_HEADS, 2 * HEAD_DIM), 1)
    eg = jnp.exp(gt_ref[...])
    kt = kt_ref[...]
    qt = qt_ref[...] * QK_SCALE

    def cols(t, b):
        return jnp.where(lo_half, t[0][:, b:b + 1], t[1][:, b:b + 1])

    for b in range(tb):
        vexp = jnp.concatenate(
            [jnp.broadcast_to(v2_ref[b, hh:hh + 1, :], (rows_per_head, 2 * HEAD_DIM)) for hh in range(GLA_HEADS)],
            axis=0)
        s_new = cols(eg, b) * s_ref[b] + cols(kt, b) * vexp
        sn_ref[b] = s_new
        prod = cols(qt, b) * s_new
        red = jnp.concatenate(
            [jnp.sum(prod[hh * rows_per_head:(hh + 1) * rows_per_head, :], axis=0, keepdims=True)
             for hh in range(GLA_HEADS)], axis=0)
        o = red + pltpu.roll(red, half, 1)
        ms = jnp.sum(jnp.where(lane4 < half, o * o, 0.0), axis=1, keepdims=True) * (1.0 / HEAD_DIM)
        gog = gog2_ref[b]
        go_ref[b] = o * lax.rsqrt(ms + NORM_EPS) * gg2_ref[...] * (gog * jax.nn.sigmoid(gog))


def _dup_heads(x):
    lead = x.shape[:-1]
    xh = x.reshape(lead + (GLA_HEADS, 1, HEAD_DIM))
    return jnp.broadcast_to(xh, lead + (GLA_HEADS, 2, HEAD_DIM)).reshape(lead + (GLA_HEADS, 2 * HEAD_DIM))


def _rec_step_call(rz, conv_t, h0, s0, lw, gg, tb):
    bd = rz.shape[0]
    n_steps = bd // tb

    def key_cols(x):
        return x.reshape(n_steps, tb, GLA_W // 2, 2).transpose(0, 3, 2, 1)

    gq, gk, gv = rz[:, 2 * GLA_W:3 * GLA_W], rz[:, 3 * GLA_W:4 * GLA_W], rz[:, 4 * GLA_W:5 * GLA_W]
    gog, gl = rz[:, 5 * GLA_W:6 * GLA_W], rz[:, 6 * GLA_W:7 * GLA_W]
    const = lambda i: (0, 0)
    kc_spec = pl.BlockSpec((None, 2, GLA_W // 2, tb), lambda i: (i, 0, 0, 0))
    hv_spec = pl.BlockSpec((tb, GLA_HEADS, 2 * HEAD_DIM), lambda i: (i, 0, 0))
    st_spec = pl.BlockSpec((tb, 2 * HEAD_DIM, 2 * HEAD_DIM), lambda i: (i, 0, 0))
    wspec = [
        pl.BlockSpec((CONV_W, LRU_W), const), pl.BlockSpec((1, LRU_W), const),
        pl.BlockSpec((LRU_W, LRU_W), const), pl.BlockSpec((1, LRU_W), const),
        pl.BlockSpec((LRU_W, LRU_W), const), pl.BlockSpec((1, LRU_W), const),
        pl.BlockSpec((1, LRU_W), const), pl.BlockSpec((GLA_HEADS, 2 * HEAD_DIM), const),
    ]
    lo, conv_n, h_n, go, s_n = pl.pallas_call(
        functools.partial(_rec_step_kernel, tb=tb),
        grid=(n_steps,),
        in_specs=[pl.BlockSpec((tb, LRU_W), lambda i: (i, 0)),
                  pl.BlockSpec((tb, LRU_W), lambda i: (i, 1)),
                  pl.BlockSpec((CONV_W - 1, tb, LRU_W), lambda i: (0, i, 0)),
                  pl.BlockSpec((tb, LRU_W), lambda i: (i, 0)),
                  kc_spec, kc_spec, kc_spec, hv_spec, hv_spec, st_spec] + wspec,
        out_specs=(
            pl.BlockSpec((tb, LRU_W), lambda i: (i, 0)),
            pl.BlockSpec((CONV_W - 1, tb, LRU_W), lambda i: (0, i, 0)),
            pl.BlockSpec((tb, LRU_W), lambda i: (i, 0)),
            hv_spec,
            st_spec,
        ),
        out_shape=(
            jax.ShapeDtypeStruct((bd, LRU_W), BF16),
            jax.ShapeDtypeStruct((CONV_W - 1, bd, LRU_W), F32),
            jax.ShapeDtypeStruct((bd, LRU_W), F32),
            jax.ShapeDtypeStruct((bd, GLA_HEADS, 2 * HEAD_DIM), F32),
            jax.ShapeDtypeStruct((bd, 2 * HEAD_DIM, 2 * HEAD_DIM), F32),
        ),
        compiler_params=_cparams(("arbitrary",)),
        name="rec_step",
    )(rz, rz, conv_t, h0, key_cols(gl), key_cols(gk), key_cols(gq), _dup_heads(gv), _dup_heads(gog), s0,
      lw["conv_w"], lw["conv_b"], lw["wa"], lw["ba"], lw["wx"], lw["bx"], lw["lam"], _dup_heads(gg[0]))
    return lo, conv_n, h_n, go[:, :, 0:HEAD_DIM].reshape(bd, GLA_W), s_n


def _outproj_kernel(y_ref, fo_ref, lo_ref, go_ref, mod_ref, g2_ref, wo_ref, wr_ref, br_ref, *rest, aliased):
    y1_ref, h2_ref, ti_ref, tw_ref = rest[aliased:aliased + 4]
    m = jnp.dot(fo_ref[...], wo_ref[0:FOX_W, :], preferred_element_type=F32)
    m = m + jnp.dot(lo_ref[...], wo_ref[FOX_W:FOX_W + LRU_W, :], preferred_element_type=F32)
    m = m + jnp.dot(go_ref[...].astype(BF16), wo_ref[FOX_W + LRU_W:, :], preferred_element_type=F32)
    y1 = y_ref[...] + mod_ref[2] * m
    y1_ref[...] = y1
    xn = y1 * lax.rsqrt(jnp.mean(y1 * y1, axis=-1, keepdims=True) + NORM_EPS) * g2_ref[...]
    h2 = (xn * (1.0 + mod_ref[4]) + mod_ref[3]).astype(BF16)
    h2_ref[...] = h2
    logits = jnp.dot(h2, wr_ref[...], preferred_element_type=F32) + br_ref[...]
    lane = lax.broadcasted_iota(jnp.int32, logits.shape, 1)
    logits = jnp.where(lane < N_EXPERTS, logits, -jnp.inf)
    idx_out = jnp.zeros(logits.shape, jnp.int32)
    val_out = jnp.zeros(logits.shape, F32)
    vals = []
    for kk in range(TOP_K):
        mx = jnp.max(logits, axis=1, keepdims=True)
        sel = jnp.min(jnp.where(logits == mx, lane, SMALL_W), axis=1, keepdims=True)
        idx_out = jnp.where(lane == kk, sel, idx_out)
        vals.append(mx)
        logits = jnp.where(lane == sel, -jnp.inf, logits)
    es = [jnp.exp(vv - vals[0]) for vv in vals]
    tot = es[0] + es[1] + es[2] + es[3]
    for kk in range(TOP_K):
        val_out = jnp.where(lane == kk, es[kk] / tot, val_out)
    ti_ref[...] = idx_out
    tw_ref[...] = val_out


def _outproj_call(y, fo, lo, go, mod, g2, w_out, w_r, b_r, tm, seq_len, n_total, row_off, prev=None):
    n, d = y.shape
    per_row = mod.shape[2] != 1
    tiles_per_seq = max(seq_len // tm, 1)
    if per_row:
        mod_spec = pl.BlockSpec((6, None, tm, d), lambda i: (0, 0, i, 0))
    else:
        mod_spec = pl.BlockSpec((6, None, 1, d), lambda i: (0, i // tiles_per_seq, 0, 0))
    const = lambda i: (0, 0)
    row = lambda i: (i, 0)
    off = row_off // tm
    orow = lambda i: (i + off, 0)
    in_specs = [
        pl.BlockSpec((tm, d), row),
        pl.BlockSpec((tm, FOX_W), row),
        pl.BlockSpec((tm, LRU_W), row),
        pl.BlockSpec((tm, GLA_W), row),
        mod_spec,
        pl.BlockSpec((1, d), const),
        pl.BlockSpec((d, d), const),
        pl.BlockSpec((d, SMALL_W), const),
        pl.BlockSpec((1, SMALL_W), const),
    ]
    args = [y, fo, lo, go, mod, g2, w_out, w_r, b_r]
    aliases = {}
    if prev is not None:
        in_specs += [pl.BlockSpec(memory_space=pl.ANY)] * 3
        args += list(prev)
        aliases = {9: 1, 10: 2, 11: 3}
    return pl.pallas_call(
        functools.partial(_outproj_kernel, aliased=0 if prev is None else 3),
        grid=(n // tm,),
        in_specs=in_specs,
        out_specs=(
            pl.BlockSpec((tm, d), row),
            pl.BlockSpec((tm, d), orow),
            pl.BlockSpec((tm, SMALL_W), orow),
            pl.BlockSpec((tm, SMALL_W), orow),
        ),
        out_shape=(
            jax.ShapeDtypeStruct((n, d), F32),
            jax.ShapeDtypeStruct((n_total, d), BF16),
            jax.ShapeDtypeStruct((n_total, SMALL_W), jnp.int32),
            jax.ShapeDtypeStruct((n_total, SMALL_W), F32),
        ),
        input_output_aliases=aliases,
        compiler_params=_cparams(("arbitrary",)),
        name="out_proj_router",
    )(*args)


def _ffn_kernel(be_ref, nu_ref, x_ref, wgu_ref, bgu_ref, wdn_ref, bdn_ref, y_ref, wgu_b, wdn_b):
    j = pl.program_id(0)
    e = be_ref[j]
    prev = be_ref[jnp.maximum(j - 1, 0)]

    @pl.when((j == 0) | (e != prev))
    def _():
        wgu_b[...] = wgu_ref[...].astype(BF16)
        wdn_b[...] = wdn_ref[...].astype(BF16)

    @pl.when(j < nu_ref[0])
    def _():
        gu = jnp.dot(x_ref[...], wgu_b[...], preferred_element_type=F32) + bgu_ref[...]
        g = jnp.minimum(gu[:, 0:D_FF], SWIGLU_LIMIT)
        u = jnp.clip(gu[:, D_FF:], -SWIGLU_LIMIT, SWIGLU_LIMIT)
        act = g * jax.nn.sigmoid(SWIGLU_ALPHA * g)
        hmid = ((u + 1.0) * act).astype(BF16)
        y_ref[...] = jnp.dot(hmid, wdn_b[...], preferred_element_type=F32) + bdn_ref[...]

    @pl.when(j >= nu_ref[0])
    def _():
        y_ref[...] = jnp.zeros_like(y_ref)


def _ffn_call(layer, blk_e, n_used, xs, w_gu, b_gu, w_dn, b_dn):
    n_rows, d = xs.shape
    bm = MOE_BLOCK
    grid_spec = pltpu.PrefetchScalarGridSpec(
        num_scalar_prefetch=2,
        grid=(n_rows // bm,),
        in_specs=[
            pl.BlockSpec((bm, d), lambda j, be, nu: (j, 0)),
            pl.BlockSpec((None, None, d, 2 * D_FF), lambda j, be, nu: (layer, be[j], 0, 0)),
            pl.BlockSpec((None, None, 1, 2 * D_FF), lambda j, be, nu: (layer, be[j], 0, 0)),
            pl.BlockSpec((None, None, D_FF, d), lambda j, be, nu: (layer, be[j], 0, 0)),
            pl.BlockSpec((None, None, 1, d), lambda j, be, nu: (layer, be[j], 0, 0)),
        ],
        out_specs=pl.BlockSpec((bm, d), lambda j, be, nu: (j, 0)),
        scratch_shapes=[pltpu.VMEM((d, 2 * D_FF), BF16), pltpu.VMEM((D_FF, d), BF16)],
    )
    return pl.pallas_call(
        _ffn_kernel,
        grid_spec=grid_spec,
        out_shape=jax.ShapeDtypeStruct((n_rows, d), F32),
        compiler_params=_cparams(("arbitrary",)),
        name="expert_ffn",
    )(blk_e, n_used, xs, w_gu, b_gu, w_dn, b_dn)


def _combine_kernel(y1_ref, yk_ref, tw_ref, mod_ref, gf_ref, o_ref, *, final):
    tw = tw_ref[...]
    acc = tw[:, 0:1] * yk_ref[0]
    for kk in range(1, TOP_K):
        acc = acc + tw[:, kk:kk + 1] * yk_ref[kk]
    y2 = y1_ref[...] + mod_ref[5] * acc
    if final:
        y2 = y2 * lax.rsqrt(jnp.mean(y2 * y2, axis=-1, keepdims=True) + NORM_EPS) * gf_ref[...]
    o_ref[...] = y2


def _combine_call(y1, yk, tw, mod, g_final, tm, seq_len, row_off, final):
    n, d = y1.shape
    per_row = mod.shape[2] != 1
    tiles_per_seq = max(seq_len // tm, 1)
    if per_row:
        mod_spec = pl.BlockSpec((6, None, tm, d), lambda i: (0, 0, i, 0))
    else:
        mod_spec = pl.BlockSpec((6, None, 1, d), lambda i: (0, i // tiles_per_seq, 0, 0))
    off = row_off // tm
    return pl.pallas_call(
        functools.partial(_combine_kernel, final=final),
        grid=(n // tm,),
        in_specs=[
            pl.BlockSpec((tm, d), lambda i: (i, 0)),
            pl.BlockSpec((TOP_K, tm, d), lambda i: (0, i + off, 0)),
            pl.BlockSpec((tm, SMALL_W), lambda i: (i + off, 0)),
            mod_spec,
            pl.BlockSpec((1, d), lambda i: (0, 0)),
        ],
        out_specs=pl.BlockSpec((tm, d), lambda i: (i, 0)),
        out_shape=jax.ShapeDtypeStruct((n, d), F32),
        compiler_params=_cparams(("arbitrary",)),
        name="moe_combine",
    )(y1, yk, tw, mod, g_final)


def _routing_plan(top_i, n_blocks):
    bm = MOE_BLOCK
    n_asg = top_i.size
    flat_e = top_i.reshape(-1)
    order = jnp.argsort(flat_e)
    e_sorted = flat_e[order]
    counts = jnp.bincount(flat_e, length=N_EXPERTS)
    padded = (counts + bm - 1) // bm * bm
    pad_end = jnp.cumsum(padded)
    pad_start = pad_end - padded
    grp_start = jnp.cumsum(counts) - counts
    dest_sorted = (pad_start[e_sorted] + jnp.arange(n_asg) - grp_start[e_sorted]).astype(jnp.int32)
    row_tok = jnp.zeros((n_blocks * bm,), jnp.int32).at[dest_sorted].set((order // TOP_K).astype(jnp.int32))
    pos = jnp.zeros((n_asg,), jnp.int32).at[order].set(dest_sorted)
    blk_e = jnp.minimum(jnp.searchsorted(pad_end, jnp.arange(n_blocks) * bm, side="right"),
                        N_EXPERTS - 1).astype(jnp.int32)
    n_used = (pad_end[-1] // bm).astype(jnp.int32).reshape(1)
    return row_tok, pos, blk_e, n_used


def _block_diag(w):
    nb, bw, _ = w.shape
    eye = jnp.eye(nb, dtype=w.dtype)
    return (eye[:, None, :, None] * w[:, :, None, :]).reshape(nb * bw, nb * bw)


def _layer_weights(l, w_in, b_forget, conv_w, conv_b, lru_lambda, lru_wa, lru_ba, lru_wx, lru_bx,
                   gla_w2, gla_b2, gla_gnorm, w_out, w_router, b_router):
    wi = w_in[l]
    d = wi.shape[0]
    w_ff = wi[:, _O_FF:_O_LX]
    w2 = jnp.zeros((SMALL_W, GLA_W), F32).at[FOX_HEADS:FOX_HEADS + GLA_RANK].set(gla_w2[l])
    inw = {
        "w_qkv": wi[:, _O_FQ:_O_FF].astype(BF16),
        "w_rec": jnp.concatenate([wi[:, _O_LX:_O_GA], wi[:, _O_GOG:]], axis=1).astype(BF16),
        "w_sm": jnp.concatenate([w_ff, wi[:, _O_GA:_O_GOG],
                                 jnp.zeros((d, SMALL_W - FOX_HEADS - GLA_RANK), F32)], axis=1).astype(BF16),
        "w_fft": w_ff.T.astype(BF16),
        "b_sm": jnp.zeros((1, SMALL_W), F32).at[0, 0:FOX_HEADS].set(b_forget[l]),
        "b_fc": b_forget[l].reshape(FOX_HEADS, 1),
        "w2": w2.astype(BF16),
        "b2": gla_b2[l].reshape(1, GLA_W),
    }
    lw = {
        "conv_w": conv_w[l], "conv_b": conv_b[l].reshape(1, LRU_W),
        "wa": _block_diag(lru_wa[l]).astype(BF16), "ba": lru_ba[l].reshape(1, LRU_W),
        "wx": _block_diag(lru_wx[l]).astype(BF16), "bx": lru_bx[l].reshape(1, LRU_W),
        "lam": lru_lambda[l].reshape(1, LRU_W),
    }
    ow = {
        "gg": gla_gnorm[l].reshape(1, GLA_W),
        "w_out": w_out[l].astype(BF16),
        "w_r": jnp.concatenate([w_router[l], jnp.zeros((d, SMALL_W - N_EXPERTS), F32)], axis=1).astype(BF16),
        "b_r": jnp.zeros((1, SMALL_W), F32).at[0, 0:N_EXPERTS].set(b_router[l]),
    }
    return inw, lw, ow


def kernel(x_prompt, x_sample, cache_fox_k, cache_fox_v, cache_fox_logf, state_conv, state_lru, state_gla, page_table, c_prompt, c_sample, w_ada, b_ada, g_norm1, g_norm2, w_in, b_forget, conv_w, conv_b, lru_lambda, lru_wa, lru_ba, lru_wx, lru_bx, gla_w2, gla_b2, gla_gnorm, w_out, w_router, b_router, w_gu, b_gu, w_down, b_down, g_final):
    n_layers = w_ada.shape[0]
    bp, seq, d = x_prompt.shape
    bs = x_sample.shape[0]
    n_p = bp * seq
    n_tot = n_p + bs
    n_pool = cache_fox_k.shape[1]

    mod = _ada_call(jnp.concatenate([c_prompt, c_sample], axis=0), w_ada, b_ada)
    mod_p = mod[:, :bp].reshape(n_layers, bp, 6, 1, d).transpose(0, 2, 1, 3, 4)
    mod_s = mod[:, bp:].reshape(n_layers, 1, bs, 6, d).transpose(0, 3, 1, 2, 4)

    ck = cache_fox_k.reshape(n_layers, n_pool, PAGE_SIZE, FOX_W)
    cv = cache_fox_v.reshape(n_layers, n_pool, PAGE_SIZE, FOX_W)
    b_gu4 = b_gu.reshape(n_layers, N_EXPERTS, 1, 2 * D_FF)
    b_dn4 = b_down.reshape(n_layers, N_EXPERTS, 1, d)
    gf = g_final.reshape(1, d)

    n_asg = n_tot * TOP_K
    n_blocks = -(-(n_asg + N_EXPERTS * (MOE_BLOCK - 1)) // MOE_BLOCK)

    yp = x_prompt.reshape(n_p, d)
    ys = x_sample.reshape(bs, d)
    outs_p = [[] for _ in range(6)]
    outs_s = [[] for _ in range(6)]
    for l in range(n_layers):
        inw, lw, ow = _layer_weights(l, w_in, b_forget, conv_w, conv_b, lru_lambda, lru_wa, lru_ba, lru_wx,
                                     lru_bx, gla_w2, gla_b2, gla_gnorm, w_out, w_router, b_router)
        g1 = g_norm1[l].reshape(1, d)
        g2 = g_norm2[l].reshape(1, d)

        q, k, v, logf, cum, cumt, rz = _inproj_call(yp, mod_p[l], g1, inw, seq, 256)
        cum2 = cum.reshape(n_p, FOX_HEADS // 2, 2).transpose(1, 0, 2)
        cumt2 = cumt.reshape(bp, FOX_HEADS // 2, 2, seq)
        fo = _fox_call(q, k, v, cum2, cumt2, bp, seq, 512)
        lo, conv_p, hlast_p = _lru_call(rz, lw, bp, seq, 256)
        go, st_p = _gla_call(rz, ow["gg"], bp, seq, 256)
        y1p, h2, ti, tw = _outproj_call(yp, fo, lo, go, mod_p[l], g2, ow["w_out"], ow["w_r"], ow["b_r"],
                                        256, seq, n_tot, 0)
        st_p = st_p.reshape(bp, GLA_HEADS, HEAD_DIM, GLA_HEADS, HEAD_DIM)
        st_p = jnp.stack([st_p[:, hh, :, hh, :] for hh in range(GLA_HEADS)], axis=1).transpose(0, 1, 3, 2)
        outs_p[0].append(k.reshape(bp, seq, FOX_HEADS, HEAD_DIM))
        outs_p[1].append(v.reshape(bp, seq, FOX_HEADS, HEAD_DIM))
        outs_p[2].append(logf.reshape(bp, seq, FOX_HEADS))
        outs_p[3].append(conv_p)
        outs_p[4].append(hlast_p.reshape(bp, LRU_W))
        outs_p[5].append(st_p)

        qs, ks, vs, logfs, _, _, rzs = _inproj_call(ys, mod_s[l], g1, inw, bs, bs)
        fos = _fox_dec_call(l, page_table, qs, ks, vs, logfs, ck, cv, cache_fox_logf)
        los, conv_s, h_s, gos, s_s = _rec_step_call(
            rzs, state_conv[l].transpose(1, 0, 2), state_lru[l],
            state_gla[l].reshape(bs, 2 * HEAD_DIM, 2 * HEAD_DIM), lw, ow["gg"], 32)
        y1s, h2, ti, tw = _outproj_call(ys, fos, los, gos, mod_s[l], g2, ow["w_out"], ow["w_r"], ow["b_r"],
                                        bs, bs, n_tot, n_p, prev=(h2, ti, tw))
        outs_s[0].append(ks.reshape(bs, 1, FOX_HEADS, HEAD_DIM))
        outs_s[1].append(vs.reshape(bs, 1, FOX_HEADS, HEAD_DIM))
        outs_s[2].append(logfs.reshape(bs, 1, FOX_HEADS))
        outs_s[3].append(conv_s.transpose(1, 0, 2))
        outs_s[4].append(h_s)
        outs_s[5].append(s_s.reshape(bs, GLA_HEADS, HEAD_DIM, HEAD_DIM))

        row_tok, pos, blk_e, n_used = _routing_plan(ti[:, 0:TOP_K], n_blocks)
        xs = jnp.take(h2, row_tok, axis=0)
        ye = _ffn_call(l, blk_e, n_used, xs, w_gu, b_gu4, w_down, b_dn4)
        yk = jnp.take(ye, pos.reshape(n_tot, TOP_K).T.reshape(-1), axis=0).reshape(TOP_K, n_tot, d)
        final = l == n_layers - 1
        yp = _combine_call(y1p, yk, tw, mod_p[l], gf, 256, seq, 0, final)
        ys = _combine_call(y1s, yk, tw, mod_s[l], gf, bs, bs, n_p, final)

    res_p = tuple(jnp.stack(o) for o in outs_p)
    res_s = tuple(jnp.stack(o) for o in outs_s)
    return (yp.reshape(bp, seq, d), ys.reshape(bs, 1, d)) + res_p + res_s
```

```python
import functools

import jax
import jax.numpy as jnp
from jax import lax
from jax.experimental import pallas as pl
from jax.experimental.pallas import tpu as pltpu
from jax.experimental.pallas import tpu_sc as plsc

F32 = jnp.float32
BF16 = jnp.bfloat16
I32 = jnp.int32

D_MODEL = 1024
HEAD_DIM = 64
FOX_W = 512
FOX_HEADS = 8
LRU_W = 256
GLA_W = 256
GLA_HEADS = 4
GLA_RANK = 16
GLA_CHUNK = 64
GLA_TAU = 16.0
LRU_C = 8.0
CONV_W = 4
N_EXPERTS = 32
TOP_K = 4
D_FF = 1024
SWIGLU_LIMIT = 7.0
SWIGLU_ALPHA = 1.702
NORM_EPS = 1e-6
PAGE_SIZE = 128
QK_SCALE = HEAD_DIM ** -0.5

_O_FQ, _O_FK, _O_FV, _O_FF = 0, 512, 1024, 1536
_O_LX, _O_LG, _O_GQ, _O_GK, _O_GV, _O_GA, _O_GOG = 1544, 1800, 2056, 2312, 2568, 2824, 2840
REC_W = 1792
GLA_T_ROWS = 5 * GLA_W
SMALL_W = 128
SMALL_T = 32

VMEM_LIMIT = 56 * 1024 * 1024
MOE_BLOCK = 256
SC_WINDOW = 128
SC_COLS = 256
SC_WORKERS = 32

_NT = (((1,), (1,)), ((), ()))
_TN = (((0,), (0,)), ((), ()))


def _cparams(sem, vmem=VMEM_LIMIT):
    return pltpu.CompilerParams(dimension_semantics=sem, vmem_limit_bytes=vmem)


def _log_sigmoid(x):
    return jnp.minimum(x, 0.0) - jnp.log1p(jnp.exp(-jnp.abs(x)))


def _softplus(x):
    return jnp.maximum(x, 0.0) + jnp.log1p(jnp.exp(-jnp.abs(x)))


def _cumsum(x, axis):
    n = x.shape[axis]
    idx = lax.broadcasted_iota(I32, x.shape, axis)
    s = 1
    while s < n:
        x = x + jnp.where(idx >= s, pltpu.roll(x, s, axis), 0.0)
        s *= 2
    return x


def _mod_spec(mod, tm, tiles_per_seq, last_tile=None):
    d = mod.shape[-1]
    clamp = (lambda i: i) if last_tile is None else (lambda i: jnp.minimum(i, last_tile))
    if mod.shape[2] != 1:
        return pl.BlockSpec((6, None, tm, d), lambda i: (0, 0, clamp(i), 0))
    return pl.BlockSpec((6, None, 1, d), lambda i: (0, clamp(i) // tiles_per_seq, 0, 0))


def _ada_kernel(c_ref, w_ref, b_ref, o_ref):
    c = c_ref[...]
    a = (c * jax.nn.sigmoid(c)).astype(BF16)
    o_ref[...] = jnp.dot(a, w_ref[...].astype(BF16), preferred_element_type=F32) + b_ref[...]


def _ada_call(c_all, w_ada, b_ada):
    n_layers, d, w = w_ada.shape
    r = c_all.shape[0]
    tn = 1536
    return pl.pallas_call(
        _ada_kernel,
        grid=(n_layers, w // tn),
        in_specs=[
            pl.BlockSpec((r, d), lambda l, j: (0, 0)),
            pl.BlockSpec((None, d, tn), lambda l, j: (l, 0, j)),
            pl.BlockSpec((None, 1, tn), lambda l, j: (l, 0, j)),
        ],
        out_specs=pl.BlockSpec((None, r, tn), lambda l, j: (l, 0, j)),
        out_shape=jax.ShapeDtypeStruct((n_layers, r, w), F32),
        compiler_params=_cparams(("arbitrary", "arbitrary")),
        name="ada_mod",
    )(c_all, w_ada, b_ada.reshape(n_layers, 1, w))


def _inproj_kernel(*refs, tiles_per_seq, sample, n_alias):
    (y_ref, mod_ref, g_ref, wq_ref, wkvt_ref, wrec_ref, wsm_ref, wsmt_ref,
     bsm_ref, bfc_ref, w2_ref, b2_ref) = refs[:12]
    pos = 12
    if sample:
        wgt_ref, w2t_ref, b2c_ref = refs[12:15]
        pos = 15
    pos += n_alias
    q_ref, kt_ref, vt_ref, lft_ref, cumt_ref, cum_ref, rz_ref = refs[pos:pos + 7]
    pos += 7
    if sample:
        k_ref, v_ref, logf_ref, gt_ref = refs[pos:pos + 4]
        pos += 4
    carry_c, carry_r = refs[pos:pos + 2]
    i = pl.program_id(0)

    @pl.when(i % tiles_per_seq == 0)
    def _():
        carry_c[...] = jnp.zeros_like(carry_c)
        carry_r[...] = jnp.zeros_like(carry_r)

    x = y_ref[...]
    xn = x * lax.rsqrt(jnp.mean(x * x, axis=-1, keepdims=True) + NORM_EPS) * g_ref[...]
    h = (xn * (1.0 + mod_ref[1]) + mod_ref[0]).astype(BF16)

    q_ref[...] = (jnp.dot(h, wq_ref[...], preferred_element_type=F32) * QK_SCALE).astype(BF16)
    kt = lax.dot_general(wkvt_ref[0:FOX_W, :], h, _NT, preferred_element_type=F32)
    vt = lax.dot_general(wkvt_ref[FOX_W:2 * FOX_W, :], h, _NT, preferred_element_type=F32)
    kt_ref[...] = kt
    vt_ref[...] = vt
    rz_ref[:, 0:REC_W - GLA_W] = jnp.dot(h, wrec_ref[...], preferred_element_type=F32)

    sm = jnp.dot(h, wsm_ref[...], preferred_element_type=F32)
    lane = lax.broadcasted_iota(I32, sm.shape, 1)
    logf = jnp.where(lane < FOX_HEADS, _log_sigmoid(sm + bsm_ref[...]), 0.0)
    cum = _cumsum(logf, 0) + carry_c[...]
    carry_c[...] = cum[cum.shape[0] - 1:, :]
    cum_ref[...] = cum[:, 0:FOX_HEADS]
    glin = jnp.dot(sm.astype(BF16), w2_ref[...], preferred_element_type=F32) + b2_ref[...]
    rz_ref[:, REC_W - GLA_W:REC_W] = _log_sigmoid(glin) * (1.0 / GLA_TAU)

    smt = lax.dot_general(wsmt_ref[...], h, _NT, preferred_element_type=F32)
    lft = _log_sigmoid(smt[0:FOX_HEADS, :] + bfc_ref[...])
    lft_ref[...] = lft
    cumt = _cumsum(lft, 1) + carry_r[...]
    carry_r[...] = cumt[:, cumt.shape[1] - 1:]
    cumt_ref[...] = cumt

    if sample:
        k_ref[...] = kt.T
        v_ref[...] = vt.T
        logf_ref[...] = logf[:, 0:FOX_HEADS]
        gt_ref[0:4 * GLA_W, :] = lax.dot_general(wgt_ref[...], h, _NT, preferred_element_type=F32)
        glt = jnp.dot(w2t_ref[...], smt.astype(BF16), preferred_element_type=F32) + b2c_ref[...]
        gt_ref[4 * GLA_W:GLA_T_ROWS, :] = _log_sigmoid(glt) * (1.0 / GLA_TAU)


def _inproj_call(y, mod, g, wts, seq_len, tm, layer, n_layers, kv_bufs=None, sample=False):
    n, d = y.shape
    tiles_per_seq = seq_len // tm
    n_seq = n // seq_len
    const = lambda i: (0, 0)
    row = lambda i: (i, 0)
    seq_t = lambda i: (layer, i // tiles_per_seq, 0, i % tiles_per_seq)
    in_specs = [
        pl.BlockSpec((tm, d), row),
        _mod_spec(mod, tm, tiles_per_seq),
        pl.BlockSpec((1, d), const),
        pl.BlockSpec((d, FOX_W), const),
        pl.BlockSpec((2 * FOX_W, d), const),
        pl.BlockSpec((d, REC_W - GLA_W), const),
        pl.BlockSpec((d, SMALL_W), const),
        pl.BlockSpec((SMALL_T, d), const),
        pl.BlockSpec((1, SMALL_W), const),
        pl.BlockSpec((FOX_HEADS, 1), const),
        pl.BlockSpec((SMALL_W, GLA_W), const),
        pl.BlockSpec((1, GLA_W), const),
    ]
    args = [y, mod, g, wts["w_q"], wts["w_kvt"], wts["w_rec"], wts["w_sm"], wts["w_smt"],
            wts["b_sm"], wts["b_fc"], wts["w2"], wts["b2"]]
    if sample:
        in_specs += [pl.BlockSpec((4 * GLA_W, d), const), pl.BlockSpec((GLA_W, SMALL_T), const),
                     pl.BlockSpec((GLA_W, 1), const)]
        args += [wts["w_gt"], wts["w2t"], wts["b2c"]]
    aliases = {}
    n_alias = 0
    if kv_bufs is not None:
        n_alias = 3
        first = len(args)
        in_specs += [pl.BlockSpec(memory_space=pl.ANY)] * 3
        args += list(kv_bufs)
        aliases = {first: 1, first + 1: 2, first + 2: 3}
    out_specs = [
        pl.BlockSpec((tm, FOX_W), row),
        pl.BlockSpec((None, None, FOX_W, tm), seq_t),
        pl.BlockSpec((None, None, FOX_W, tm), seq_t),
        pl.BlockSpec((None, None, FOX_HEADS, tm), seq_t),
        pl.BlockSpec((None, FOX_HEADS, tm), lambda i: (i // tiles_per_seq, 0, i % tiles_per_seq)),
        pl.BlockSpec((tm, FOX_HEADS), row),
        pl.BlockSpec((tm, REC_W), row),
    ]
    out_shape = [
        jax.ShapeDtypeStruct((n, FOX_W), BF16),
        jax.ShapeDtypeStruct((n_layers, n_seq, FOX_W, seq_len), F32),
        jax.ShapeDtypeStruct((n_layers, n_seq, FOX_W, seq_len), F32),
        jax.ShapeDtypeStruct((n_layers, n_seq, FOX_HEADS, seq_len), F32),
        jax.ShapeDtypeStruct((n_seq, FOX_HEADS, seq_len), F32),
        jax.ShapeDtypeStruct((n, FOX_HEADS), F32),
        jax.ShapeDtypeStruct((n, REC_W), F32),
    ]
    if sample:
        out_specs += [pl.BlockSpec((tm, FOX_W), row), pl.BlockSpec((tm, FOX_W), row),
                      pl.BlockSpec((tm, FOX_HEADS), row), pl.BlockSpec((GLA_T_ROWS, tm), lambda i: (0, i))]
        out_shape += [jax.ShapeDtypeStruct((n, FOX_W), F32), jax.ShapeDtypeStruct((n, FOX_W), F32),
                      jax.ShapeDtypeStruct((n, FOX_HEADS), F32), jax.ShapeDtypeStruct((GLA_T_ROWS, n), F32)]
    return pl.pallas_call(
        functools.partial(_inproj_kernel, tiles_per_seq=tiles_per_seq, sample=sample, n_alias=n_alias),
        grid=(n // tm,),
        in_specs=in_specs,
        out_specs=tuple(out_specs),
        out_shape=tuple(out_shape),
        input_output_aliases=aliases,
        scratch_shapes=[pltpu.VMEM((1, SMALL_W), F32), pltpu.VMEM((FOX_HEADS, 1), F32)],
        compiler_params=_cparams(("arbitrary",)),
        name="in_proj",
    )(*args)


def _fox_kernel(q_ref, kt_ref, vt_ref, cq_ref, ck_ref, o_ref, m_sc, l_sc, acc_sc, *, tq, tk):
    qi = pl.program_id(2)
    ki = pl.program_id(3)

    @pl.when(ki == 0)
    def _():
        m_sc[...] = jnp.full_like(m_sc, -jnp.inf)
        l_sc[...] = jnp.zeros_like(l_sc)
        acc_sc[...] = jnp.zeros_like(acc_sc)

    def step(masked):
        q = q_ref[...]
        kt = kt_ref[...].astype(BF16)
        vt = vt_ref[...].astype(BF16)
        lane = lax.broadcasted_iota(I32, q.shape, 1)
        if masked:
            causal = lax.broadcasted_iota(I32, (tq, tk), 1) <= lax.broadcasted_iota(I32, (tq, tk), 0)
        for j in range(2):
            qm = jnp.where(lane // HEAD_DIM == j, q, jnp.zeros_like(q))
            s = jnp.dot(qm, kt, preferred_element_type=F32)
            s = s + cq_ref[:, j:j + 1] - ck_ref[j:j + 1, :]
            if masked:
                s = jnp.where(causal, s, -jnp.inf)
            m_prev = m_sc[j]
            m_new = jnp.maximum(m_prev, jnp.max(s, axis=1, keepdims=True))
            p = jnp.exp(s - m_new)
            alpha = jnp.exp(m_prev - m_new)
            l_sc[j] = alpha * l_sc[j] + jnp.sum(p, axis=1, keepdims=True)
            acc_sc[j] = alpha * acc_sc[j] + lax.dot_general(p.astype(BF16), vt, _NT, preferred_element_type=F32)
            m_sc[j] = m_new

    @pl.when(ki < qi)
    def _():
        step(False)

    @pl.when(ki == qi)
    def _():
        step(True)
        lane = lax.broadcasted_iota(I32, (tq, 2 * HEAD_DIM), 1)
        o0 = acc_sc[0] / l_sc[0]
        o1 = acc_sc[1] / l_sc[1]
        o_ref[...] = jnp.where(lane < HEAD_DIM, o0, o1).astype(BF16)


def _fox_call(layer, q, kt, vt, cum2, cumt2, n_seq, seq_len, tq):
    n = q.shape[0]
    nt = seq_len // tq
    tk = tq
    hp = FOX_HEADS // 2
    w = 2 * HEAD_DIM
    kv_spec = pl.BlockSpec((None, None, w, tk), lambda b, h, i, j: (layer, b, h, jnp.minimum(i, j)))
    return pl.pallas_call(
        functools.partial(_fox_kernel, tq=tq, tk=tk),
        grid=(n_seq, hp, nt, nt),
        in_specs=[
            pl.BlockSpec((tq, w), lambda b, h, i, j: (b * nt + i, h)),
            kv_spec,
            kv_spec,
            pl.BlockSpec((None, tq, 2), lambda b, h, i, j: (h, b * nt + i, 0)),
            pl.BlockSpec((None, None, 2, tk), lambda b, h, i, j: (b, h, 0, jnp.minimum(i, j))),
        ],
        out_specs=pl.BlockSpec((tq, w), lambda b, h, i, j: (b * nt + i, h)),
        out_shape=jax.ShapeDtypeStruct((n, FOX_W), BF16),
        scratch_shapes=[pltpu.VMEM((2, tq, 1), F32), pltpu.VMEM((2, tq, 1), F32), pltpu.VMEM((2, tq, w), F32)],
        compiler_params=_cparams(("arbitrary", "arbitrary", "arbitrary", "arbitrary")),
        name="fox_prompt",
    )(q, kt, vt, cum2, cumt2)


def _lru_gates(xc, wa_ref, ba_ref, wx_ref, bx_ref, lam_ref):
    xb = xc.astype(BF16)
    r = jax.nn.sigmoid(jnp.dot(xb, wa_ref[...], preferred_element_type=F32) + ba_ref[...])
    gi = jax.nn.sigmoid(jnp.dot(xb, wx_ref[...], preferred_element_type=F32) + bx_ref[...])
    log_a = -LRU_C * r * _softplus(-lam_ref[...])
    a = jnp.exp(log_a)
    mult = jnp.sqrt(-jnp.tanh(log_a) * (a * a + 1.0))
    return a, mult, gi


def _lru_kernel(lx_ref, lg_ref, cw_ref, cb_ref, wa_ref, ba_ref, wx_ref, bx_ref, lam_ref,
                lo_ref, conv_ref, hlast_ref, xbuf, hcar, *, tt):
    ti = pl.program_id(1)
    nt = pl.num_programs(1)

    @pl.when(ti == 0)
    def _():
        xbuf[0:8, :] = jnp.zeros((8, LRU_W), F32)
        hcar[...] = jnp.zeros_like(hcar)

    x = lx_ref[...]
    xbuf[8:8 + tt, :] = x
    xc = cb_ref[...] + cw_ref[3:4, :] * x
    for j in range(CONV_W - 1):
        xc = xc + cw_ref[j:j + 1, :] * xbuf[5 + j:5 + j + tt, :]
    xbuf[0:8, :] = x[tt - 8:tt, :]

    a, mult, gi = _lru_gates(xc, wa_ref, ba_ref, wx_ref, bx_ref, lam_ref)
    row = lax.broadcasted_iota(I32, (tt, LRU_W), 0)
    mult = jnp.where((row == 0) & (ti == 0), 1.0, mult)
    b = mult * gi * xc
    s = 1
    while s < tt:
        keep = row >= s
        a_sh = jnp.where(keep, pltpu.roll(a, s, 0), 1.0)
        b_sh = jnp.where(keep, pltpu.roll(b, s, 0), 0.0)
        b = a * b_sh + b
        a = a * a_sh
        s *= 2
    h = a * hcar[...] + b
    hcar[...] = h[tt - 1:tt, :]
    lo_ref[...] = (h * jax.nn.gelu(lg_ref[...])).astype(BF16)

    @pl.when(ti == nt - 1)
    def _():
        conv_ref[...] = x[tt - (CONV_W - 1):tt, :]
        hlast_ref[...] = h[tt - 1:tt, :]


def _lru_weight_specs(const):
    return [
        pl.BlockSpec((CONV_W, LRU_W), const), pl.BlockSpec((1, LRU_W), const),
        pl.BlockSpec((LRU_W, LRU_W), const), pl.BlockSpec((1, LRU_W), const),
        pl.BlockSpec((LRU_W, LRU_W), const), pl.BlockSpec((1, LRU_W), const),
        pl.BlockSpec((1, LRU_W), const),
    ]


def _lru_weight_args(lw):
    return [lw["conv_w"], lw["conv_b"], lw["wa"], lw["ba"], lw["wx"], lw["bx"], lw["lam"]]


def _lru_call(rz, lw, n_seq, seq_len, tt):
    n = rz.shape[0]
    nt = seq_len // tt
    return pl.pallas_call(
        functools.partial(_lru_kernel, tt=tt),
        grid=(n_seq, nt),
        in_specs=[
            pl.BlockSpec((tt, LRU_W), lambda b, t: (b * nt + t, 0)),
            pl.BlockSpec((tt, LRU_W), lambda b, t: (b * nt + t, 1)),
        ] + _lru_weight_specs(lambda b, t: (0, 0)),
        out_specs=(
            pl.BlockSpec((tt, LRU_W), lambda b, t: (b * nt + t, 0)),
            pl.BlockSpec((None, CONV_W - 1, LRU_W), lambda b, t: (b, 0, 0)),
            pl.BlockSpec((None, 1, LRU_W), lambda b, t: (b, 0, 0)),
        ),
        out_shape=(
            jax.ShapeDtypeStruct((n, LRU_W), BF16),
            jax.ShapeDtypeStruct((n_seq, CONV_W - 1, LRU_W), F32),
            jax.ShapeDtypeStruct((n_seq, 1, LRU_W), F32),
        ),
        scratch_shapes=[pltpu.VMEM((tt + 8, LRU_W), F32), pltpu.VMEM((1, LRU_W), F32)],
        compiler_params=_cparams(("arbitrary", "arbitrary")),
        name="lru_prompt",
    )(rz, rz, *_lru_weight_args(lw))


def _head_rms_gate(o, gg_ref, gog):
    lane = lax.broadcasted_iota(I32, o.shape, 1)
    o2 = o * o
    rs = jnp.zeros_like(o)
    for hh in range(GLA_HEADS):
        mh = lane // HEAD_DIM == hh
        ms = jnp.sum(jnp.where(mh, o2, 0.0), axis=1, keepdims=True) * (1.0 / HEAD_DIM)
        rs = jnp.where(mh, lax.rsqrt(ms + NORM_EPS), rs)
    return o * rs * gg_ref[...] * (gog * jax.nn.sigmoid(gog))


def _gla_kernel(gq_ref, gk_ref, gv_ref, gog_ref, gl_ref, gg_ref, go_ref, st_ref, s_sc, *, tt):
    ti = pl.program_id(1)
    nt = pl.num_programs(1)
    c = GLA_CHUNK

    @pl.when(ti == 0)
    def _():
        s_sc[...] = jnp.zeros_like(s_sc)

    lane = lax.broadcasted_iota(I32, (c, GLA_W), 1)
    r2 = lax.broadcasted_iota(I32, (GLA_W, GLA_W), 0)
    c2 = lax.broadcasted_iota(I32, (GLA_W, GLA_W), 1)
    same_head = (r2 // HEAD_DIM) == (c2 // HEAD_DIM)
    tril = lax.broadcasted_iota(I32, (c, c), 1) <= lax.broadcasted_iota(I32, (c, c), 0)

    for ci in range(tt // c):
        sl = slice(ci * c, (ci + 1) * c)
        q = gq_ref[sl, :] * QK_SCALE
        k = gk_ref[sl, :]
        v = gv_ref[sl, :].astype(BF16)
        bc = _cumsum(gl_ref[sl, :], 0)
        b_last = bc[c - 1:c, :]
        qd = (q * jnp.exp(bc)).astype(BF16)
        kinv = (k * jnp.exp(-bc)).astype(BF16)
        kdec = (k * jnp.exp(b_last - bc)).astype(BF16)
        s_prev = s_sc[...]
        o = lax.dot_general(qd, s_prev.astype(BF16), _NT, preferred_element_type=F32)
        for hh in range(GLA_HEADS):
            mh = lane // HEAD_DIM == hh
            att = lax.dot_general(jnp.where(mh, qd, jnp.zeros_like(qd)), kinv, _NT, preferred_element_type=F32)
            att = jnp.where(tril, att, 0.0).astype(BF16)
            o = o + jnp.dot(att, jnp.where(mh, v, jnp.zeros_like(v)), preferred_element_type=F32)
        ut = lax.dot_general(v, kdec, _TN, preferred_element_type=F32)
        s_sc[...] = s_prev * jnp.exp(b_last) + jnp.where(same_head, ut, 0.0)
        go_ref[sl, :] = _head_rms_gate(o, gg_ref, gog_ref[sl, :]).astype(BF16)

    @pl.when(ti == nt - 1)
    def _():
        st_ref[...] = s_sc[...]


def _gla_call(rz, gg, n_seq, seq_len, tt):
    n = rz.shape[0]
    nt = seq_len // tt

    def col(j):
        return pl.BlockSpec((tt, GLA_W), lambda b, t: (b * nt + t, j))

    return pl.pallas_call(
        functools.partial(_gla_kernel, tt=tt),
        grid=(n_seq, nt),
        in_specs=[col(2), col(3), col(4), col(5), col(6), pl.BlockSpec((1, GLA_W), lambda b, t: (0, 0))],
        out_specs=(
            pl.BlockSpec((tt, GLA_W), lambda b, t: (b * nt + t, 0)),
            pl.BlockSpec((None, GLA_W, GLA_W), lambda b, t: (b, 0, 0)),
        ),
        out_shape=(
            jax.ShapeDtypeStruct((n, GLA_W), BF16),
            jax.ShapeDtypeStruct((n_seq, GLA_W, GLA_W), F32),
        ),
        scratch_shapes=[pltpu.VMEM((GLA_W, GLA_W), F32)],
        compiler_params=_cparams(("arbitrary", "arbitrary")),
        name="gla_prompt",
    )(rz, rz, rz, rz, rz, gg)


def _fox_dec_kernel(pt_ref, q_ref, kn_ref, vn_ref, dn_ref, *refs, n_pages):
    del pt_ref
    k_refs = refs[0:n_pages]
    v_refs = refs[n_pages:2 * n_pages]
    f_refs = refs[2 * n_pages:3 * n_pages]
    o_ref = refs[3 * n_pages]
    w = FOX_W
    hrow = lax.broadcasted_iota(I32, (FOX_HEADS, w), 0)
    hlane = lax.broadcasted_iota(I32, (FOX_HEADS, w), 1) // HEAD_DIM
    diag = hrow == hlane
    q = q_ref[...].astype(F32)
    qbd = jnp.where(diag, jnp.broadcast_to(q, (FOX_HEADS, w)), 0.0).astype(BF16)
    s = jnp.concatenate(
        [jnp.dot(qbd, k_refs[p][...].astype(BF16), preferred_element_type=F32) for p in range(n_pages)], axis=1)
    lf = jnp.concatenate([f_refs[p][...] for p in range(n_pages)], axis=1)
    cs = _cumsum(lf, 1)
    suffix = cs[:, cs.shape[1] - 1:] - cs
    s = s + dn_ref[...] + suffix
    s_new = jnp.sum(qbd.astype(F32) * kn_ref[...], axis=1, keepdims=True)
    m = jnp.maximum(jnp.max(s, axis=1, keepdims=True), s_new)
    p_past = jnp.exp(s - m)
    p_new = jnp.exp(s_new - m)
    denom = jnp.sum(p_past, axis=1, keepdims=True) + p_new
    acc = p_new * vn_ref[...]
    pb = p_past.astype(BF16)
    for p in range(n_pages):
        acc = acc + lax.dot_general(pb[:, p * PAGE_SIZE:(p + 1) * PAGE_SIZE], v_refs[p][...].astype(BF16), _NT,
                                    preferred_element_type=F32)
    out = jnp.where(diag, acc / denom, 0.0)
    o_ref[...] = jnp.sum(out, axis=0, keepdims=True).astype(BF16)


def _fox_dec_call(layer, page_table, q, k_new, v_new, logf_new, cache_kt, cache_vt, cache_ft):
    bd, n_pages = page_table.shape
    w = FOX_W

    def page_spec(rows, j):
        return pl.BlockSpec((None, None, rows, PAGE_SIZE), lambda b, pt, j=j: (layer, pt[b, j], 0, 0))

    row = lambda b, pt: (b, 0, 0)
    in_specs = [
        pl.BlockSpec((None, 1, w), row),
        pl.BlockSpec((None, 1, w), row),
        pl.BlockSpec((None, 1, w), row),
        pl.BlockSpec((None, FOX_HEADS, 1), row),
    ]
    in_specs += [page_spec(w, j) for j in range(n_pages)]
    in_specs += [page_spec(w, j) for j in range(n_pages)]
    in_specs += [page_spec(FOX_HEADS, j) for j in range(n_pages)]
    grid_spec = pltpu.PrefetchScalarGridSpec(
        num_scalar_prefetch=1,
        grid=(bd,),
        in_specs=in_specs,
        out_specs=pl.BlockSpec((None, 1, w), row),
    )
    out = pl.pallas_call(
        functools.partial(_fox_dec_kernel, n_pages=n_pages),
        grid_spec=grid_spec,
        out_shape=jax.ShapeDtypeStruct((bd, 1, w), BF16),
        compiler_params=_cparams(("arbitrary",)),
        name="fox_sample",
    )(page_table, q.reshape(bd, 1, w), k_new.reshape(bd, 1, w), v_new.reshape(bd, 1, w),
      logf_new.reshape(bd, FOX_HEADS, 1),
      *([cache_kt] * n_pages), *([cache_vt] * n_pages), *([cache_ft] * n_pages))
    return out.reshape(bd, w)


def _lru_step_kernel(lx_ref, lg_ref, conv_ref, h0_ref, cw_ref, cb_ref, wa_ref, ba_ref, wx_ref, bx_ref, lam_ref,
                     lo_ref, convn_ref, hn_ref):
    x = lx_ref[...]
    xc = cb_ref[...] + cw_ref[3:4, :] * x
    for j in range(CONV_W - 1):
        xc = xc + cw_ref[j:j + 1, :] * conv_ref[j]
    convn_ref[0] = conv_ref[1]
    convn_ref[1] = conv_ref[2]
    convn_ref[2] = x
    a, mult, gi = _lru_gates(xc, wa_ref, ba_ref, wx_ref, bx_ref, lam_ref)
    h = a * h0_ref[...] + mult * gi * xc
    hn_ref[...] = h
    lo_ref[...] = (h * jax.nn.gelu(lg_ref[...])).astype(BF16)


def _lru_step_call(layer, rz, conv_t, h0, lw):
    bd = rz.shape[0]
    return pl.pallas_call(
        _lru_step_kernel,
        grid=(1,),
        in_specs=[
            pl.BlockSpec((bd, LRU_W), lambda i: (0, 0)),
            pl.BlockSpec((bd, LRU_W), lambda i: (0, 1)),
            pl.BlockSpec((None, CONV_W - 1, bd, LRU_W), lambda i: (layer, 0, 0, 0)),
            pl.BlockSpec((None, bd, LRU_W), lambda i: (layer, 0, 0)),
        ] + _lru_weight_specs(lambda i: (0, 0)),
        out_specs=(
            pl.BlockSpec((bd, LRU_W), lambda i: (0, 0)),
            pl.BlockSpec((CONV_W - 1, bd, LRU_W), lambda i: (0, 0, 0)),
            pl.BlockSpec((bd, LRU_W), lambda i: (0, 0)),
        ),
        out_shape=(
            jax.ShapeDtypeStruct((bd, LRU_W), BF16),
            jax.ShapeDtypeStruct((CONV_W - 1, bd, LRU_W), F32),
            jax.ShapeDtypeStruct((bd, LRU_W), F32),
        ),
        compiler_params=_cparams(("arbitrary",)),
        name="lru_step",
    )(rz, rz, conv_t, h0, *_lru_weight_args(lw))


def _gla_step_kernel(q_ref, k_ref, v_ref, gog_ref, gl_ref, gg_ref, s_ref, go_ref, sn_ref):
    eg = jnp.exp(gl_ref[...])
    kt = k_ref[...]
    qt = q_ref[...] * QK_SCALE
    vt = v_ref[...]
    o = jnp.zeros_like(vt)
    for kk in range(HEAD_DIM):
        s_new = eg[kk:kk + 1, :] * s_ref[kk] + kt[kk:kk + 1, :] * vt
        sn_ref[kk] = s_new
        o = o + qt[kk:kk + 1, :] * s_new
    ms = jnp.mean(o * o, axis=0, keepdims=True)
    gog = gog_ref[...]
    go_ref[...] = o * lax.rsqrt(ms + NORM_EPS) * gg_ref[...] * (gog * jax.nn.sigmoid(gog))


def _gla_step_call(layer, gt, s_view, ggc):
    bd = gt.shape[1]
    hd = HEAD_DIM

    def part(j):
        return pl.BlockSpec((hd, bd), lambda h, j=j: (j * GLA_HEADS + h, 0))

    return pl.pallas_call(
        _gla_step_kernel,
        grid=(GLA_HEADS,),
        in_specs=[part(0), part(1), part(2), part(3), part(4),
                  pl.BlockSpec((hd, 1), lambda h: (h, 0)),
                  pl.BlockSpec((None, None, hd, hd, bd), lambda h: (layer, h, 0, 0, 0))],
        out_specs=(
            pl.BlockSpec((hd, bd), lambda h: (h, 0)),
            pl.BlockSpec((None, hd, hd, bd), lambda h: (h, 0, 0, 0)),
        ),
        out_shape=(
            jax.ShapeDtypeStruct((GLA_W, bd), F32),
            jax.ShapeDtypeStruct((GLA_HEADS, hd, hd, bd), F32),
        ),
        compiler_params=_cparams(("arbitrary",)),
        name="gla_step",
    )(gt, gt, gt, gt, gt, ggc, s_view)


def _outproj_kernel(*refs, n_alias, n_tiles):
    h2_ref, ti_ref, tw_ref = refs[10 + n_alias:13 + n_alias]
    i = pl.program_id(0)

    @pl.when(i < n_tiles)
    def _():
        _outproj_tile(*refs[0:9], *refs[9 + n_alias:13 + n_alias])

    @pl.when(i >= n_tiles)
    def _():
        h2_ref[...] = jnp.zeros_like(h2_ref)
        ti_ref[...] = jnp.zeros_like(ti_ref)
        tw_ref[...] = jnp.zeros_like(tw_ref)


def _outproj_tile(y_ref, fo_ref, lo_ref, go_ref, mod_ref, g2_ref, wo_ref, wr_ref, br_ref,
                  y1_ref, h2_ref, ti_ref, tw_ref):
    m = jnp.dot(fo_ref[...], wo_ref[0:FOX_W, :], preferred_element_type=F32)
    m = m + jnp.dot(lo_ref[...], wo_ref[FOX_W:FOX_W + LRU_W, :], preferred_element_type=F32)
    m = m + jnp.dot(go_ref[...].astype(BF16), wo_ref[FOX_W + LRU_W:, :], preferred_element_type=F32)
    y1 = y_ref[...] + mod_ref[2] * m
    y1_ref[...] = y1
    xn = y1 * lax.rsqrt(jnp.mean(y1 * y1, axis=-1, keepdims=True) + NORM_EPS) * g2_ref[...]
    h2 = xn * (1.0 + mod_ref[4]) + mod_ref[3]
    h2_ref[...] = h2
    logits = jnp.dot(h2.astype(BF16), wr_ref[...], preferred_element_type=F32) + br_ref[...]
    lane = lax.broadcasted_iota(I32, logits.shape, 1)
    logits = jnp.where(lane < N_EXPERTS, logits, -jnp.inf)
    idx_out = jnp.zeros(logits.shape, I32)
    val_out = jnp.zeros(logits.shape, F32)
    vals = []
    for kk in range(TOP_K):
        mx = jnp.max(logits, axis=1, keepdims=True)
        sel = jnp.min(jnp.where(logits == mx, lane, SMALL_W), axis=1, keepdims=True)
        idx_out = jnp.where(lane == kk, sel, idx_out)
        vals.append(mx)
        logits = jnp.where(lane == sel, -jnp.inf, logits)
    es = [jnp.exp(vv - vals[0]) for vv in vals]
    tot = es[0] + es[1] + es[2] + es[3]
    for kk in range(TOP_K):
        val_out = jnp.where(lane == kk, es[kk] / tot, val_out)
    ti_ref[...] = idx_out
    tw_ref[...] = val_out


def _outproj_call(y, fo, lo, go, mod, g2, w_out, w_r, b_r, tm, seq_len, n_buf, row_off, prev=None):
    n, d = y.shape
    tiles_per_seq = max(seq_len // tm, 1)
    n_tiles = n // tm
    n_steps = n_tiles if prev is not None else n_buf // tm
    const = lambda i: (0, 0)
    row = lambda i: (jnp.minimum(i, n_tiles - 1), 0)
    off = row_off // tm
    orow = lambda i: (i + off, 0)
    in_specs = [
        pl.BlockSpec((tm, d), row),
        pl.BlockSpec((tm, FOX_W), row),
        pl.BlockSpec((tm, LRU_W), row),
        pl.BlockSpec((tm, GLA_W), row),
        _mod_spec(mod, tm, tiles_per_seq, n_tiles - 1),
        pl.BlockSpec((1, d), const),
        pl.BlockSpec((d, d), const),
        pl.BlockSpec((d, SMALL_W), const),
        pl.BlockSpec((1, SMALL_W), const),
    ]
    args = [y, fo, lo, go, mod, g2, w_out, w_r, b_r]
    aliases = {}
    if prev is not None:
        in_specs += [pl.BlockSpec(memory_space=pl.ANY)] * 3
        args += list(prev)
        aliases = {9: 1, 10: 2, 11: 3}
    return pl.pallas_call(
        functools.partial(_outproj_kernel, n_alias=0 if prev is None else 3, n_tiles=n_tiles),
        grid=(n_steps,),
        in_specs=in_specs,
        out_specs=(
            pl.BlockSpec((tm, d), row),
            pl.BlockSpec((tm, d), orow),
            pl.BlockSpec((tm, SMALL_W), orow),
            pl.BlockSpec((tm, SMALL_W), orow),
        ),
        out_shape=(
            jax.ShapeDtypeStruct((n, d), F32),
            jax.ShapeDtypeStruct((n_buf, d), F32),
            jax.ShapeDtypeStruct((n_buf, SMALL_W), I32),
            jax.ShapeDtypeStruct((n_buf, SMALL_W), F32),
        ),
        input_output_aliases=aliases,
        compiler_params=_cparams(("arbitrary",)),
        name="out_proj_router",
    )(*args)


def _rank_kernel(ti_ref, dest_ref, cnt_ref, carry, *, tm, n_valid, trash):
    p = pl.program_id(0)
    i = pl.program_id(1)
    nt = pl.num_programs(1)

    @pl.when((p == 0) & (i == 0))
    def _():
        carry[...] = jnp.zeros_like(carry)

    @pl.when((p == 1) & (i == 0))
    def _():
        cnt = carry[...]
        cnt_ref[...] = cnt
        padded = jnp.floor((cnt + (MOE_BLOCK - 1.0)) * (1.0 / MOE_BLOCK)) * MOE_BLOCK
        carry[...] = _cumsum(padded, 1) - padded

    t = ti_ref[...]
    lane = lax.broadcasted_iota(I32, (tm, SMALL_W), 1)
    valid = (lax.broadcasted_iota(I32, (tm, 1), 0) + i * tm) < n_valid
    before = (lax.broadcasted_iota(I32, (tm, tm), 1) < lax.broadcasted_iota(I32, (tm, tm), 0)).astype(BF16)
    base = carry[...]
    out = jnp.zeros((tm, SMALL_W), F32)
    for kk in range(TOP_K):
        oh = jnp.where(valid, (lane == t[:, kk:kk + 1]).astype(F32), 0.0)
        pre = jnp.dot(before, oh.astype(BF16), preferred_element_type=F32)
        slot = jnp.sum(oh * (pre + base), axis=1, keepdims=True)
        out = jnp.where(lane == kk, jnp.where(valid, slot, float(trash)), out)
        base = base + jnp.sum(oh, axis=0, keepdims=True)
    carry[...] = base
    dest_ref[...] = out.T[0:8, :].astype(I32)


def _rank_call(ti, n_valid, trash, tm):
    n_buf = ti.shape[0]
    return pl.pallas_call(
        functools.partial(_rank_kernel, tm=tm, n_valid=n_valid, trash=trash),
        grid=(2, n_buf // tm),
        in_specs=[pl.BlockSpec((tm, SMALL_W), lambda p, i: (i, 0))],
        out_specs=(
            pl.BlockSpec((8, tm), lambda p, i: (0, i * p)),
            pl.BlockSpec((1, SMALL_W), lambda p, i: (0, 0)),
        ),
        out_shape=(
            jax.ShapeDtypeStruct((8, n_buf), I32),
            jax.ShapeDtypeStruct((1, SMALL_W), F32),
        ),
        scratch_shapes=[pltpu.VMEM((1, SMALL_W), F32)],
        compiler_params=_cparams(("arbitrary", "arbitrary")),
        name="moe_rank",
    )(ti)


def _sc_mesh():
    return plsc.VectorSubcoreMesh(core_axis_name="core", subcore_axis_name="subcore")


def _sc_scatter_rows(x, dest_km, n_out):
    kk, n = dest_km.shape
    d = x.shape[1]
    nb = n // SC_WINDOW
    assert (kk * nb) % SC_WORKERS == 0 and d % SC_COLS == 0

    @pl.kernel(out_type=jax.ShapeDtypeStruct((n_out, d), x.dtype), mesh=_sc_mesh())
    def scatter_kernel(x_hbm, i_hbm, o_hbm):
        def body(x_vmem, i_vmem):
            j = pl.program_id(1)
            pltpu.sync_copy(x_vmem, o_hbm.at[i_vmem.at[0], pl.ds(j * SC_COLS, SC_COLS)])

        pltpu.emit_pipeline(
            body,
            grid=(kk * nb, d // SC_COLS),
            in_specs=[pl.BlockSpec((SC_WINDOW, SC_COLS), lambda g, j: (g % nb, j)),
                      pl.BlockSpec((1, SC_WINDOW), lambda g, j: (g // nb, g % nb))],
            out_specs=[],
            core_axis_name=("core", "subcore"),
            dimension_semantics=(pltpu.PARALLEL, pltpu.ARBITRARY),
        )(x_hbm, i_hbm)

    return scatter_kernel(x, dest_km)


def _sc_gather_rows(x, idx):
    n = idx.shape[0]
    d = x.shape[1]
    assert (n // SC_WINDOW) % SC_WORKERS == 0 and d % SC_COLS == 0

    @pl.kernel(out_type=jax.ShapeDtypeStruct((n, d), x.dtype), mesh=_sc_mesh())
    def gather_kernel(x_hbm, i_hbm, o_hbm):
        def body(i_vmem, o_vmem):
            j = pl.program_id(1)
            pltpu.sync_copy(x_hbm.at[i_vmem.at[0], pl.ds(j * SC_COLS, SC_COLS)], o_vmem)

        pltpu.emit_pipeline(
            body,
            grid=(n // SC_WINDOW, d // SC_COLS),
            in_specs=[pl.BlockSpec((1, SC_WINDOW), lambda i, j: (0, i))],
            out_specs=[pl.BlockSpec((SC_WINDOW, SC_COLS), lambda i, j: (i, j))],
            core_axis_name=("core", "subcore"),
            dimension_semantics=(pltpu.PARALLEL, pltpu.ARBITRARY),
        )(i_hbm, o_hbm)

    return gather_kernel(x, idx.reshape(1, n))


def _ffn_kernel(be_ref, nu_ref, x_ref, wgu_ref, bgu_ref, wdn_ref, bdn_ref, y_ref, wgu_b, wdn_b):
    j = pl.program_id(0)
    e = be_ref[j]
    prev = be_ref[jnp.maximum(j - 1, 0)]

    @pl.when((j == 0) | (e != prev))
    def _():
        wgu_b[...] = wgu_ref[...].astype(BF16)
        wdn_b[...] = wdn_ref[...].astype(BF16)

    @pl.when(j < nu_ref[0])
    def _():
        gu = jnp.dot(x_ref[...].astype(BF16), wgu_b[...], preferred_element_type=F32) + bgu_ref[...]
        g = jnp.minimum(gu[:, 0:D_FF], SWIGLU_LIMIT)
        u = jnp.clip(gu[:, D_FF:], -SWIGLU_LIMIT, SWIGLU_LIMIT)
        act = g * jax.nn.sigmoid(SWIGLU_ALPHA * g)
        hmid = ((u + 1.0) * act).astype(BF16)
        y_ref[...] = jnp.dot(hmid, wdn_b[...], preferred_element_type=F32) + bdn_ref[...]

    @pl.when(j >= nu_ref[0])
    def _():
        y_ref[...] = jnp.zeros_like(y_ref)


def _ffn_call(layer, blk_e, n_used, xs, w_gu, b_gu, w_dn, b_dn):
    n_blocks = blk_e.shape[0]
    d = xs.shape[1]
    bm = MOE_BLOCK
    grid_spec = pltpu.PrefetchScalarGridSpec(
        num_scalar_prefetch=2,
        grid=(n_blocks,),
        in_specs=[
            pl.BlockSpec((bm, d), lambda j, be, nu: (j, 0)),
            pl.BlockSpec((None, None, d, 2 * D_FF), lambda j, be, nu: (layer, be[j], 0, 0)),
            pl.BlockSpec((None, None, 1, 2 * D_FF), lambda j, be, nu: (layer, be[j], 0, 0)),
            pl.BlockSpec((None, None, D_FF, d), lambda j, be, nu: (layer, be[j], 0, 0)),
            pl.BlockSpec((None, None, 1, d), lambda j, be, nu: (layer, be[j], 0, 0)),
        ],
        out_specs=pl.BlockSpec((bm, d), lambda j, be, nu: (j, 0)),
        scratch_shapes=[pltpu.VMEM((d, 2 * D_FF), BF16), pltpu.VMEM((D_FF, d), BF16)],
    )
    return pl.pallas_call(
        _ffn_kernel,
        grid_spec=grid_spec,
        out_shape=jax.ShapeDtypeStruct((n_blocks * bm, d), F32),
        compiler_params=_cparams(("arbitrary",)),
        name="expert_ffn",
    )(blk_e, n_used, xs, w_gu, b_gu, w_dn, b_dn)


def _combine_kernel(y1_ref, yk_ref, tw_ref, mod_ref, gf_ref, o_ref, *, final):
    tw = tw_ref[...]
    acc = tw[:, 0:1] * yk_ref[0]
    for kk in range(1, TOP_K):
        acc = acc + tw[:, kk:kk + 1] * yk_ref[kk]
    y2 = y1_ref[...] + mod_ref[5] * acc
    if final:
        y2 = y2 * lax.rsqrt(jnp.mean(y2 * y2, axis=-1, keepdims=True) + NORM_EPS) * gf_ref[...]
    o_ref[...] = y2


def _combine_call(y1, yk, tw, mod, g_final, tm, seq_len, row_off, final):
    n, d = y1.shape
    tiles_per_seq = max(seq_len // tm, 1)
    off = row_off // tm
    return pl.pallas_call(
        functools.partial(_combine_kernel, final=final),
        grid=(n // tm,),
        in_specs=[
            pl.BlockSpec((tm, d), lambda i: (i, 0)),
            pl.BlockSpec((TOP_K, tm, d), lambda i: (0, i + off, 0)),
            pl.BlockSpec((tm, SMALL_W), lambda i: (i + off, 0)),
            _mod_spec(mod, tm, tiles_per_seq),
            pl.BlockSpec((1, d), lambda i: (0, 0)),
        ],
        out_specs=pl.BlockSpec((tm, d), lambda i: (i, 0)),
        out_shape=jax.ShapeDtypeStruct((n, d), F32),
        compiler_params=_cparams(("arbitrary",)),
        name="moe_combine",
    )(y1, yk, tw, mod, g_final)


def _block_diag(w):
    nb, bw, _ = w.shape
    eye = jnp.eye(nb, dtype=w.dtype)
    return (eye[:, None, :, None] * w[:, :, None, :]).reshape(nb * bw, nb * bw)


def _layer_weights(l, w_in, b_forget, conv_w, conv_b, lru_lambda, lru_wa, lru_ba, lru_wx, lru_bx,
                   gla_w2, gla_b2, gla_gnorm, w_out, w_router, b_router):
    wi = w_in[l]
    d = wi.shape[0]
    w_ff = wi[:, _O_FF:_O_LX]
    w_ga = wi[:, _O_GA:_O_GOG]
    w_rec = jnp.concatenate([wi[:, _O_LX:_O_GA], wi[:, _O_GOG:]], axis=1)
    w2 = jnp.zeros((SMALL_W, GLA_W), F32).at[FOX_HEADS:FOX_HEADS + GLA_RANK].set(gla_w2[l])
    inw = {
        "w_q": wi[:, _O_FQ:_O_FK].astype(BF16),
        "w_kvt": wi[:, _O_FK:_O_FF].T.astype(BF16),
        "w_rec": w_rec.astype(BF16),
        "w_sm": jnp.concatenate([w_ff, w_ga, jnp.zeros((d, SMALL_W - FOX_HEADS - GLA_RANK), F32)], axis=1).astype(BF16),
        "w_smt": jnp.concatenate([w_ff, w_ga, jnp.zeros((d, SMALL_T - FOX_HEADS - GLA_RANK), F32)], axis=1).T.astype(BF16),
        "b_sm": jnp.zeros((1, SMALL_W), F32).at[0, 0:FOX_HEADS].set(b_forget[l]),
        "b_fc": b_forget[l].reshape(FOX_HEADS, 1),
        "w2": w2.astype(BF16),
        "b2": gla_b2[l].reshape(1, GLA_W),
        "w_gt": w_rec[:, 2 * GLA_W:].T.astype(BF16),
        "w2t": w2[0:SMALL_T].T.astype(BF16),
        "b2c": gla_b2[l].reshape(GLA_W, 1),
    }
    lw = {
        "conv_w": conv_w[l], "conv_b": conv_b[l].reshape(1, LRU_W),
        "wa": _block_diag(lru_wa[l]).astype(BF16), "ba": lru_ba[l].reshape(1, LRU_W),
        "wx": _block_diag(lru_wx[l]).astype(BF16), "bx": lru_bx[l].reshape(1, LRU_W),
        "lam": lru_lambda[l].reshape(1, LRU_W),
    }
    ow = {
        "gg": gla_gnorm[l].reshape(1, GLA_W),
        "ggc": gla_gnorm[l].reshape(GLA_W, 1),
        "w_out": w_out[l].astype(BF16),
        "w_r": jnp.concatenate([w_router[l], jnp.zeros((d, SMALL_W - N_EXPERTS), F32)], axis=1).astype(BF16),
        "b_r": jnp.zeros((1, SMALL_W), F32).at[0, 0:N_EXPERTS].set(b_router[l]),
    }
    return inw, lw, ow


def kernel(x_prompt, x_sample, cache_fox_k, cache_fox_v, cache_fox_logf, state_conv, state_lru, state_gla, page_table, c_prompt, c_sample, w_ada, b_ada, g_norm1, g_norm2, w_in, b_forget, conv_w, conv_b, lru_lambda, lru_wa, lru_ba, lru_wx, lru_bx, gla_w2, gla_b2, gla_gnorm, w_out, w_router, b_router, w_gu, b_gu, w_down, b_down, g_final):
    n_layers = w_ada.shape[0]
    bp, seq, d = x_prompt.shape
    bs = x_sample.shape[0]
    n_p = bp * seq
    n_tot = n_p + bs
    n_pool = cache_fox_k.shape[1]
    bm = MOE_BLOCK

    mod = _ada_call(jnp.concatenate([c_prompt, c_sample], axis=0), w_ada, b_ada)
    mod_p = mod[:, :bp].reshape(n_layers, bp, 6, 1, d).transpose(0, 2, 1, 3, 4)
    mod_s = mod[:, bp:].reshape(n_layers, 1, bs, 6, d).transpose(0, 3, 1, 2, 4)

    ckt = cache_fox_k.transpose(0, 1, 3, 4, 2).reshape(n_layers, n_pool, FOX_W, PAGE_SIZE)
    cvt = cache_fox_v.transpose(0, 1, 3, 4, 2).reshape(n_layers, n_pool, FOX_W, PAGE_SIZE)
    cft = cache_fox_logf.transpose(0, 1, 3, 2)
    conv_t = state_conv.transpose(0, 2, 1, 3)
    s_view = state_gla.transpose(0, 2, 3, 4, 1)
    b_gu4 = b_gu.reshape(n_layers, N_EXPERTS, 1, 2 * D_FF)
    b_dn4 = b_down.reshape(n_layers, N_EXPERTS, 1, d)
    gf = g_final.reshape(1, d)

    row_quant = SC_WINDOW * SC_WORKERS // TOP_K
    n_buf = -(-n_tot // row_quant) * row_quant
    n_blocks = -(-(n_tot * TOP_K + N_EXPERTS * (bm - 1)) // bm)
    trash = n_blocks * bm
    n_rows = (n_blocks + 1) * bm

    yp = x_prompt.reshape(n_p, d)
    ys = x_sample.reshape(bs, d)
    kv_p = (jnp.zeros((n_layers, bp, FOX_W, seq), F32), jnp.zeros((n_layers, bp, FOX_W, seq), F32),
            jnp.zeros((n_layers, bp, FOX_HEADS, seq), F32))
    outs_p = [[] for _ in range(3)]
    outs_s = [[] for _ in range(6)]
    for l in range(n_layers):
        inw, lw, ow = _layer_weights(l, w_in, b_forget, conv_w, conv_b, lru_lambda, lru_wa, lru_ba, lru_wx,
                                     lru_bx, gla_w2, gla_b2, gla_gnorm, w_out, w_router, b_router)
        g1 = g_norm1[l].reshape(1, d)
        g2 = g_norm2[l].reshape(1, d)

        q, kt_p, vt_p, lft_p, cumt, cum, rz = _inproj_call(yp, mod_p[l], g1, inw, seq, 256, l, n_layers, kv_bufs=kv_p)
        kv_p = (kt_p, vt_p, lft_p)
        cum2 = cum.reshape(n_p, FOX_HEADS // 2, 2).transpose(1, 0, 2)
        cumt2 = cumt.reshape(bp, FOX_HEADS // 2, 2, seq)
        fo = _fox_call(l, q, kt_p, vt_p, cum2, cumt2, bp, seq, 512)
        lo, conv_p, hlast_p = _lru_call(rz, lw, bp, seq, 256)
        go, st_p = _gla_call(rz, ow["gg"], bp, seq, 256)
        y1p, h2, ti, tw = _outproj_call(yp, fo, lo, go, mod_p[l], g2, ow["w_out"], ow["w_r"], ow["b_r"],
                                        256, seq, n_buf, 0)
        st_p = st_p.reshape(bp, GLA_HEADS, HEAD_DIM, GLA_HEADS, HEAD_DIM)
        st_p = jnp.stack([st_p[:, hh, :, hh, :] for hh in range(GLA_HEADS)], axis=1).transpose(0, 1, 3, 2)
        outs_p[0].append(conv_p)
        outs_p[1].append(hlast_p.reshape(bp, LRU_W))
        outs_p[2].append(st_p)

        qs, kts, vts, lfts, _, _, rzs, ks, vs, logfs, gts = _inproj_call(
            ys, mod_s[l], g1, inw, bs, bs, 0, 1, sample=True)
        fos = _fox_dec_call(l, page_table, qs, ks, vs, logfs, ckt, cvt, cft)
        los, conv_s, h_s = _lru_step_call(l, rzs, conv_t, state_lru, lw)
        gost, s_s = _gla_step_call(l, gts, s_view, ow["ggc"])
        y1s, h2, ti, tw = _outproj_call(ys, fos, los, gost.T, mod_s[l], g2, ow["w_out"], ow["w_r"], ow["b_r"],
                                        bs, bs, n_buf, n_p, prev=(h2, ti, tw))
        outs_s[0].append(kts[0, 0])
        outs_s[1].append(vts[0, 0])
        outs_s[2].append(lfts[0, 0])
        outs_s[3].append(conv_s)
        outs_s[4].append(h_s)
        outs_s[5].append(s_s)

        dest, cnt = _rank_call(ti, n_tot, trash, 512)
        counts = cnt[0, 0:N_EXPERTS].astype(I32)
        pad_end = jnp.cumsum((counts + bm - 1) // bm * bm)
        blk_e = jnp.minimum(jnp.sum((jnp.arange(n_blocks, dtype=I32)[:, None] * bm >= pad_end[None, :]).astype(I32),
                                    axis=1), N_EXPERTS - 1).astype(I32)
        n_used = (pad_end[-1] // bm).astype(I32).reshape(1)
        dest_km = dest[0:TOP_K]
        xs = _sc_scatter_rows(h2, dest_km, n_rows)
        ye = _ffn_call(l, blk_e, n_used, xs, w_gu, b_gu4, w_down, b_dn4)
        yk = _sc_gather_rows(ye, jnp.minimum(dest_km, trash - 1).reshape(-1)).reshape(TOP_K, n_buf, d)
        final = l == n_layers - 1
        yp = _combine_call(y1p, yk, tw, mod_p[l], gf, 256, seq, 0, final)
        ys = _combine_call(y1s, yk, tw, mod_s[l], gf, bs, bs, n_p, final)

    kt_p, vt_p, lft_p = kv_p
    fox_k_p = kt_p.reshape(n_layers, bp, FOX_HEADS, HEAD_DIM, seq).transpose(0, 1, 4, 2, 3)
    fox_v_p = vt_p.reshape(n_layers, bp, FOX_HEADS, HEAD_DIM, seq).transpose(0, 1, 4, 2, 3)
    fox_f_p = lft_p.transpose(0, 1, 3, 2)
    fox_k_s = jnp.stack(outs_s[0]).reshape(n_layers, FOX_HEADS, HEAD_DIM, bs).transpose(0, 3, 1, 2)[:, :, None]
    fox_v_s = jnp.stack(outs_s[1]).reshape(n_layers, FOX_HEADS, HEAD_DIM, bs).transpose(0, 3, 1, 2)[:, :, None]
    fox_f_s = jnp.stack(outs_s[2]).transpose(0, 2, 1)[:, :, None]
    return (yp.reshape(bp, seq, d), ys.reshape(bs, 1, d),
            fox_k_p, fox_v_p, fox_f_p,
            jnp.stack(outs_p[0]), jnp.stack(outs_p[1]), jnp.stack(outs_p[2]),
            fox_k_s, fox_v_s, fox_f_s,
            jnp.stack(outs_s[3]).transpose(0, 2, 1, 3), jnp.stack(outs_s[4]),
            jnp.stack(outs_s[5]).transpose(0, 4, 1, 2, 3))
```

```python
import functools

import jax
import jax.numpy as jnp
from jax import lax
from jax.experimental import pallas as pl
from jax.experimental.pallas import tpu as pltpu
from jax.experimental.pallas import tpu_sc as plsc

F32 = jnp.float32
BF16 = jnp.bfloat16
I32 = jnp.int32

D_MODEL = 1024
HEAD_DIM = 64
FOX_W = 512
FOX_HEADS = 8
LRU_W = 256
GLA_W = 256
GLA_HEADS = 4
GLA_RANK = 16
GLA_CHUNK = 64
GLA_TAU = 16.0
LRU_C = 8.0
CONV_W = 4
N_EXPERTS = 32
TOP_K = 4
D_FF = 1024
SWIGLU_LIMIT = 7.0
SWIGLU_ALPHA = 1.702
NORM_EPS = 1e-6
PAGE_SIZE = 128
QK_SCALE = HEAD_DIM ** -0.5

_O_FQ, _O_FK, _O_FV, _O_FF = 0, 512, 1024, 1536
_O_LX, _O_LG, _O_GQ, _O_GK, _O_GV, _O_GA, _O_GOG = 1544, 1800, 2056, 2312, 2568, 2824, 2840
REC_W = 1792
GLA_T_ROWS = 5 * GLA_W
SMALL_W = 128
SMALL_T = 32
FOX_BLK = 2 * HEAD_DIM
FOX_QX = FOX_HEADS * FOX_BLK

VMEM_LIMIT = 56 * 1024 * 1024
MOE_BLOCK = 256
SC_WINDOW = 128
SC_COLS = 256
SC_WORKERS = 32

_NT = (((1,), (1,)), ((), ()))
_TN = (((0,), (0,)), ((), ()))


def _cparams(sem, vmem=VMEM_LIMIT):
    return pltpu.CompilerParams(dimension_semantics=sem, vmem_limit_bytes=vmem)


def _log_sigmoid(x):
    return jnp.minimum(x, 0.0) - jnp.log1p(jnp.exp(-jnp.abs(x)))


def _softplus(x):
    return jnp.maximum(x, 0.0) + jnp.log1p(jnp.exp(-jnp.abs(x)))


def _cumsum(x, axis):
    n = x.shape[axis]
    idx = lax.broadcasted_iota(I32, x.shape, axis)
    s = 1
    while s < n:
        x = x + jnp.where(idx >= s, pltpu.roll(x, s, axis), 0.0)
        s *= 2
    return x


_HI16 = -65536


def _pack_bf16_pairs(lo, hi):
    lo_bits = lax.bitcast_convert_type(lo.astype(BF16).astype(F32), I32)
    hi_bits = lax.bitcast_convert_type(hi.astype(BF16).astype(F32), I32)
    return lax.shift_right_logical(lo_bits, jnp.full(lo_bits.shape, 16, I32)) | (hi_bits & _HI16)


def _unpack_bf16_pairs(packed):
    lo = lax.bitcast_convert_type(lax.shift_left(packed, jnp.full(packed.shape, 16, I32)), F32)
    hi = lax.bitcast_convert_type(packed & _HI16, F32)
    return lo.astype(BF16), hi.astype(BF16)


def _mod_spec(mod, tm, tiles_per_seq, last_tile=None):
    d = mod.shape[-1]
    clamp = (lambda i: i) if last_tile is None else (lambda i: jnp.minimum(i, last_tile))
    if mod.shape[2] != 1:
        return pl.BlockSpec((6, None, tm, d), lambda i: (0, 0, clamp(i), 0))
    return pl.BlockSpec((6, None, 1, d), lambda i: (0, clamp(i) // tiles_per_seq, 0, 0))


def _ada_kernel(c_ref, w_ref, b_ref, o_ref):
    c = c_ref[...]
    a = (c * jax.nn.sigmoid(c)).astype(BF16)
    o_ref[...] = jnp.dot(a, w_ref[...].astype(BF16), preferred_element_type=F32) + b_ref[...]


def _ada_call(c_all, w_ada, b_ada):
    n_layers, d, w = w_ada.shape
    r = c_all.shape[0]
    tn = 1536
    return pl.pallas_call(
        _ada_kernel,
        grid=(n_layers, w // tn),
        in_specs=[
            pl.BlockSpec((r, d), lambda l, j: (0, 0)),
            pl.BlockSpec((None, d, tn), lambda l, j: (l, 0, j)),
            pl.BlockSpec((None, 1, tn), lambda l, j: (l, 0, j)),
        ],
        out_specs=pl.BlockSpec((None, r, tn), lambda l, j: (l, 0, j)),
        out_shape=jax.ShapeDtypeStruct((n_layers, r, w), F32),
        compiler_params=_cparams(("arbitrary", "arbitrary")),
        name="ada_mod",
    )(c_all, w_ada, b_ada.reshape(n_layers, 1, w))


def _inproj_kernel(*refs, tiles_per_seq, sample, n_alias):
    (y_ref, mod_ref, g_ref, wq_ref, wkvt_ref, wrec_ref, wsm_ref, wsmt_ref,
     bsm_ref, bfc_ref, w2_ref, b2_ref) = refs[:12]
    if sample:
        wgt_ref, w2t_ref, b2c_ref = refs[12:15]
        pos = 15 + n_alias
    else:
        eq_ref, oneq_ref, ek_ref, onek_ref = refs[12:16]
        pos = 16 + n_alias
    kt_ref, vt_ref, lft_ref, rz_ref = refs[pos:pos + 4]
    pos += 4
    if sample:
        q_ref, k_ref, v_ref, logf_ref, gt_ref = refs[pos:pos + 5]
        pos += 5
    else:
        qx_ref, kx_ref, vx_ref = refs[pos:pos + 3]
        pos += 3
    carry_c, carry_r = refs[pos:pos + 2]
    i = pl.program_id(0)

    @pl.when(i % tiles_per_seq == 0)
    def _():
        carry_c[...] = jnp.zeros_like(carry_c)
        carry_r[...] = jnp.zeros_like(carry_r)

    x = y_ref[...]
    xn = x * lax.rsqrt(jnp.mean(x * x, axis=-1, keepdims=True) + NORM_EPS) * g_ref[...]
    h = (xn * (1.0 + mod_ref[1]) + mod_ref[0]).astype(BF16)

    kt = lax.dot_general(wkvt_ref[0:FOX_W, :], h, _NT, preferred_element_type=F32)
    vt = lax.dot_general(wkvt_ref[FOX_W:2 * FOX_W, :], h, _NT, preferred_element_type=F32)
    kt_ref[...] = kt
    vt_ref[...] = vt
    rz_ref[:, 0:REC_W - GLA_W] = jnp.dot(h, wrec_ref[...], preferred_element_type=F32)

    sm = jnp.dot(h, wsm_ref[...], preferred_element_type=F32)
    lane = lax.broadcasted_iota(I32, sm.shape, 1)
    logf = jnp.where(lane < FOX_HEADS, _log_sigmoid(sm + bsm_ref[...]), 0.0)
    cum = _cumsum(logf, 0) + carry_c[...]
    carry_c[...] = cum[cum.shape[0] - 1:, :]
    glin =jnp.dot(sm.astype(BF16), w2_ref[...], preferred_element_type=F32) + b2_ref[...]
    rz_ref[:, REC_W - GLA_W:REC_W] = _log_sigmoid(glin) * (1.0 / GLA_TAU)

    smt = lax.dot_general(wsmt_ref[...], h, _NT, preferred_element_type=F32)
    lft = _log_sigmoid(smt[0:FOX_HEADS, :] + bfc_ref[...])
    lft_ref[...] = lft
    cumt = _cumsum(lft, 1) + carry_r[...]
    carry_r[...] = cumt[:, cumt.shape[1] - 1:]

    q = jnp.dot(h, wq_ref[...], preferred_element_type=F32) * QK_SCALE
    if not sample:
        def split3(c):
            hi = c.astype(BF16)
            r1 = c - hi.astype(F32)
            mid = r1.astype(BF16)
            lo = (r1 - mid.astype(F32)).astype(BF16)
            return [hi, mid, lo]

        cq = jnp.concatenate(split3(cum), axis=1)
        qx_ref[...] = (q + jnp.dot(cq, eq_ref[...], preferred_element_type=F32) + oneq_ref[...]).astype(BF16)
        ck = jnp.concatenate(split3(cumt) + [jnp.zeros(cumt.shape, BF16)], axis=0)
        kb = jnp.dot(ek_ref[...], ck, preferred_element_type=F32) + onek_ref[...]
        parts = []
        for hh in range(FOX_HEADS):
            parts += [kt[hh * HEAD_DIM:(hh + 1) * HEAD_DIM, :], kb[hh * HEAD_DIM:(hh + 1) * HEAD_DIM, :]]
        kx_ref[...] = jnp.concatenate(parts, axis=0).astype(BF16)
        vx_ref[...] = vt.astype(BF16)

    if sample:
        q_ref[...] = q.astype(BF16)
        k_ref[...] = kt.T
        v_ref[...] = vt.T
        logf_ref[...] = logf[:, 0:FOX_HEADS]
        gt_ref[0:4 * GLA_W, :] = lax.dot_general(wgt_ref[...], h, _NT, preferred_element_type=F32)
        glt = jnp.dot(w2t_ref[...], smt.astype(BF16), preferred_element_type=F32) + b2c_ref[...]
        gt_ref[4 * GLA_W:GLA_T_ROWS, :] = _log_sigmoid(glt) * (1.0 / GLA_TAU)


def _inproj_call(y, mod, g, wts, seq_len, tm, layer, n_layers, kv_bufs=None, sample=False):
    n, d = y.shape
    tiles_per_seq = seq_len // tm
    n_seq = n // seq_len
    qw = FOX_W if sample else FOX_QX
    const = lambda i: (0, 0)
    row = lambda i: (i, 0)
    seq_t = lambda i: (layer, i // tiles_per_seq, 0, i % tiles_per_seq)
    seq_t3 = lambda i: (i // tiles_per_seq, 0, i % tiles_per_seq)
    in_specs = [
        pl.BlockSpec((tm, d), row),
        _mod_spec(mod, tm, tiles_per_seq),
        pl.BlockSpec((1, d), const),
        pl.BlockSpec((d, qw), const),
        pl.BlockSpec((2 * FOX_W, d), const),
        pl.BlockSpec((d, REC_W - GLA_W), const),
        pl.BlockSpec((d, SMALL_W), const),
        pl.BlockSpec((SMALL_T, d), const),
        pl.BlockSpec((1, SMALL_W), const),
        pl.BlockSpec((FOX_HEADS, 1), const),
        pl.BlockSpec((SMALL_W, GLA_W), const),
        pl.BlockSpec((1, GLA_W), const),
    ]
    args = [y, mod, g, wts["w_q" if sample else "w_qx"], wts["w_kvt"], wts["w_rec"], wts["w_sm"], wts["w_smt"],
            wts["b_sm"], wts["b_fc"], wts["w2"], wts["b2"]]
    if sample:
        in_specs += [pl.BlockSpec((4 * GLA_W, d), const), pl.BlockSpec((GLA_W, SMALL_T), const),
                     pl.BlockSpec((GLA_W, 1), const)]
        args += [wts["w_gt"], wts["w2t"], wts["b2c"]]
    else:
        in_specs += [pl.BlockSpec((3 * SMALL_W, FOX_QX), const), pl.BlockSpec((1, FOX_QX), const),
                     pl.BlockSpec((FOX_W, SMALL_T), const), pl.BlockSpec((FOX_W, 1), const)]
        args += list(_bias_fold_consts())
    aliases = {}
    n_alias = 0
    if kv_bufs is not None:
        n_alias = 3
        first = len(args)
        in_specs += [pl.BlockSpec(memory_space=pl.ANY)] * 3
        args += list(kv_bufs)
        aliases = {first: 0, first + 1: 1, first + 2: 2}
    out_specs = [
        pl.BlockSpec((None, None, FOX_W, tm), seq_t),
        pl.BlockSpec((None, None, FOX_W, tm), seq_t),
        pl.BlockSpec((None, None, FOX_HEADS, tm), seq_t),
        pl.BlockSpec((tm, REC_W), row),
    ]
    out_shape = [
        jax.ShapeDtypeStruct((n_layers, n_seq, FOX_W, seq_len), F32),
        jax.ShapeDtypeStruct((n_layers, n_seq, FOX_W, seq_len), F32),
        jax.ShapeDtypeStruct((n_layers, n_seq, FOX_HEADS, seq_len), F32),
        jax.ShapeDtypeStruct((n, REC_W), F32),
    ]
    if sample:
        out_specs += [pl.BlockSpec((tm, FOX_W), row), pl.BlockSpec((tm, FOX_W), row), pl.BlockSpec((tm, FOX_W), row),
                      pl.BlockSpec((tm, FOX_HEADS), row), pl.BlockSpec((GLA_T_ROWS, tm), lambda i: (0, i))]
        out_shape += [jax.ShapeDtypeStruct((n, FOX_W), BF16), jax.ShapeDtypeStruct((n, FOX_W), F32),
                      jax.ShapeDtypeStruct((n, FOX_W), F32), jax.ShapeDtypeStruct((n, FOX_HEADS), F32),
                      jax.ShapeDtypeStruct((GLA_T_ROWS, n), F32)]
    else:
        out_specs += [pl.BlockSpec((tm, FOX_QX), row), pl.BlockSpec((None, FOX_QX, tm), seq_t3),
                      pl.BlockSpec((None, FOX_W, tm), seq_t3)]
        out_shape += [jax.ShapeDtypeStruct((n, FOX_QX), BF16), jax.ShapeDtypeStruct((n_seq, FOX_QX, seq_len), BF16),
                      jax.ShapeDtypeStruct((n_seq, FOX_W, seq_len), BF16)]
    return pl.pallas_call(
        functools.partial(_inproj_kernel, tiles_per_seq=tiles_per_seq, sample=sample, n_alias=n_alias),
        grid=(n // tm,),
        in_specs=in_specs,
        out_specs=tuple(out_specs),
        out_shape=tuple(out_shape),
        input_output_aliases=aliases,
        scratch_shapes=[pltpu.VMEM((1, SMALL_W), F32), pltpu.VMEM((FOX_HEADS, 1), F32)],
        compiler_params=_cparams(("arbitrary",)),
        name="in_proj",
    )(*args)


def _bias_fold_consts():
    h = jnp.arange(FOX_HEADS)
    eq = jnp.zeros((3 * SMALL_W, FOX_QX), F32)
    ek = jnp.zeros((FOX_W, SMALL_T), F32)
    oneq = jnp.zeros((1, FOX_QX), F32)
    onek = jnp.zeros((FOX_W, 1), F32)
    for piece in range(3):
        eq = eq.at[piece * SMALL_W + h, h * FOX_BLK + HEAD_DIM + piece].set(1.0)
        oneq = oneq.at[0, h * FOX_BLK + HEAD_DIM + 3 + piece].set(1.0)
        onek = onek.at[h * HEAD_DIM + piece, 0].set(1.0)
        ek = ek.at[h * HEAD_DIM + 3 + piece, piece * FOX_HEADS + h].set(-1.0)
    return eq.astype(BF16), oneq, ek.astype(BF16), onek


def _fox_kernel(qx_ref, kx_ref, vx_ref, o_ref, m_sc, l_sc, acc_sc, *, tq, tk):
    qi = pl.program_id(2)
    ki = pl.program_id(3)
    w = 2 * HEAD_DIM
    nc = tk // w

    @pl.when(ki == 0)
    def _():
        m_sc[...] = jnp.full_like(m_sc, -jnp.inf)
        l_sc[...] = jnp.zeros_like(l_sc)
        acc_sc[...] = jnp.zeros_like(acc_sc)

    def step(masked):
        vt = vx_ref[...]
        if masked:
            causal = lax.broadcasted_iota(I32, (tq, tk), 1) <= lax.broadcasted_iota(I32, (tq, tk), 0)
        for j in range(2):
            s = jnp.dot(qx_ref[:, j * FOX_BLK:(j + 1) * FOX_BLK], kx_ref[j * FOX_BLK:(j + 1) * FOX_BLK, :],
                        preferred_element_type=F32)
            if masked:
                s = jnp.where(causal, s, -jnp.inf)
            sc = [s[:, c * w:(c + 1) * w] for c in range(nc)]
            mb = sc[0]
            for c in range(1, nc):
                mb = jnp.maximum(mb, sc[c])
            m_prev = m_sc[j]
            m_new = jnp.maximum(m_prev, jnp.broadcast_to(jnp.max(mb, axis=1, keepdims=True), (tq, w)))
            ps = [jnp.exp(sc[c] - m_new) for c in range(nc)]
            lsum = ps[0]
            for c in range(1, nc):
                lsum = lsum + ps[c]
            alpha = jnp.exp(m_prev - m_new)
            l_sc[j] = alpha * l_sc[j] + jnp.broadcast_to(jnp.sum(lsum, axis=1, keepdims=True), (tq, w))
            p = jnp.concatenate([pc.astype(BF16) for pc in ps], axis=1)
            acc_sc[j] = alpha * acc_sc[j] + lax.dot_general(p, vt, _NT, preferred_element_type=F32)
            m_sc[j] = m_new

    @pl.when(ki < qi)
    def _():
        step(False)

    @pl.when(ki == qi)
    def _():
        step(True)
        lane = lax.broadcasted_iota(I32, (tq, w), 1)
        o0 = acc_sc[0] / l_sc[0]
        o1 = acc_sc[1] / l_sc[1]
        o_ref[...] = jnp.where(lane < HEAD_DIM, o0, o1).astype(BF16)


def _fox_call(qx, kx, vx, n_seq, seq_len, tq):
    n = qx.shape[0]
    nt = seq_len // tq
    tk = tq
    hp = FOX_HEADS // 2
    w = 2 * HEAD_DIM
    past = lambda b, h, i, j: (b, h, jnp.minimum(i, j))
    return pl.pallas_call(
        functools.partial(_fox_kernel, tq=tq, tk=tk),
        grid=(n_seq, hp, nt, nt),
        in_specs=[
            pl.BlockSpec((tq, 2 * FOX_BLK), lambda b, h, i, j: (b * nt + i, h)),
            pl.BlockSpec((None, 2 * FOX_BLK, tk), past),
            pl.BlockSpec((None, w, tk), past),
        ],
        out_specs=pl.BlockSpec((tq, w), lambda b, h, i, j: (b * nt + i, h)),
        out_shape=jax.ShapeDtypeStruct((n, FOX_W), BF16),
        scratch_shapes=[pltpu.VMEM((2, tq, w), F32), pltpu.VMEM((2, tq, w), F32), pltpu.VMEM((2, tq, w), F32)],
        compiler_params=_cparams(("arbitrary", "arbitrary", "arbitrary", "arbitrary")),
        name="fox_prompt",
    )(qx, kx, vx)


def _lru_gates(xc, wa_ref, ba_ref, wx_ref, bx_ref, lam_ref):
    xb = xc.astype(BF16)
    r = jax.nn.sigmoid(jnp.dot(xb, wa_ref[...], preferred_element_type=F32) + ba_ref[...])
    gi = jax.nn.sigmoid(jnp.dot(xb, wx_ref[...], preferred_element_type=F32) + bx_ref[...])
    log_a = -LRU_C * r * _softplus(-lam_ref[...])
    a = jnp.exp(log_a)
    mult = jnp.sqrt(-jnp.tanh(log_a) * (a * a + 1.0))
    return a, mult, gi


def _lru_kernel(lx_ref, lg_ref, cw_ref, cb_ref, wa_ref, ba_ref, wx_ref, bx_ref, lam_ref,
                lo_ref, conv_ref, hlast_ref, xbuf, hcar, *, tt):
    ti = pl.program_id(1)
    nt = pl.num_programs(1)

    @pl.when(ti == 0)
    def _():
        xbuf[0:8, :] = jnp.zeros((8, LRU_W), F32)
        hcar[...] = jnp.zeros_like(hcar)

    x = lx_ref[...]
    xbuf[8:8 + tt, :] = x
    xc = cb_ref[...] + cw_ref[3:4, :] * x
    for j in range(CONV_W - 1):
        xc = xc + cw_ref[j:j + 1, :] * xbuf[5 + j:5 + j + tt, :]
    xbuf[0:8, :] = x[tt - 8:tt, :]

    a, mult, gi = _lru_gates(xc, wa_ref, ba_ref, wx_ref, bx_ref, lam_ref)
    row = lax.broadcasted_iota(I32, (tt, LRU_W), 0)
    mult = jnp.where((row == 0) & (ti == 0), 1.0, mult)
    b = mult * gi * xc
    s = 1
    while s < tt:
        keep = row >= s
        a_sh = jnp.where(keep, pltpu.roll(a, s, 0), 1.0)
        b_sh = jnp.where(keep, pltpu.roll(b, s, 0), 0.0)
        b = a * b_sh + b
        a = a * a_sh
        s *= 2
    h = a * hcar[...] + b
    hcar[...] = h[tt - 1:tt, :]
    lo_ref[...] = (h * jax.nn.gelu(lg_ref[...])).astype(BF16)

    @pl.when(ti == nt - 1)
    def _():
        conv_ref[...] = x[tt - (CONV_W - 1):tt, :]
        hlast_ref[...] = h[tt - 1:tt, :]


def _lru_weight_specs(const):
    return [
        pl.BlockSpec((CONV_W, LRU_W), const), pl.BlockSpec((1, LRU_W), const),
        pl.BlockSpec((LRU_W, LRU_W), const), pl.BlockSpec((1, LRU_W), const),
        pl.BlockSpec((LRU_W, LRU_W), const), pl.BlockSpec((1, LRU_W), const),
        pl.BlockSpec((1, LRU_W), const),
    ]


def _lru_weight_args(lw):
    return [lw["conv_w"], lw["conv_b"], lw["wa"], lw["ba"], lw["wx"], lw["bx"], lw["lam"]]


def _lru_call(rz, lw, n_seq, seq_len, tt):
    n = rz.shape[0]
    nt = seq_len // tt
    return pl.pallas_call(
        functools.partial(_lru_kernel, tt=tt),
        grid=(n_seq, nt),
        in_specs=[
            pl.BlockSpec((tt, LRU_W), lambda b, t: (b * nt + t, 0)),
            pl.BlockSpec((tt, LRU_W), lambda b, t: (b * nt + t, 1)),
        ] + _lru_weight_specs(lambda b, t: (0, 0)),
        out_specs=(
            pl.BlockSpec((tt, LRU_W), lambda b, t: (b * nt + t, 0)),
            pl.BlockSpec((None, CONV_W - 1, LRU_W), lambda b, t: (b, 0, 0)),
            pl.BlockSpec((None, 1, LRU_W), lambda b, t: (b, 0, 0)),
        ),
        out_shape=(
            jax.ShapeDtypeStruct((n, LRU_W), BF16),
            jax.ShapeDtypeStruct((n_seq, CONV_W - 1, LRU_W), F32),
            jax.ShapeDtypeStruct((n_seq, 1, LRU_W), F32),
        ),
        scratch_shapes=[pltpu.VMEM((tt + 8, LRU_W), F32), pltpu.VMEM((1, LRU_W), F32)],
        compiler_params=_cparams(("arbitrary", "arbitrary")),
        name="lru_prompt",
    )(rz, rz, *_lru_weight_args(lw))


def _head_rms_gate(o, gg_ref, gog):
    lane = lax.broadcasted_iota(I32, o.shape, 1)
    o2 = o * o
    rs = jnp.zeros_like(o)
    for hh in range(GLA_HEADS):
        mh = lane // HEAD_DIM == hh
        ms = jnp.sum(jnp.where(mh, o2, 0.0), axis=1, keepdims=True) * (1.0 / HEAD_DIM)
        rs = jnp.where(mh, lax.rsqrt(ms + NORM_EPS), rs)
    return o * rs * gg_ref[...] * (gog * jax.nn.sigmoid(gog))


def _gla_kernel(gq_ref, gk_ref, gv_ref, gog_ref, gl_ref, gg_ref, go_ref, st_ref, s_sc, *, tt):
    ti = pl.program_id(1)
    nt = pl.num_programs(1)
    c = GLA_CHUNK

    @pl.when(ti == 0)
    def _():
        s_sc[...] = jnp.zeros_like(s_sc)

    lane = lax.broadcasted_iota(I32, (c, GLA_W), 1)
    r2 = lax.broadcasted_iota(I32, (GLA_W, GLA_W), 0)
    c2 = lax.broadcasted_iota(I32, (GLA_W, GLA_W), 1)
    same_head = (r2 // HEAD_DIM) == (c2 // HEAD_DIM)
    tril = lax.broadcasted_iota(I32, (c, c), 1) <= lax.broadcasted_iota(I32, (c, c), 0)

    for ci in range(tt // c):
        sl = slice(ci * c, (ci + 1) * c)
        q = gq_ref[sl, :] * QK_SCALE
        k = gk_ref[sl, :]
        v = gv_ref[sl, :].astype(BF16)
        bc = _cumsum(gl_ref[sl, :], 0)
        b_last = bc[c - 1:c, :]
        qd = (q * jnp.exp(bc)).astype(BF16)
        kinv = (k * jnp.exp(-bc)).astype(BF16)
        kdec = (k * jnp.exp(b_last - bc)).astype(BF16)
        s_prev = s_sc[...]
        o = lax.dot_general(qd, s_prev.astype(BF16), _NT, preferred_element_type=F32)
        for hh in range(GLA_HEADS):
            mh = lane // HEAD_DIM == hh
            att = lax.dot_general(jnp.where(mh, qd, jnp.zeros_like(qd)), kinv, _NT, preferred_element_type=F32)
            att = jnp.where(tril, att, 0.0).astype(BF16)
            o = o + jnp.dot(att, jnp.where(mh, v, jnp.zeros_like(v)), preferred_element_type=F32)
        ut = lax.dot_general(v, kdec, _TN, preferred_element_type=F32)
        s_sc[...] = s_prev * jnp.exp(b_last) + jnp.where(same_head, ut, 0.0)
        go_ref[sl, :] = _head_rms_gate(o, gg_ref, gog_ref[sl, :]).astype(BF16)

    @pl.when(ti == nt - 1)
    def _():
        st_ref[...] = s_sc[...]


def _gla_call(rz, gg, n_seq, seq_len, tt):
    n = rz.shape[0]
    nt = seq_len // tt

    def col(j):
        return pl.BlockSpec((tt, GLA_W), lambda b, t: (b * nt + t, j))

    return pl.pallas_call(
        functools.partial(_gla_kernel, tt=tt),
        grid=(n_seq, nt),
        in_specs=[col(2), col(3), col(4), col(5), col(6), pl.BlockSpec((1, GLA_W), lambda b, t: (0, 0))],
        out_specs=(
            pl.BlockSpec((tt, GLA_W), lambda b, t: (b * nt + t, 0)),
            pl.BlockSpec((None, GLA_W, GLA_W), lambda b, t: (b, 0, 0)),
        ),
        out_shape=(
            jax.ShapeDtypeStruct((n, GLA_W), BF16),
            jax.ShapeDtypeStruct((n_seq, GLA_W, GLA_W), F32),
        ),
        scratch_shapes=[pltpu.VMEM((GLA_W, GLA_W), F32)],
        compiler_params=_cparams(("arbitrary", "arbitrary")),
        name="gla_prompt",
    )(rz, rz, rz, rz, rz, gg)


def _fox_dec_kernel(pt_ref, q_ref, kn_ref, vn_ref, dn_ref, *refs, n_pages):
    del pt_ref
    k_refs = refs[0:n_pages]
    v_refs = refs[n_pages:2 * n_pages]
    f_refs = refs[2 * n_pages:3 * n_pages]
    o_ref = refs[3 * n_pages]
    w = FOX_W
    hrow = lax.broadcasted_iota(I32, (FOX_HEADS, w), 0)
    hlane = lax.broadcasted_iota(I32, (FOX_HEADS, w), 1) // HEAD_DIM
    diag = hrow == hlane
    q = q_ref[...].astype(F32)
    qbd = jnp.where(diag, jnp.broadcast_to(q, (FOX_HEADS, w)), 0.0).astype(BF16)
    s = jnp.concatenate(
        [jnp.dot(qbd, k_refs[p][...].astype(BF16), preferred_element_type=F32) for p in range(n_pages)], axis=1)
    lf = jnp.concatenate([f_refs[p][...] for p in range(n_pages)], axis=1)
    cs = _cumsum(lf, 1)
    suffix = cs[:, cs.shape[1] - 1:] - cs
    s = s + dn_ref[...] + suffix
    s_new = jnp.sum(qbd.astype(F32) * kn_ref[...], axis=1, keepdims=True)
    m = jnp.maximum(jnp.max(s, axis=1, keepdims=True), s_new)
    p_past = jnp.exp(s - m)
    p_new = jnp.exp(s_new - m)
    denom = jnp.sum(p_past, axis=1, keepdims=True) + p_new
    acc = p_new * vn_ref[...]
    pb = p_past.astype(BF16)
    for p in range(n_pages):
        acc = acc + lax.dot_general(pb[:, p * PAGE_SIZE:(p + 1) * PAGE_SIZE], v_refs[p][...].astype(BF16), _NT,
                                    preferred_element_type=F32)
    out = jnp.where(diag, acc / denom, 0.0)
    o_ref[...] = jnp.sum(out, axis=0, keepdims=True).astype(BF16)


def _fox_dec_call(layer, page_table, q, k_new, v_new, logf_new, cache_kt, cache_vt, cache_ft):
    bd, n_pages = page_table.shape
    w = FOX_W

    def page_spec(rows, j):
        return pl.BlockSpec((None, None, rows, PAGE_SIZE), lambda b, pt, j=j: (layer, pt[b, j], 0, 0))

    row = lambda b, pt: (b, 0, 0)
    in_specs = [
        pl.BlockSpec((None, 1, w), row),
        pl.BlockSpec((None, 1, w), row),
        pl.BlockSpec((None, 1, w), row),
        pl.BlockSpec((None, FOX_HEADS, 1), row),
    ]
    in_specs += [page_spec(w, j) for j in range(n_pages)]
    in_specs += [page_spec(w, j) for j in range(n_pages)]
    in_specs += [page_spec(FOX_HEADS, j) for j in range(n_pages)]
    grid_spec = pltpu.PrefetchScalarGridSpec(
        num_scalar_prefetch=1,
        grid=(bd,),
        in_specs=in_specs,
        out_specs=pl.BlockSpec((None, 1, w), row),
    )
    out = pl.pallas_call(
        functools.partial(_fox_dec_kernel, n_pages=n_pages),
        grid_spec=grid_spec,
        out_shape=jax.ShapeDtypeStruct((bd, 1, w), BF16),
        compiler_params=_cparams(("arbitrary",)),
        name="fox_sample",
    )(page_table, q.reshape(bd, 1, w), k_new.reshape(bd, 1, w), v_new.reshape(bd, 1, w),
      logf_new.reshape(bd, FOX_HEADS, 1),
      *([cache_kt] * n_pages), *([cache_vt] * n_pages), *([cache_ft] * n_pages))
    return out.reshape(bd, w)


def _lru_step_kernel(lx_ref, lg_ref, conv_ref, h0_ref, cw_ref, cb_ref, wa_ref, ba_ref, wx_ref, bx_ref, lam_ref,
                     lo_ref, convn_ref, hn_ref):
    x = lx_ref[...]
    xc = cb_ref[...] + cw_ref[3:4, :] * x
    for j in range(CONV_W - 1):
        xc = xc + cw_ref[j:j + 1, :] * conv_ref[j]
    convn_ref[0] = conv_ref[1]
    convn_ref[1] = conv_ref[2]
    convn_ref[2] = x
    a, mult, gi = _lru_gates(xc, wa_ref, ba_ref, wx_ref, bx_ref, lam_ref)
    h = a * h0_ref[...] + mult * gi * xc
    hn_ref[...] = h
    lo_ref[...] = (h * jax.nn.gelu(lg_ref[...])).astype(BF16)


def _lru_step_call(layer, rz, conv_t, h0, lw):
    bd = rz.shape[0]
    return pl.pallas_call(
        _lru_step_kernel,
        grid=(1,),
        in_specs=[
            pl.BlockSpec((bd, LRU_W), lambda i: (0, 0)),
            pl.BlockSpec((bd, LRU_W), lambda i: (0, 1)),
            pl.BlockSpec((None, CONV_W - 1, bd, LRU_W), lambda i: (layer, 0, 0, 0)),
            pl.BlockSpec((None, bd, LRU_W), lambda i: (layer, 0, 0)),
        ] + _lru_weight_specs(lambda i: (0, 0)),
        out_specs=(
            pl.BlockSpec((bd, LRU_W), lambda i: (0, 0)),
            pl.BlockSpec((CONV_W - 1, bd, LRU_W), lambda i: (0, 0, 0)),
            pl.BlockSpec((bd, LRU_W), lambda i: (0, 0)),
        ),
        out_shape=(
            jax.ShapeDtypeStruct((bd, LRU_W), BF16),
            jax.ShapeDtypeStruct((CONV_W - 1, bd, LRU_W), F32),
            jax.ShapeDtypeStruct((bd, LRU_W), F32),
        ),
        compiler_params=_cparams(("arbitrary",)),
        name="lru_step",
    )(rz, rz, conv_t, h0, *_lru_weight_args(lw))


def _gla_step_kernel(q_ref, k_ref, v_ref, gog_ref, gl_ref, gg_ref, s_ref, go_ref, sn_ref):
    eg = jnp.exp(gl_ref[...])
    kt = k_ref[...]
    qt = q_ref[...] * QK_SCALE
    vt = v_ref[...]
    o = jnp.zeros_like(vt)
    for kk in range(HEAD_DIM):
        s_new = eg[kk:kk + 1, :] * s_ref[kk] + kt[kk:kk + 1, :] * vt
        sn_ref[kk] = s_new
        o = o + qt[kk:kk + 1, :] * s_new
    ms = jnp.mean(o * o, axis=0, keepdims=True)
    gog = gog_ref[...]
    go_ref[...] = o * lax.rsqrt(ms + NORM_EPS) * gg_ref[...] * (gog * jax.nn.sigmoid(gog))


def _gla_step_call(layer, gt, s_view, ggc):
    bd = gt.shape[1]
    hd = HEAD_DIM

    def part(j):
        return pl.BlockSpec((hd, bd), lambda h, j=j: (j * GLA_HEADS + h, 0))

    return pl.pallas_call(
        _gla_step_kernel,
        grid=(GLA_HEADS,),
        in_specs=[part(0), part(1), part(2), part(3), part(4),
                  pl.BlockSpec((hd, 1), lambda h: (h, 0)),
                  pl.BlockSpec((None, None, hd, hd, bd), lambda h: (layer, h, 0, 0, 0))],
        out_specs=(
            pl.BlockSpec((hd, bd), lambda h: (h, 0)),
            pl.BlockSpec((None, hd, hd, bd), lambda h: (h, 0, 0, 0)),
        ),
        out_shape=(
            jax.ShapeDtypeStruct((GLA_W, bd), F32),
            jax.ShapeDtypeStruct((GLA_HEADS, hd, hd, bd), F32),
        ),
        compiler_params=_cparams(("arbitrary",)),
        name="gla_step",
    )(gt, gt, gt, gt, gt, ggc, s_view)


def _outproj_kernel(*refs, n_alias, n_tiles):
    h2_ref, ti_ref, tw_ref = refs[10 + n_alias:13 + n_alias]
    i = pl.program_id(0)

    @pl.when(i < n_tiles)
    def _():
        _outproj_tile(*refs[0:9], *refs[9 + n_alias:13 + n_alias])

    @pl.when(i >= n_tiles)
    def _():
        h2_ref[...] = jnp.zeros_like(h2_ref)
        ti_ref[...] = jnp.zeros_like(ti_ref)
        tw_ref[...] = jnp.zeros_like(tw_ref)


def _outproj_tile(y_ref, fo_ref, lo_ref, go_ref, mod_ref, g2_ref, wo_ref, wr_ref, br_ref,
                  y1_ref, h2_ref, ti_ref, tw_ref):
    m = jnp.dot(fo_ref[...], wo_ref[0:FOX_W, :], preferred_element_type=F32)
    m = m + jnp.dot(lo_ref[...], wo_ref[FOX_W:FOX_W + LRU_W, :], preferred_element_type=F32)
    m = m + jnp.dot(go_ref[...].astype(BF16), wo_ref[FOX_W + LRU_W:, :], preferred_element_type=F32)
    y1 = y_ref[...] + mod_ref[2] * m
    y1_ref[...] = y1
    xn = y1 * lax.rsqrt(jnp.mean(y1 * y1, axis=-1, keepdims=True) + NORM_EPS) * g2_ref[...]
    h2 = xn * (1.0 + mod_ref[4]) + mod_ref[3]
    half = h2.shape[1] // 2
    h2_ref[...] = _pack_bf16_pairs(h2[:, 0:half], h2[:, half:])
    logits = jnp.dot(h2.astype(BF16), wr_ref[...], preferred_element_type=F32) + br_ref[...]
    lane = lax.broadcasted_iota(I32, logits.shape, 1)
    logits = jnp.where(lane < N_EXPERTS, logits, -jnp.inf)
    idx_out = jnp.zeros(logits.shape, I32)
    val_out = jnp.zeros(logits.shape, F32)
    vals = []
    for kk in range(TOP_K):
        mx = jnp.max(logits, axis=1, keepdims=True)
        sel = jnp.min(jnp.where(logits == mx, lane, SMALL_W), axis=1, keepdims=True)
        idx_out = jnp.where(lane == kk, sel, idx_out)
        vals.append(mx)
        logits = jnp.where(lane == sel, -jnp.inf, logits)
    es = [jnp.exp(vv - vals[0]) for vv in vals]
    tot = es[0] + es[1] + es[2] + es[3]
    for kk in range(TOP_K):
        val_out = jnp.where(lane == kk, es[kk] / tot, val_out)
    ti_ref[...] = idx_out
    tw_ref[...] = val_out


def _outproj_call(y, fo, lo, go, mod, g2, w_out, w_r, b_r, tm, seq_len, n_buf, row_off, prev=None):
    n, d = y.shape
    tiles_per_seq = max(seq_len // tm, 1)
    n_tiles = n // tm
    n_steps = n_tiles if prev is not None else n_buf // tm
    const = lambda i: (0, 0)
    row = lambda i: (jnp.minimum(i, n_tiles - 1), 0)
    off = row_off // tm
    orow = lambda i: (i + off, 0)
    in_specs = [
        pl.BlockSpec((tm, d), row),
        pl.BlockSpec((tm, FOX_W), row),
        pl.BlockSpec((tm, LRU_W), row),
        pl.BlockSpec((tm, GLA_W), row),
        _mod_spec(mod, tm, tiles_per_seq, n_tiles - 1),
        pl.BlockSpec((1, d), const),
        pl.BlockSpec((d, d), const),
        pl.BlockSpec((d, SMALL_W), const),
        pl.BlockSpec((1, SMALL_W), const),
    ]
    args = [y, fo, lo, go, mod, g2, w_out, w_r, b_r]
    aliases = {}
    if prev is not None:
        in_specs += [pl.BlockSpec(memory_space=pl.ANY)] * 3
        args += list(prev)
        aliases = {9: 1, 10: 2, 11: 3}
    return pl.pallas_call(
        functools.partial(_outproj_kernel, n_alias=0 if prev is None else 3, n_tiles=n_tiles),
        grid=(n_steps,),
        in_specs=in_specs,
        out_specs=(
            pl.BlockSpec((tm, d), row),
            pl.BlockSpec((tm, d // 2), orow),
            pl.BlockSpec((tm, SMALL_W), orow),
            pl.BlockSpec((tm, SMALL_W), orow),
        ),
        out_shape=(
            jax.ShapeDtypeStruct((n, d), F32),
            jax.ShapeDtypeStruct((n_buf, d // 2), I32),
            jax.ShapeDtypeStruct((n_buf, SMALL_W), I32),
            jax.ShapeDtypeStruct((n_buf, SMALL_W), F32),
        ),
        input_output_aliases=aliases,
        compiler_params=_cparams(("arbitrary",)),
        name="out_proj_router",
    )(*args)


def _rank_kernel(ti_ref, dest_ref, cnt_ref, carry, *, tm, n_valid, trash):
    p = pl.program_id(0)
    i = pl.program_id(1)
    nt = pl.num_programs(1)

    @pl.when((p == 0) & (i == 0))
    def _():
        carry[...] = jnp.zeros_like(carry)

    @pl.when((p == 1) & (i == 0))
    def _():
        cnt = carry[...]
        cnt_ref[...] = cnt
        padded = jnp.floor((cnt + (MOE_BLOCK - 1.0)) * (1.0 / MOE_BLOCK)) * MOE_BLOCK
        carry[...] = _cumsum(padded, 1) - padded

    t = ti_ref[...]
    lane = lax.broadcasted_iota(I32, (tm, SMALL_W), 1)
    valid = (lax.broadcasted_iota(I32, (tm, 1), 0) + i * tm) < n_valid
    ohs = [jnp.where(valid, (lane == t[:, kk:kk + 1]).astype(F32), 0.0) for kk in range(TOP_K)]

    @pl.when(p == 0)
    def _():
        carry[...] = carry[...] + jnp.sum(sum(ohs[1:], ohs[0]), axis=0, keepdims=True)

    @pl.when(p == 1)
    def _():
        before = (lax.broadcasted_iota(I32, (tm, tm), 1) < lax.broadcasted_iota(I32, (tm, tm), 0)).astype(BF16)
        base = carry[...]
        out = jnp.zeros((tm, SMALL_W), F32)
        for kk in range(TOP_K):
            oh = ohs[kk]
            pre = jnp.dot(before, oh.astype(BF16), preferred_element_type=F32)
            slot = jnp.sum(oh * (pre + base), axis=1, keepdims=True)
            out = jnp.where(lane == kk, jnp.where(valid, slot, float(trash)), out)
            base = base + jnp.sum(oh, axis=0, keepdims=True)
        carry[...] = base
        dest_ref[...] = out.T[0:8, :].astype(I32)


def _rank_call(ti, n_valid, trash, tm):
    n_buf = ti.shape[0]
    return pl.pallas_call(
        functools.partial(_rank_kernel, tm=tm, n_valid=n_valid, trash=trash),
        grid=(2, n_buf // tm),
        in_specs=[pl.BlockSpec((tm, SMALL_W), lambda p, i: (i, 0))],
        out_specs=(
            pl.BlockSpec((8, tm), lambda p, i: (0, i * p)),
            pl.BlockSpec((1, SMALL_W), lambda p, i: (0, 0)),
        ),
        out_shape=(
            jax.ShapeDtypeStruct((8, n_buf), I32),
            jax.ShapeDtypeStruct((1, SMALL_W), F32),
        ),
        scratch_shapes=[pltpu.VMEM((1, SMALL_W), F32)],
        compiler_params=_cparams(("arbitrary", "arbitrary")),
        name="moe_rank",
    )(ti)


def _sc_mesh():
    return plsc.VectorSubcoreMesh(core_axis_name="core", subcore_axis_name="subcore")


def _sc_scatter_rows(x, dest_km, n_out):
    kk, n = dest_km.shape
    d = x.shape[1]
    nb = n // SC_WINDOW
    assert (kk * nb) % SC_WORKERS == 0 and d % SC_COLS == 0

    @pl.kernel(out_type=jax.ShapeDtypeStruct((n_out, d), x.dtype), mesh=_sc_mesh())
    def scatter_kernel(x_hbm, i_hbm, o_hbm):
        def body(x_vmem, i_vmem):
            j = pl.program_id(1)
            pltpu.sync_copy(x_vmem, o_hbm.at[i_vmem.at[0], pl.ds(j * SC_COLS, SC_COLS)])

        pltpu.emit_pipeline(
            body,
            grid=(kk * nb, d // SC_COLS),
            in_specs=[pl.BlockSpec((SC_WINDOW, SC_COLS), lambda g, j: (g % nb, j)),
                      pl.BlockSpec((1, SC_WINDOW), lambda g, j: (g // nb, g % nb))],
            out_specs=[],
            core_axis_name=("core", "subcore"),
            dimension_semantics=(pltpu.PARALLEL, pltpu.ARBITRARY),
        )(x_hbm, i_hbm)

    return scatter_kernel(x, dest_km)


def _sc_gather_rows(x, idx):
    n = idx.shape[0]
    d = x.shape[1]
    assert (n // SC_WINDOW) % SC_WORKERS == 0 and d % SC_COLS == 0

    @pl.kernel(out_type=jax.ShapeDtypeStruct((n, d), x.dtype), mesh=_sc_mesh())
    def gather_kernel(x_hbm, i_hbm, o_hbm):
        def body(i_vmem, o_vmem):
            j = pl.program_id(1)
            pltpu.sync_copy(x_hbm.at[i_vmem.at[0], pl.ds(j * SC_COLS, SC_COLS)], o_vmem)

        pltpu.emit_pipeline(
            body,
            grid=(n // SC_WINDOW, d // SC_COLS),
            in_specs=[pl.BlockSpec((1, SC_WINDOW), lambda i, j: (0, i))],
            out_specs=[pl.BlockSpec((SC_WINDOW, SC_COLS), lambda i, j: (i, j))],
            core_axis_name=("core", "subcore"),
            dimension_semantics=(pltpu.PARALLEL, pltpu.ARBITRARY),
        )(i_hbm, o_hbm)

    return gather_kernel(x, idx.reshape(1, n))


def _ffn_kernel(be_ref, nu_ref, x_ref, wgu_ref, bgu_ref, wdn_ref, bdn_ref, y_ref, wgu_b, wdn_b):
    j = pl.program_id(0)
    e = be_ref[j]
    prev = be_ref[jnp.maximum(j - 1, 0)]

    @pl.when((j == 0) | (e != prev))
    def _():
        wgu_b[...] = wgu_ref[...].astype(BF16)
        wdn_b[...] = wdn_ref[...].astype(BF16)

    @pl.when(j < nu_ref[0])
    def _():
        xp = x_ref[...]
        half = xp.shape[1]
        x_lo, x_hi = _unpack_bf16_pairs(xp)
        gu = (jnp.dot(x_lo, wgu_b[0:half, :], preferred_element_type=F32)
              + jnp.dot(x_hi, wgu_b[half:, :], preferred_element_type=F32) + bgu_ref[...])
        g = jnp.minimum(gu[:, 0:D_FF], SWIGLU_LIMIT)
        u = jnp.clip(gu[:, D_FF:], -SWIGLU_LIMIT, SWIGLU_LIMIT)
        act = g * jax.nn.sigmoid(SWIGLU_ALPHA * g)
        hmid = ((u + 1.0) * act).astype(BF16)
        y_ref[...] = jnp.dot(hmid, wdn_b[...], preferred_element_type=F32) + bdn_ref[...]

    @pl.when(j >= nu_ref[0])
    def _():
        y_ref[...] = jnp.zeros_like(y_ref)


def _ffn_call(layer, blk_e, n_used, xs, w_gu, b_gu, w_dn, b_dn):
    n_blocks = blk_e.shape[0]
    d = 2 * xs.shape[1]
    bm = MOE_BLOCK
    grid_spec = pltpu.PrefetchScalarGridSpec(
        num_scalar_prefetch=2,
        grid=(n_blocks,),
        in_specs=[
            pl.BlockSpec((bm, d // 2), lambda j, be, nu: (j, 0)),
            pl.BlockSpec((None, None, d, 2 * D_FF), lambda j, be, nu: (layer, be[j], 0, 0)),
            pl.BlockSpec((None, None, 1, 2 * D_FF), lambda j, be, nu: (layer, be[j], 0, 0)),
            pl.BlockSpec((None, None, D_FF, d), lambda j, be, nu: (layer, be[j], 0, 0)),
            pl.BlockSpec((None, None, 1, d), lambda j, be, nu: (layer, be[j], 0, 0)),
        ],
        out_specs=pl.BlockSpec((bm, d), lambda j, be, nu: (j, 0)),
        scratch_shapes=[pltpu.VMEM((d, 2 * D_FF), BF16), pltpu.VMEM((D_FF, d), BF16)],
    )
    return pl.pallas_call(
        _ffn_kernel,
        grid_spec=grid_spec,
        out_shape=jax.ShapeDtypeStruct((n_blocks * bm, d), F32),
        compiler_params=_cparams(("arbitrary",)),
        name="expert_ffn",
    )(blk_e, n_used, xs, w_gu, b_gu, w_dn, b_dn)


def _combine_kernel(y1_ref, yk_ref, tw_ref, mod_ref, gf_ref, o_ref, *, final):
    tw = tw_ref[...]
    acc = tw[:, 0:1] * yk_ref[0]
    for kk in range(1, TOP_K):
        acc = acc + tw[:, kk:kk + 1] * yk_ref[kk]
    y2 = y1_ref[...] + mod_ref[5] * acc
    if final:
        y2 = y2 * lax.rsqrt(jnp.mean(y2 * y2, axis=-1, keepdims=True) + NORM_EPS) * gf_ref[...]
    o_ref[...] = y2


def _combine_call(y1, yk, tw, mod, g_final, tm, seq_len, row_off, final):
    n, d = y1.shape
    tiles_per_seq = max(seq_len // tm, 1)
    off = row_off // tm
    return pl.pallas_call(
        functools.partial(_combine_kernel, final=final),
        grid=(n // tm,),
        in_specs=[
            pl.BlockSpec((tm, d), lambda i: (i, 0)),
            pl.BlockSpec((TOP_K, tm, d), lambda i: (0, i + off, 0)),
            pl.BlockSpec((tm, SMALL_W), lambda i: (i + off, 0)),
            _mod_spec(mod, tm, tiles_per_seq),
            pl.BlockSpec((1, d), lambda i: (0, 0)),
        ],
        out_specs=pl.BlockSpec((tm, d), lambda i: (i, 0)),
        out_shape=jax.ShapeDtypeStruct((n, d), F32),
        compiler_params=_cparams(("arbitrary",)),
        name="moe_combine",
    )(y1, yk, tw, mod, g_final)


def _block_diag(w):
    nb, bw, _ = w.shape
    eye = jnp.eye(nb, dtype=w.dtype)
    return (eye[:, None, :, None] * w[:, :, None, :]).reshape(nb * bw, nb * bw)


def _layer_weights(l, w_in, b_forget, conv_w, conv_b, lru_lambda, lru_wa, lru_ba, lru_wx, lru_bx,
                   gla_w2, gla_b2, gla_gnorm, w_out, w_router, b_router):
    wi = w_in[l]
    d = wi.shape[0]
    w_ff = wi[:, _O_FF:_O_LX]
    w_ga = wi[:, _O_GA:_O_GOG]
    w_rec = jnp.concatenate([wi[:, _O_LX:_O_GA], wi[:, _O_GOG:]], axis=1)
    w2 = jnp.zeros((SMALL_W, GLA_W), F32).at[FOX_HEADS:FOX_HEADS + GLA_RANK].set(gla_w2[l])
    inw = {
        "w_q": wi[:, _O_FQ:_O_FK].astype(BF16),
        "w_qx": jnp.pad(wi[:, _O_FQ:_O_FK].reshape(d, FOX_HEADS, HEAD_DIM),
                        ((0, 0), (0, 0), (0, FOX_BLK - HEAD_DIM))).reshape(d, FOX_QX).astype(BF16),
        "w_kvt": wi[:, _O_FK:_O_FF].T.astype(BF16),
        "w_rec": w_rec.astype(BF16),
        "w_sm": jnp.concatenate([w_ff, w_ga, jnp.zeros((d, SMALL_W - FOX_HEADS - GLA_RANK), F32)], axis=1).astype(BF16),
        "w_smt": jnp.concatenate([w_ff, w_ga, jnp.zeros((d, SMALL_T - FOX_HEADS - GLA_RANK), F32)], axis=1).T.astype(BF16),
        "b_sm": jnp.zeros((1, SMALL_W), F32).at[0, 0:FOX_HEADS].set(b_forget[l]),
        "b_fc": b_forget[l].reshape(FOX_HEADS, 1),
        "w2": w2.astype(BF16),
        "b2": gla_b2[l].reshape(1, GLA_W),
        "w_gt": w_rec[:, 2 * GLA_W:].T.astype(BF16),
        "w2t": w2[0:SMALL_T].T.astype(BF16),
        "b2c": gla_b2[l].reshape(GLA_W, 1),
    }
    lw = {
        "conv_w": conv_w[l], "conv_b": conv_b[l].reshape(1, LRU_W),
        "wa": _block_diag(lru_wa[l]).astype(BF16), "ba": lru_ba[l].reshape(1, LRU_W),
        "wx": _block_diag(lru_wx[l]).astype(BF16), "bx": lru_bx[l].reshape(1, LRU_W),
        "lam": lru_lambda[l].reshape(1, LRU_W),
    }
    ow = {
        "gg": gla_gnorm[l].reshape(1, GLA_W),
        "ggc": gla_gnorm[l].reshape(GLA_W, 1),
        "w_out": w_out[l].astype(BF16),
        "w_r": jnp.concatenate([w_router[l], jnp.zeros((d, SMALL_W - N_EXPERTS), F32)], axis=1).astype(BF16),
        "b_r": jnp.zeros((1, SMALL_W), F32).at[0, 0:N_EXPERTS].set(b_router[l]),
    }
    return inw, lw, ow


def kernel(x_prompt, x_sample, cache_fox_k, cache_fox_v, cache_fox_logf, state_conv, state_lru, state_gla, page_table, c_prompt, c_sample, w_ada, b_ada, g_norm1, g_norm2, w_in, b_forget, conv_w, conv_b, lru_lambda, lru_wa, lru_ba, lru_wx, lru_bx, gla_w2, gla_b2, gla_gnorm, w_out, w_router, b_router, w_gu, b_gu, w_down, b_down, g_final):
    n_layers = w_ada.shape[0]
    bp, seq, d = x_prompt.shape
    bs = x_sample.shape[0]
    n_p = bp * seq
    n_tot = n_p + bs
    n_pool = cache_fox_k.shape[1]
    bm = MOE_BLOCK

    mod = _ada_call(jnp.concatenate([c_prompt, c_sample], axis=0), w_ada, b_ada)
    mod_p = mod[:, :bp].reshape(n_layers, bp, 6, 1, d).transpose(0, 2, 1, 3, 4)
    mod_s = mod[:, bp:].reshape(n_layers, 1, bs, 6, d).transpose(0, 3, 1, 2, 4)

    ckt = cache_fox_k.transpose(0, 1, 3, 4, 2).reshape(n_layers, n_pool, FOX_W, PAGE_SIZE)
    cvt = cache_fox_v.transpose(0, 1, 3, 4, 2).reshape(n_layers, n_pool, FOX_W, PAGE_SIZE)
    cft = cache_fox_logf.transpose(0, 1, 3, 2)
    conv_t = state_conv.transpose(0, 2, 1, 3)
    s_view = state_gla.transpose(0, 2, 3, 4, 1)
    b_gu4 = b_gu.reshape(n_layers, N_EXPERTS, 1, 2 * D_FF)
    b_dn4 = b_down.reshape(n_layers, N_EXPERTS, 1, d)
    gf = g_final.reshape(1, d)

    row_quant = SC_WINDOW * SC_WORKERS // TOP_K
    n_buf = -(-n_tot // row_quant) * row_quant
    n_blocks = -(-(n_tot * TOP_K + N_EXPERTS * (bm - 1)) // bm)
    trash = n_blocks * bm
    n_rows = (n_blocks + 1) * bm

    yp = x_prompt.reshape(n_p, d)
    ys = x_sample.reshape(bs, d)
    kv_p = (jnp.zeros((n_layers, bp, FOX_W, seq), F32), jnp.zeros((n_layers, bp, FOX_W, seq), F32),
            jnp.zeros((n_layers, bp, FOX_HEADS, seq), F32))
    outs_p = [[] for _ in range(3)]
    outs_s = [[] for _ in range(6)]
    for l in range(n_layers):
        inw, lw, ow = _layer_weights(l, w_in, b_forget, conv_w, conv_b, lru_lambda, lru_wa, lru_ba, lru_wx,
                                     lru_bx, gla_w2, gla_b2, gla_gnorm, w_out, w_router, b_router)
        g1 = g_norm1[l].reshape(1, d)
        g2 = g_norm2[l].reshape(1, d)

        kt_p, vt_p, lft_p, rz, qx, kx, vx = _inproj_call(yp, mod_p[l], g1, inw, seq, 256, l, n_layers, kv_bufs=kv_p)
        kv_p = (kt_p, vt_p, lft_p)
        fo = _fox_call(qx, kx, vx, bp, seq, 512)
        lo, conv_p, hlast_p = _lru_call(rz, lw, bp, seq, 256)
        go, st_p = _gla_call(rz, ow["gg"], bp, seq, 256)
        y1p, h2, ti, tw = _outproj_call(yp, fo, lo, go, mod_p[l], g2, ow["w_out"], ow["w_r"], ow["b_r"],
                                        256, seq, n_buf, 0)
        st_p = st_p.reshape(bp, GLA_HEADS, HEAD_DIM, GLA_HEADS, HEAD_DIM)
        st_p = jnp.stack([st_p[:, hh, :, hh, :] for hh in range(GLA_HEADS)], axis=1).transpose(0, 1, 3, 2)
        outs_p[0].append(conv_p)
        outs_p[1].append(hlast_p.reshape(bp, LRU_W))
        outs_p[2].append(st_p)

        kts, vts, lfts, rzs, qs, ks, vs, logfs, gts = _inproj_call(
            ys, mod_s[l], g1, inw, bs, bs, 0, 1, sample=True)
        fos = _fox_dec_call(l, page_table, qs, ks, vs, logfs, ckt, cvt, cft)
        los, conv_s, h_s = _lru_step_call(l, rzs, conv_t, state_lru, lw)
        gost, s_s = _gla_step_call(l, gts, s_view, ow["ggc"])
        y1s, h2, ti, tw = _outproj_call(ys, fos, los, gost.T, mod_s[l], g2, ow["w_out"], ow["w_r"], ow["b_r"],
                                        bs, bs, n_buf, n_p, prev=(h2, ti, tw))
        outs_s[0].append(kts[0, 0])
        outs_s[1].append(vts[0, 0])
        outs_s[2].append(lfts[0, 0])
        outs_s[3].append(conv_s)
        outs_s[4].append(h_s)
        outs_s[5].append(s_s)

        dest, cnt = _rank_call(ti, n_tot, trash, 512)
        counts = cnt[0, 0:N_EXPERTS].astype(I32)
        pad_end = jnp.cumsum((counts + bm - 1) // bm * bm)
        blk_e = jnp.minimum(jnp.sum((jnp.arange(n_blocks, dtype=I32)[:, None] * bm >= pad_end[None, :]).astype(I32),
                                    axis=1), N_EXPERTS - 1).astype(I32)
        n_used = (pad_end[-1] // bm).astype(I32).reshape(1)
        dest_km = dest[0:TOP_K]
        xs = _sc_scatter_rows(h2, dest_km, n_rows)
        ye = _ffn_call(l, blk_e, n_used, xs, w_gu, b_gu4, w_down, b_dn4)
        yk = _sc_gather_rows(ye, jnp.minimum(dest_km, trash - 1).reshape(-1)).reshape(TOP_K, n_buf, d)
        final = l == n_layers - 1
        yp = _combine_call(y1p, yk, tw, mod_p[l], gf, 256, seq, 0, final)
        ys = _combine_call(y1s, yk, tw, mod_s[l], gf, bs, bs, n_p, final)

    kt_p, vt_p, lft_p = kv_p
    fox_k_p = kt_p.reshape(n_layers, bp, FOX_HEADS, HEAD_DIM, seq).transpose(0, 1, 4, 2, 3)
    fox_v_p = vt_p.reshape(n_layers, bp, FOX_HEADS, HEAD_DIM, seq).transpose(0, 1, 4, 2, 3)
    fox_f_p = lft_p.transpose(0, 1, 3, 2)
    fox_k_s = jnp.stack(outs_s[0]).reshape(n_layers, FOX_HEADS, HEAD_DIM, bs).transpose(0, 3, 1, 2)[:, :, None]
    fox_v_s = jnp.stack(outs_s[1]).reshape(n_layers, FOX_HEADS, HEAD_DIM, bs).transpose(0, 3, 1, 2)[:, :, None]
    fox_f_s = jnp.stack(outs_s[2]).transpose(0, 2, 1)[:, :, None]
    return (yp.reshape(bp, seq, d), ys.reshape(bs, 1, d),
            fox_k_p, fox_v_p, fox_f_p,
            jnp.stack(outs_p[0]), jnp.stack(outs_p[1]), jnp.stack(outs_p[2]),
            fox_k_s, fox_v_s, fox_f_s,
            jnp.stack(outs_s[3]).transpose(0, 2, 1, 3), jnp.stack(outs_s[4]),
            jnp.stack(outs_s[5]).transpose(0, 4, 1, 2, 3))
```

```python
import functools

import jax
import jax.numpy as jnp
from jax import lax
from jax.experimental import pallas as pl
from jax.experimental.pallas import tpu as pltpu
from jax.experimental.pallas import tpu_sc as plsc

F32 = jnp.float32
BF16 = jnp.bfloat16
I32 = jnp.int32

D_MODEL = 1024
HEAD_DIM = 64
FOX_W = 512
FOX_HEADS = 8
LRU_W = 256
GLA_W = 256
GLA_HEADS = 4
GLA_RANK = 16
GLA_CHUNK = 64
GLA_TAU = 16.0
LRU_C = 8.0
CONV_W = 4
N_EXPERTS = 32
TOP_K = 4
D_FF = 1024
SWIGLU_LIMIT = 7.0
SWIGLU_ALPHA = 1.702
NORM_EPS = 1e-6
PAGE_SIZE = 128
QK_SCALE = HEAD_DIM ** -0.5

_O_FQ, _O_FK, _O_FV, _O_FF = 0, 512, 1024, 1536
_O_LX, _O_LG, _O_GQ, _O_GK, _O_GV, _O_GA, _O_GOG = 1544, 1800, 2056, 2312, 2568, 2824, 2840
REC_W = 1792
GLA_T_ROWS = 5 * GLA_W
SMALL_W = 128
SMALL_T = 32
FOX_BLK = 2 * HEAD_DIM
FOX_QX = FOX_HEADS * FOX_BLK

VMEM_LIMIT = 56 * 1024 * 1024
MOE_BLOCK = 256
SC_WINDOW = 128
SC_COLS = 256
SC_WORKERS = 32

_NT = (((1,), (1,)), ((), ()))
_TN = (((0,), (0,)), ((), ()))


def _cparams(sem, vmem=VMEM_LIMIT):
    return pltpu.CompilerParams(dimension_semantics=sem, vmem_limit_bytes=vmem)


def _log_sigmoid(x):
    return jnp.minimum(x, 0.0) - jnp.log1p(jnp.exp(-jnp.abs(x)))


def _softplus(x):
    return jnp.maximum(x, 0.0) + jnp.log1p(jnp.exp(-jnp.abs(x)))


def _cumsum(x, axis):
    n = x.shape[axis]
    idx = lax.broadcasted_iota(I32, x.shape, axis)
    s = 1
    while s < n:
        x = x + jnp.where(idx >= s, pltpu.roll(x, s, axis), 0.0)
        s *= 2
    return x


_HI16 = -65536


def _pack_bf16_pairs(lo, hi):
    lo_bits = lax.bitcast_convert_type(lo.astype(BF16).astype(F32), I32)
    hi_bits = lax.bitcast_convert_type(hi.astype(BF16).astype(F32), I32)
    return lax.shift_right_logical(lo_bits, jnp.full(lo_bits.shape, 16, I32)) | (hi_bits & _HI16)


def _unpack_bf16_pairs(packed):
    lo = lax.bitcast_convert_type(lax.shift_left(packed, jnp.full(packed.shape, 16, I32)), F32)
    hi = lax.bitcast_convert_type(packed & _HI16, F32)
    return lo.astype(BF16), hi.astype(BF16)


def _mod_spec(mod, tm, tiles_per_seq, last_tile=None, tile_off=0):
    d = mod.shape[-1]
    clamp = (lambda i: i + tile_off) if last_tile is None else (lambda i: jnp.minimum(i, last_tile))
    if mod.shape[2] != 1:
        return pl.BlockSpec((6, None, tm, d), lambda i: (0, 0, clamp(i), 0))
    return pl.BlockSpec((6, None, 1, d), lambda i: (0, clamp(i) // tiles_per_seq, 0, 0))


def _ada_kernel(c_ref, w_ref, b_ref, o_ref):
    c = c_ref[...]
    a = (c * jax.nn.sigmoid(c)).astype(BF16)
    o_ref[...] = jnp.dot(a, w_ref[...].astype(BF16), preferred_element_type=F32) + b_ref[...]


def _ada_call(c_all, w_ada, b_ada):
    n_layers, d, w = w_ada.shape
    r = c_all.shape[0]
    tn = 1536
    return pl.pallas_call(
        _ada_kernel,
        grid=(n_layers, w // tn),
        in_specs=[
            pl.BlockSpec((r, d), lambda l, j: (0, 0)),
            pl.BlockSpec((None, d, tn), lambda l, j: (l, 0, j)),
            pl.BlockSpec((None, 1, tn), lambda l, j: (l, 0, j)),
        ],
        out_specs=pl.BlockSpec((None, r, tn), lambda l, j: (l, 0, j)),
        out_shape=jax.ShapeDtypeStruct((n_layers, r, w), F32),
        compiler_params=_cparams(("arbitrary", "arbitrary")),
        name="ada_mod",
    )(c_all, w_ada, b_ada.reshape(n_layers, 1, w))


def _inproj_kernel(*refs, tiles_per_seq, sample, n_alias):
    (y_ref, mod_ref, g_ref, wq_ref, wkvt_ref, wrec_ref, wsm_ref, wsmt_ref,
     bsm_ref, bfc_ref, w2_ref, b2_ref) = refs[:12]
    if sample:
        wgt_ref, w2t_ref, b2c_ref = refs[12:15]
        pos = 15 + n_alias
    else:
        eq_ref, oneq_ref, ek_ref, onek_ref = refs[12:16]
        pos = 16 + n_alias
    kt_ref, vt_ref, lft_ref, rz_ref = refs[pos:pos + 4]
    pos += 4
    if sample:
        q_ref, k_ref, v_ref, logf_ref, gt_ref = refs[pos:pos + 5]
        pos += 5
    else:
        qx_ref, kx_ref, vx_ref = refs[pos:pos + 3]
        pos += 3
    carry_c, carry_r = refs[pos:pos + 2]
    i = pl.program_id(0)

    @pl.when(i % tiles_per_seq == 0)
    def _():
        carry_c[...] = jnp.zeros_like(carry_c)
        carry_r[...] = jnp.zeros_like(carry_r)

    x = y_ref[...]
    xn = x * lax.rsqrt(jnp.mean(x * x, axis=-1, keepdims=True) + NORM_EPS) * g_ref[...]
    h = (xn * (1.0 + mod_ref[1]) + mod_ref[0]).astype(BF16)

    kt = lax.dot_general(wkvt_ref[0:FOX_W, :], h, _NT, preferred_element_type=F32)
    vt = lax.dot_general(wkvt_ref[FOX_W:2 * FOX_W, :], h, _NT, preferred_element_type=F32)
    kt_ref[...] = kt
    vt_ref[...] = vt
    rz_ref[:, 0:REC_W - GLA_W] = jnp.dot(h, wrec_ref[...], preferred_element_type=F32)

    sm = jnp.dot(h, wsm_ref[...], preferred_element_type=F32)
    lane = lax.broadcasted_iota(I32, sm.shape, 1)
    logf = jnp.where(lane < FOX_HEADS, _log_sigmoid(sm + bsm_ref[...]), 0.0)
    cum = _cumsum(logf, 0) + carry_c[...]
    carry_c[...] = cum[cum.shape[0] - 1:, :]
    glin =jnp.dot(sm.astype(BF16), w2_ref[...], preferred_element_type=F32) + b2_ref[...]
    rz_ref[:, REC_W - GLA_W:REC_W] = _log_sigmoid(glin) * (1.0 / GLA_TAU)

    smt = lax.dot_general(wsmt_ref[...], h, _NT, preferred_element_type=F32)
    lft = _log_sigmoid(smt[0:FOX_HEADS, :] + bfc_ref[...])
    lft_ref[...] = lft
    cumt = _cumsum(lft, 1) + carry_r[...]
    carry_r[...] = cumt[:, cumt.shape[1] - 1:]

    q = jnp.dot(h, wq_ref[...], preferred_element_type=F32) * QK_SCALE
    if not sample:
        def split3(c):
            hi = c.astype(BF16)
            r1 = c - hi.astype(F32)
            mid = r1.astype(BF16)
            lo = (r1 - mid.astype(F32)).astype(BF16)
            return [hi, mid, lo]

        cq = jnp.concatenate(split3(cum), axis=1)
        qx_ref[...] = (q + jnp.dot(cq, eq_ref[...], preferred_element_type=F32) + oneq_ref[...]).astype(BF16)
        ck = jnp.concatenate(split3(cumt) + [jnp.zeros(cumt.shape, BF16)], axis=0)
        kb = jnp.dot(ek_ref[...], ck, preferred_element_type=F32) + onek_ref[...]
        parts = []
        for hh in range(FOX_HEADS):
            parts += [kt[hh * HEAD_DIM:(hh + 1) * HEAD_DIM, :], kb[hh * HEAD_DIM:(hh + 1) * HEAD_DIM, :]]
        kx_ref[...] = jnp.concatenate(parts, axis=0).astype(BF16)
        vx_ref[...] = vt.astype(BF16)

    if sample:
        q_ref[...] = q.astype(BF16)
        k_ref[...] = kt.T
        v_ref[...] = vt.T
        logf_ref[...] = logf[:, 0:FOX_HEADS]
        gt_ref[0:4 * GLA_W, :] = lax.dot_general(wgt_ref[...], h, _NT, preferred_element_type=F32)
        glt = jnp.dot(w2t_ref[...], smt.astype(BF16), preferred_element_type=F32) + b2c_ref[...]
        gt_ref[4 * GLA_W:GLA_T_ROWS, :] = _log_sigmoid(glt) * (1.0 / GLA_TAU)


def _inproj_call(y, mod, g, wts, seq_len, tm, layer, n_layers, kv_bufs=None, sample=False):
    n, d = y.shape
    tiles_per_seq = seq_len // tm
    n_seq = n // seq_len
    qw = FOX_W if sample else FOX_QX
    const = lambda i: (0, 0)
    row = lambda i: (i, 0)
    seq_t = lambda i: (layer, i // tiles_per_seq, 0, i % tiles_per_seq)
    seq_t3 = lambda i: (i // tiles_per_seq, 0, i % tiles_per_seq)
    in_specs = [
        pl.BlockSpec((tm, d), row),
        _mod_spec(mod, tm, tiles_per_seq),
        pl.BlockSpec((1, d), const),
        pl.BlockSpec((d, qw), const),
        pl.BlockSpec((2 * FOX_W, d), const),
        pl.BlockSpec((d, REC_W - GLA_W), const),
        pl.BlockSpec((d, SMALL_W), const),
        pl.BlockSpec((SMALL_T, d), const),
        pl.BlockSpec((1, SMALL_W), const),
        pl.BlockSpec((FOX_HEADS, 1), const),
        pl.BlockSpec((SMALL_W, GLA_W), const),
        pl.BlockSpec((1, GLA_W), const),
    ]
    args = [y, mod, g, wts["w_q" if sample else "w_qx"], wts["w_kvt"], wts["w_rec"], wts["w_sm"], wts["w_smt"],
            wts["b_sm"], wts["b_fc"], wts["w2"], wts["b2"]]
    if sample:
        in_specs += [pl.BlockSpec((4 * GLA_W, d), const), pl.BlockSpec((GLA_W, SMALL_T), const),
                     pl.BlockSpec((GLA_W, 1), const)]
        args += [wts["w_gt"], wts["w2t"], wts["b2c"]]
    else:
        in_specs += [pl.BlockSpec((3 * SMALL_W, FOX_QX), const), pl.BlockSpec((1, FOX_QX), const),
                     pl.BlockSpec((FOX_W, SMALL_T), const), pl.BlockSpec((FOX_W, 1), const)]
        args += list(_bias_fold_consts())
    aliases = {}
    n_alias = 0
    if kv_bufs is not None:
        n_alias = 3
        first = len(args)
        in_specs += [pl.BlockSpec(memory_space=pl.ANY)] * 3
        args += list(kv_bufs)
        aliases = {first: 0, first + 1: 1, first + 2: 2}
    out_specs = [
        pl.BlockSpec((None, None, FOX_W, tm), seq_t),
        pl.BlockSpec((None, None, FOX_W, tm), seq_t),
        pl.BlockSpec((None, None, FOX_HEADS, tm), seq_t),
        pl.BlockSpec((tm, REC_W), row),
    ]
    out_shape = [
        jax.ShapeDtypeStruct((n_layers, n_seq, FOX_W, seq_len), F32),
        jax.ShapeDtypeStruct((n_layers, n_seq, FOX_W, seq_len), F32),
        jax.ShapeDtypeStruct((n_layers, n_seq, FOX_HEADS, seq_len), F32),
        jax.ShapeDtypeStruct((n, REC_W), F32),
    ]
    if sample:
        out_specs += [pl.BlockSpec((tm, FOX_W), row), pl.BlockSpec((tm, FOX_W), row), pl.BlockSpec((tm, FOX_W), row),
                      pl.BlockSpec((tm, FOX_HEADS), row), pl.BlockSpec((GLA_T_ROWS, tm), lambda i: (0, i))]
        out_shape += [jax.ShapeDtypeStruct((n, FOX_W), BF16), jax.ShapeDtypeStruct((n, FOX_W), F32),
                      jax.ShapeDtypeStruct((n, FOX_W), F32), jax.ShapeDtypeStruct((n, FOX_HEADS), F32),
                      jax.ShapeDtypeStruct((GLA_T_ROWS, n), F32)]
    else:
        out_specs += [pl.BlockSpec((tm, FOX_QX), row), pl.BlockSpec((None, FOX_QX, tm), seq_t3),
                      pl.BlockSpec((None, FOX_W, tm), seq_t3)]
        out_shape += [jax.ShapeDtypeStruct((n, FOX_QX), BF16), jax.ShapeDtypeStruct((n_seq, FOX_QX, seq_len), BF16),
                      jax.ShapeDtypeStruct((n_seq, FOX_W, seq_len), BF16)]
    return pl.pallas_call(
        functools.partial(_inproj_kernel, tiles_per_seq=tiles_per_seq, sample=sample, n_alias=n_alias),
        grid=(n // tm,),
        in_specs=in_specs,
        out_specs=tuple(out_specs),
        out_shape=tuple(out_shape),
        input_output_aliases=aliases,
        scratch_shapes=[pltpu.VMEM((1, SMALL_W), F32), pltpu.VMEM((FOX_HEADS, 1), F32)],
        compiler_params=_cparams(("arbitrary",)),
        name="in_proj",
    )(*args)


def _bias_fold_consts():
    h = jnp.arange(FOX_HEADS)
    eq = jnp.zeros((3 * SMALL_W, FOX_QX), F32)
    ek = jnp.zeros((FOX_W, SMALL_T), F32)
    oneq = jnp.zeros((1, FOX_QX), F32)
    onek = jnp.zeros((FOX_W, 1), F32)
    for piece in range(3):
        eq = eq.at[piece * SMALL_W + h, h * FOX_BLK + HEAD_DIM + piece].set(1.0)
        oneq = oneq.at[0, h * FOX_BLK + HEAD_DIM + 3 + piece].set(1.0)
        onek = onek.at[h * HEAD_DIM + piece, 0].set(1.0)
        ek = ek.at[h * HEAD_DIM + 3 + piece, piece * FOX_HEADS + h].set(-1.0)
    return eq.astype(BF16), oneq, ek.astype(BF16), onek


def _fox_kernel(qx_ref, kx_ref, vx_ref, o_ref, m_sc, l_sc, acc_sc, *, tq, tk):
    qi = pl.program_id(2)
    ki = pl.program_id(3)
    w = 2 * HEAD_DIM
    nc = tk // w

    @pl.when(ki == 0)
    def _():
        m_sc[...] = jnp.full_like(m_sc, -jnp.inf)
        l_sc[...] = jnp.zeros_like(l_sc)
        acc_sc[...] = jnp.zeros_like(acc_sc)

    def step(masked):
        vt = vx_ref[...]
        if masked:
            causal = lax.broadcasted_iota(I32, (tq, tk), 1) <= lax.broadcasted_iota(I32, (tq, tk), 0)
        for j in range(2):
            s = jnp.dot(qx_ref[:, j * FOX_BLK:(j + 1) * FOX_BLK], kx_ref[j * FOX_BLK:(j + 1) * FOX_BLK, :],
                        preferred_element_type=F32)
            if masked:
                s = jnp.where(causal, s, -jnp.inf)
            sc = [s[:, c * w:(c + 1) * w] for c in range(nc)]
            mb = sc[0]
            for c in range(1, nc):
                mb = jnp.maximum(mb, sc[c])
            m_prev = m_sc[j]
            m_new = jnp.maximum(m_prev, jnp.broadcast_to(jnp.max(mb, axis=1, keepdims=True), (tq, w)))
            ps = [jnp.exp(sc[c] - m_new) for c in range(nc)]
            lsum = ps[0]
            for c in range(1, nc):
                lsum = lsum + ps[c]
            alpha = jnp.exp(m_prev - m_new)
            l_sc[j] = alpha * l_sc[j] + jnp.broadcast_to(jnp.sum(lsum, axis=1, keepdims=True), (tq, w))
            p = jnp.concatenate([pc.astype(BF16) for pc in ps], axis=1)
            acc_sc[j] = alpha * acc_sc[j] + lax.dot_general(p, vt, _NT, preferred_element_type=F32)
            m_sc[j] = m_new

    @pl.when(ki < qi)
    def _():
        step(False)

    @pl.when(ki == qi)
    def _():
        step(True)
        lane = lax.broadcasted_iota(I32, (tq, w), 1)
        o0 = acc_sc[0] / l_sc[0]
        o1 = acc_sc[1] / l_sc[1]
        o_ref[...] = jnp.where(lane < HEAD_DIM, o0, o1).astype(BF16)


def _fox_call(qx, kx, vx, n_seq, seq_len, tq):
    n = qx.shape[0]
    nt = seq_len // tq
    tk = tq
    hp = FOX_HEADS // 2
    w = 2 * HEAD_DIM
    past = lambda b, h, i, j: (b, h, jnp.minimum(i, j))
    return pl.pallas_call(
        functools.partial(_fox_kernel, tq=tq, tk=tk),
        grid=(n_seq, hp, nt, nt),
        in_specs=[
            pl.BlockSpec((tq, 2 * FOX_BLK), lambda b, h, i, j: (b * nt + i, h)),
            pl.BlockSpec((None, 2 * FOX_BLK, tk), past),
            pl.BlockSpec((None, w, tk), past),
        ],
        out_specs=pl.BlockSpec((tq, w), lambda b, h, i, j: (b * nt + i, h)),
        out_shape=jax.ShapeDtypeStruct((n, FOX_W), BF16),
        scratch_shapes=[pltpu.VMEM((2, tq, w), F32), pltpu.VMEM((2, tq, w), F32), pltpu.VMEM((2, tq, w), F32)],
        compiler_params=_cparams(("arbitrary", "arbitrary", "arbitrary", "arbitrary")),
        name="fox_prompt",
    )(qx, kx, vx)


def _lru_gates(xc, wa_ref, ba_ref, wx_ref, bx_ref, lam_ref):
    xb = xc.astype(BF16)
    r = jax.nn.sigmoid(jnp.dot(xb, wa_ref[...], preferred_element_type=F32) + ba_ref[...])
    gi = jax.nn.sigmoid(jnp.dot(xb, wx_ref[...], preferred_element_type=F32) + bx_ref[...])
    log_a = -LRU_C * r * _softplus(-lam_ref[...])
    a = jnp.exp(log_a)
    mult = jnp.sqrt(-jnp.tanh(log_a) * (a * a + 1.0))
    return a, mult, gi


def _lru_kernel(lx_ref, lg_ref, cw_ref, cb_ref, wa_ref, ba_ref, wx_ref, bx_ref, lam_ref,
                lo_ref, conv_ref, hlast_ref, xbuf, hcar, *, tt):
    ti = pl.program_id(1)
    nt = pl.num_programs(1)

    @pl.when(ti == 0)
    def _():
        xbuf[0:8, :] = jnp.zeros((8, LRU_W), F32)
        hcar[...] = jnp.zeros_like(hcar)

    x = lx_ref[...]
    xbuf[8:8 + tt, :] = x
    xc = cb_ref[...] + cw_ref[3:4, :] * x
    for j in range(CONV_W - 1):
        xc = xc + cw_ref[j:j + 1, :] * xbuf[5 + j:5 + j + tt, :]
    xbuf[0:8, :] = x[tt - 8:tt, :]

    a, mult, gi = _lru_gates(xc, wa_ref, ba_ref, wx_ref, bx_ref, lam_ref)
    row = lax.broadcasted_iota(I32, (tt, LRU_W), 0)
    mult = jnp.where((row == 0) & (ti == 0), 1.0, mult)
    b = mult * gi * xc
    s = 1
    while s < tt:
        keep = row >= s
        a_sh = jnp.where(keep, pltpu.roll(a, s, 0), 1.0)
        b_sh = jnp.where(keep, pltpu.roll(b, s, 0), 0.0)
        b = a * b_sh + b
        a = a * a_sh
        s *= 2
    h = a * hcar[...] + b
    hcar[...] = h[tt - 1:tt, :]
    lo_ref[...] = (h * jax.nn.gelu(lg_ref[...])).astype(BF16)

    @pl.when(ti == nt - 1)
    def _():
        conv_ref[...] = x[tt - (CONV_W - 1):tt, :]
        hlast_ref[...] = h[tt - 1:tt, :]


def _lru_weight_specs(const):
    return [
        pl.BlockSpec((CONV_W, LRU_W), const), pl.BlockSpec((1, LRU_W), const),
        pl.BlockSpec((LRU_W, LRU_W), const), pl.BlockSpec((1, LRU_W), const),
        pl.BlockSpec((LRU_W, LRU_W), const), pl.BlockSpec((1, LRU_W), const),
        pl.BlockSpec((1, LRU_W), const),
    ]


def _lru_weight_args(lw):
    return [lw["conv_w"], lw["conv_b"], lw["wa"], lw["ba"], lw["wx"], lw["bx"], lw["lam"]]


def _lru_call(rz, lw, n_seq, seq_len, tt):
    n = rz.shape[0]
    nt = seq_len // tt
    return pl.pallas_call(
        functools.partial(_lru_kernel, tt=tt),
        grid=(n_seq, nt),
        in_specs=[
            pl.BlockSpec((tt, LRU_W), lambda b, t: (b * nt + t, 0)),
            pl.BlockSpec((tt, LRU_W), lambda b, t: (b * nt + t, 1)),
        ] + _lru_weight_specs(lambda b, t: (0, 0)),
        out_specs=(
            pl.BlockSpec((tt, LRU_W), lambda b, t: (b * nt + t, 0)),
            pl.BlockSpec((None, CONV_W - 1, LRU_W), lambda b, t: (b, 0, 0)),
            pl.BlockSpec((None, 1, LRU_W), lambda b, t: (b, 0, 0)),
        ),
        out_shape=(
            jax.ShapeDtypeStruct((n, LRU_W), BF16),
            jax.ShapeDtypeStruct((n_seq, CONV_W - 1, LRU_W), F32),
            jax.ShapeDtypeStruct((n_seq, 1, LRU_W), F32),
        ),
        scratch_shapes=[pltpu.VMEM((tt + 8, LRU_W), F32), pltpu.VMEM((1, LRU_W), F32)],
        compiler_params=_cparams(("arbitrary", "arbitrary")),
        name="lru_prompt",
    )(rz, rz, *_lru_weight_args(lw))


def _head_rms_gate(o, gg_ref, gog):
    lane = lax.broadcasted_iota(I32, o.shape, 1)
    o2 = o * o
    rs = jnp.zeros_like(o)
    for hh in range(GLA_HEADS):
        mh = lane // HEAD_DIM == hh
        ms = jnp.sum(jnp.where(mh, o2, 0.0), axis=1, keepdims=True) * (1.0 / HEAD_DIM)
        rs = jnp.where(mh, lax.rsqrt(ms + NORM_EPS), rs)
    return o * rs * gg_ref[...] * (gog * jax.nn.sigmoid(gog))


def _gla_kernel(gq_ref, gk_ref, gv_ref, gog_ref, gl_ref, gg_ref, go_ref, st_ref, s_sc, *, tt):
    ti = pl.program_id(1)
    nt = pl.num_programs(1)
    c = GLA_CHUNK

    @pl.when(ti == 0)
    def _():
        s_sc[...] = jnp.zeros_like(s_sc)

    lane = lax.broadcasted_iota(I32, (c, GLA_W), 1)
    r2 = lax.broadcasted_iota(I32, (GLA_W, GLA_W), 0)
    c2 = lax.broadcasted_iota(I32, (GLA_W, GLA_W), 1)
    same_head = (r2 // HEAD_DIM) == (c2 // HEAD_DIM)
    tril = lax.broadcasted_iota(I32, (c, c), 1) <= lax.broadcasted_iota(I32, (c, c), 0)

    for ci in range(tt // c):
        sl = slice(ci * c, (ci + 1) * c)
        q = gq_ref[sl, :] * QK_SCALE
        k = gk_ref[sl, :]
        v = gv_ref[sl, :].astype(BF16)
        bc = _cumsum(gl_ref[sl, :], 0)
        b_last = bc[c - 1:c, :]
        qd = (q * jnp.exp(bc)).astype(BF16)
        kinv = (k * jnp.exp(-bc)).astype(BF16)
        kdec = (k * jnp.exp(b_last - bc)).astype(BF16)
        s_prev = s_sc[...]
        o = lax.dot_general(qd, s_prev.astype(BF16), _NT, preferred_element_type=F32)
        for hh in range(GLA_HEADS):
            mh = lane // HEAD_DIM == hh
            att = lax.dot_general(jnp.where(mh, qd, jnp.zeros_like(qd)), kinv, _NT, preferred_element_type=F32)
            att = jnp.where(tril, att, 0.0).astype(BF16)
            o = o + jnp.dot(att, jnp.where(mh, v, jnp.zeros_like(v)), preferred_element_type=F32)
        ut = lax.dot_general(v, kdec, _TN, preferred_element_type=F32)
        s_sc[...] = s_prev * jnp.exp(b_last) + jnp.where(same_head, ut, 0.0)
        go_ref[sl, :] = _head_rms_gate(o, gg_ref, gog_ref[sl, :]).astype(BF16)

    @pl.when(ti == nt - 1)
    def _():
        st_ref[...] = s_sc[...]


def _gla_call(rz, gg, n_seq, seq_len, tt):
    n = rz.shape[0]
    nt = seq_len // tt

    def col(j):
        return pl.BlockSpec((tt, GLA_W), lambda b, t: (b * nt + t, j))

    return pl.pallas_call(
        functools.partial(_gla_kernel, tt=tt),
        grid=(n_seq, nt),
        in_specs=[col(2), col(3), col(4), col(5), col(6), pl.BlockSpec((1, GLA_W), lambda b, t: (0, 0))],
        out_specs=(
            pl.BlockSpec((tt, GLA_W), lambda b, t: (b * nt + t, 0)),
            pl.BlockSpec((None, GLA_W, GLA_W), lambda b, t: (b, 0, 0)),
        ),
        out_shape=(
            jax.ShapeDtypeStruct((n, GLA_W), BF16),
            jax.ShapeDtypeStruct((n_seq, GLA_W, GLA_W), F32),
        ),
        scratch_shapes=[pltpu.VMEM((GLA_W, GLA_W), F32)],
        compiler_params=_cparams(("arbitrary", "arbitrary")),
        name="gla_prompt",
    )(rz, rz, rz, rz, rz, gg)


def _fox_dec_kernel(pt_ref, q_ref, kn_ref, vn_ref, dn_ref, *refs, n_pages):
    del pt_ref
    k_refs = refs[0:n_pages]
    v_refs = refs[n_pages:2 * n_pages]
    f_refs = refs[2 * n_pages:3 * n_pages]
    o_ref = refs[3 * n_pages]
    w = FOX_W
    hrow = lax.broadcasted_iota(I32, (FOX_HEADS, w), 0)
    hlane = lax.broadcasted_iota(I32, (FOX_HEADS, w), 1) // HEAD_DIM
    diag = hrow == hlane
    q = q_ref[...].astype(F32)
    qbd = jnp.where(diag, jnp.broadcast_to(q, (FOX_HEADS, w)), 0.0).astype(BF16)
    s = jnp.concatenate(
        [jnp.dot(qbd, k_refs[p][...].astype(BF16), preferred_element_type=F32) for p in range(n_pages)], axis=1)
    lf = jnp.concatenate([f_refs[p][...] for p in range(n_pages)], axis=1)
    cs = _cumsum(lf, 1)
    suffix = cs[:, cs.shape[1] - 1:] - cs
    s = s + dn_ref[...] + suffix
    s_new = jnp.sum(qbd.astype(F32) * kn_ref[...], axis=1, keepdims=True)
    m = jnp.maximum(jnp.max(s, axis=1, keepdims=True), s_new)
    p_past = jnp.exp(s - m)
    p_new = jnp.exp(s_new - m)
    denom = jnp.sum(p_past, axis=1, keepdims=True) + p_new
    acc = p_new * vn_ref[...]
    pb = p_past.astype(BF16)
    for p in range(n_pages):
        acc = acc + lax.dot_general(pb[:, p * PAGE_SIZE:(p + 1) * PAGE_SIZE], v_refs[p][...].astype(BF16), _NT,
                                    preferred_element_type=F32)
    out = jnp.where(diag, acc / denom, 0.0)
    o_ref[...] = jnp.sum(out, axis=0, keepdims=True).astype(BF16)


def _fox_dec_call(layer, page_table, q, k_new, v_new, logf_new, cache_kt, cache_vt, cache_ft):
    bd, n_pages = page_table.shape
    w = FOX_W

    def page_spec(rows, j):
        return pl.BlockSpec((None, None, rows, PAGE_SIZE), lambda b, pt, j=j: (layer, pt[b, j], 0, 0))

    row = lambda b, pt: (b, 0, 0)
    in_specs = [
        pl.BlockSpec((None, 1, w), row),
        pl.BlockSpec((None, 1, w), row),
        pl.BlockSpec((None, 1, w), row),
        pl.BlockSpec((None, FOX_HEADS, 1), row),
    ]
    in_specs += [page_spec(w, j) for j in range(n_pages)]
    in_specs += [page_spec(w, j) for j in range(n_pages)]
    in_specs += [page_spec(FOX_HEADS, j) for j in range(n_pages)]
    grid_spec = pltpu.PrefetchScalarGridSpec(
        num_scalar_prefetch=1,
        grid=(bd,),
        in_specs=in_specs,
        out_specs=pl.BlockSpec((None, 1, w), row),
    )
    out = pl.pallas_call(
        functools.partial(_fox_dec_kernel, n_pages=n_pages),
        grid_spec=grid_spec,
        out_shape=jax.ShapeDtypeStruct((bd, 1, w), BF16),
        compiler_params=_cparams(("arbitrary",)),
        name="fox_sample",
    )(page_table, q.reshape(bd, 1, w), k_new.reshape(bd, 1, w), v_new.reshape(bd, 1, w),
      logf_new.reshape(bd, FOX_HEADS, 1),
      *([cache_kt] * n_pages), *([cache_vt] * n_pages), *([cache_ft] * n_pages))
    return out.reshape(bd, w)


def _lru_step_kernel(lx_ref, lg_ref, conv_ref, h0_ref, cw_ref, cb_ref, wa_ref, ba_ref, wx_ref, bx_ref, lam_ref,
                     lo_ref, convn_ref, hn_ref):
    x = lx_ref[...]
    xc = cb_ref[...] + cw_ref[3:4, :] * x
    for j in range(CONV_W - 1):
        xc = xc + cw_ref[j:j + 1, :] * conv_ref[j]
    convn_ref[0] = conv_ref[1]
    convn_ref[1] = conv_ref[2]
    convn_ref[2] = x
    a, mult, gi = _lru_gates(xc, wa_ref, ba_ref, wx_ref, bx_ref, lam_ref)
    h = a * h0_ref[...] + mult * gi * xc
    hn_ref[...] = h
    lo_ref[...] = (h * jax.nn.gelu(lg_ref[...])).astype(BF16)


def _lru_step_call(layer, rz, conv_t, h0, lw):
    bd = rz.shape[0]
    return pl.pallas_call(
        _lru_step_kernel,
        grid=(1,),
        in_specs=[
            pl.BlockSpec((bd, LRU_W), lambda i: (0, 0)),
            pl.BlockSpec((bd, LRU_W), lambda i: (0, 1)),
            pl.BlockSpec((None, CONV_W - 1, bd, LRU_W), lambda i: (layer, 0, 0, 0)),
            pl.BlockSpec((None, bd, LRU_W), lambda i: (layer, 0, 0)),
        ] + _lru_weight_specs(lambda i: (0, 0)),
        out_specs=(
            pl.BlockSpec((bd, LRU_W), lambda i: (0, 0)),
            pl.BlockSpec((CONV_W - 1, bd, LRU_W), lambda i: (0, 0, 0)),
            pl.BlockSpec((bd, LRU_W), lambda i: (0, 0)),
        ),
        out_shape=(
            jax.ShapeDtypeStruct((bd, LRU_W), BF16),
            jax.ShapeDtypeStruct((CONV_W - 1, bd, LRU_W), F32),
            jax.ShapeDtypeStruct((bd, LRU_W), F32),
        ),
        compiler_params=_cparams(("arbitrary",)),
        name="lru_step",
    )(rz, rz, conv_t, h0, *_lru_weight_args(lw))


def _gla_step_kernel(q_ref, k_ref, v_ref, gog_ref, gl_ref, gg_ref, s_ref, go_ref, sn_ref):
    eg = jnp.exp(gl_ref[...])
    kt = k_ref[...]
    qt = q_ref[...] * QK_SCALE
    vt = v_ref[...]
    o = jnp.zeros_like(vt)
    for kk in range(HEAD_DIM):
        s_new = eg[kk:kk + 1, :] * s_ref[kk] + kt[kk:kk + 1, :] * vt
        sn_ref[kk] = s_new
        o = o + qt[kk:kk + 1, :] * s_new
    ms = jnp.mean(o * o, axis=0, keepdims=True)
    gog = gog_ref[...]
    go_ref[...] = o * lax.rsqrt(ms + NORM_EPS) * gg_ref[...] * (gog * jax.nn.sigmoid(gog))


def _gla_step_call(layer, gt, s_view, ggc):
    bd = gt.shape[1]
    hd = HEAD_DIM

    def part(j):
        return pl.BlockSpec((hd, bd), lambda h, j=j: (j * GLA_HEADS + h, 0))

    return pl.pallas_call(
        _gla_step_kernel,
        grid=(GLA_HEADS,),
        in_specs=[part(0), part(1), part(2), part(3), part(4),
                  pl.BlockSpec((hd, 1), lambda h: (h, 0)),
                  pl.BlockSpec((None, None, hd, hd, bd), lambda h: (layer, h, 0, 0, 0))],
        out_specs=(
            pl.BlockSpec((hd, bd), lambda h: (h, 0)),
            pl.BlockSpec((None, hd, hd, bd), lambda h: (h, 0, 0, 0)),
        ),
        out_shape=(
            jax.ShapeDtypeStruct((GLA_W, bd), F32),
            jax.ShapeDtypeStruct((GLA_HEADS, hd, hd, bd), F32),
        ),
        compiler_params=_cparams(("arbitrary",)),
        name="gla_step",
    )(gt, gt, gt, gt, gt, ggc, s_view)


def _outproj_kernel(*refs, n_alias, n_tiles):
    h2_ref, ti_ref, tw_ref = refs[10 + n_alias:13 + n_alias]
    i = pl.program_id(0)

    @pl.when(i < n_tiles)
    def _():
        _outproj_tile(*refs[0:9], *refs[9 + n_alias:13 + n_alias])

    @pl.when(i >= n_tiles)
    def _():
        h2_ref[...] = jnp.zeros_like(h2_ref)
        ti_ref[...] = jnp.zeros_like(ti_ref)
        tw_ref[...] = jnp.zeros_like(tw_ref)


def _outproj_tile(y_ref, fo_ref, lo_ref, go_ref, mod_ref, g2_ref, wo_ref, wr_ref, br_ref,
                  y1_ref, h2_ref, ti_ref, tw_ref):
    m = jnp.dot(fo_ref[...], wo_ref[0:FOX_W, :], preferred_element_type=F32)
    m = m + jnp.dot(lo_ref[...], wo_ref[FOX_W:FOX_W + LRU_W, :], preferred_element_type=F32)
    m = m + jnp.dot(go_ref[...].astype(BF16), wo_ref[FOX_W + LRU_W:, :], preferred_element_type=F32)
    y1 = y_ref[...] + mod_ref[2] * m
    y1_ref[...] = y1
    xn = y1 * lax.rsqrt(jnp.mean(y1 * y1, axis=-1, keepdims=True) + NORM_EPS) * g2_ref[...]
    h2 = xn * (1.0 + mod_ref[4]) + mod_ref[3]
    half = h2.shape[1] // 2
    h2_ref[...] = _pack_bf16_pairs(h2[:, 0:half], h2[:, half:])
    logits = jnp.dot(h2.astype(BF16), wr_ref[...], preferred_element_type=F32) + br_ref[...]
    lane = lax.broadcasted_iota(I32, logits.shape, 1)
    logits = jnp.where(lane < N_EXPERTS, logits, -jnp.inf)
    idx_out = jnp.zeros(logits.shape, I32)
    val_out = jnp.zeros(logits.shape, F32)
    vals = []
    for kk in range(TOP_K):
        mx = jnp.max(logits, axis=1, keepdims=True)
        sel = jnp.min(jnp.where(logits == mx, lane, SMALL_W), axis=1, keepdims=True)
        idx_out = jnp.where(lane == kk, sel, idx_out)
        vals.append(mx)
        logits = jnp.where(lane == sel, -jnp.inf, logits)
    es = [jnp.exp(vv - vals[0]) for vv in vals]
    tot = es[0] + es[1] + es[2] + es[3]
    for kk in range(TOP_K):
        val_out = jnp.where(lane == kk, es[kk] / tot, val_out)
    ti_ref[...] = idx_out
    tw_ref[...] = val_out


def _outproj_call(y, fo, lo, go, mod, g2, w_out, w_r, b_r, tm, seq_len, n_buf, row_off, prev=None):
    n, d = y.shape
    tiles_per_seq = max(seq_len // tm, 1)
    n_tiles = n // tm
    n_steps = n_tiles if prev is not None else n_buf // tm
    const = lambda i: (0, 0)
    row = lambda i: (jnp.minimum(i, n_tiles - 1), 0)
    off = row_off // tm
    orow = lambda i: (i + off, 0)
    in_specs = [
        pl.BlockSpec((tm, d), row),
        pl.BlockSpec((tm, FOX_W), row),
        pl.BlockSpec((tm, LRU_W), row),
        pl.BlockSpec((tm, GLA_W), row),
        _mod_spec(mod, tm, tiles_per_seq, n_tiles - 1),
        pl.BlockSpec((1, d), const),
        pl.BlockSpec((d, d), const),
        pl.BlockSpec((d, SMALL_W), const),
        pl.BlockSpec((1, SMALL_W), const),
    ]
    args = [y, fo, lo, go, mod, g2, w_out, w_r, b_r]
    aliases = {}
    if prev is not None:
        in_specs += [pl.BlockSpec(memory_space=pl.ANY)] * 3
        args += list(prev)
        aliases = {9: 1, 10: 2, 11: 3}
    return pl.pallas_call(
        functools.partial(_outproj_kernel, n_alias=0 if prev is None else 3, n_tiles=n_tiles),
        grid=(n_steps,),
        in_specs=in_specs,
        out_specs=(
            pl.BlockSpec((tm, d), row),
            pl.BlockSpec((tm, d // 2), orow),
            pl.BlockSpec((tm, SMALL_W), orow),
            pl.BlockSpec((tm, SMALL_W), orow),
        ),
        out_shape=(
            jax.ShapeDtypeStruct((n, d), F32),
            jax.ShapeDtypeStruct((n_buf, d // 2), I32),
            jax.ShapeDtypeStruct((n_buf, SMALL_W), I32),
            jax.ShapeDtypeStruct((n_buf, SMALL_W), F32),
        ),
        input_output_aliases=aliases,
        compiler_params=_cparams(("arbitrary",)),
        name="out_proj_router",
    )(*args)


def _rank_kernel(ti_ref, dest_ref, cnt_ref, carry, *, tm, n_valid, trash):
    p = pl.program_id(0)
    i = pl.program_id(1)
    nt = pl.num_programs(1)

    @pl.when((p == 0) & (i == 0))
    def _():
        carry[...] = jnp.zeros_like(carry)

    @pl.when((p == 1) & (i == 0))
    def _():
        cnt = carry[...]
        cnt_ref[...] = cnt
        padded = jnp.floor((cnt + (MOE_BLOCK - 1.0)) * (1.0 / MOE_BLOCK)) * MOE_BLOCK
        carry[...] = _cumsum(padded, 1) - padded

    t = ti_ref[...]
    lane = lax.broadcasted_iota(I32, (tm, SMALL_W), 1)
    valid = (lax.broadcasted_iota(I32, (tm, 1), 0) + i * tm) < n_valid
    ohs = [jnp.where(valid, (lane == t[:, kk:kk + 1]).astype(F32), 0.0) for kk in range(TOP_K)]

    @pl.when(p == 0)
    def _():
        carry[...] = carry[...] + jnp.sum(sum(ohs[1:], ohs[0]), axis=0, keepdims=True)

    @pl.when(p == 1)
    def _():
        before = (lax.broadcasted_iota(I32, (tm, tm), 1) < lax.broadcasted_iota(I32, (tm, tm), 0)).astype(BF16)
        base = carry[...]
        out = jnp.zeros((tm, SMALL_W), F32)
        for kk in range(TOP_K):
            oh = ohs[kk]
            pre = jnp.dot(before, oh.astype(BF16), preferred_element_type=F32)
            slot = jnp.sum(oh * (pre + base), axis=1, keepdims=True)
            out = jnp.where(lane == kk, jnp.where(valid, slot, float(trash)), out)
            base = base + jnp.sum(oh, axis=0, keepdims=True)
        carry[...] = base
        dest_ref[...] = out.T[0:8, :].astype(I32)


def _rank_call(ti, row0, n_rows, n_valid, trash, tm):
    off = row0 // tm
    return pl.pallas_call(
        functools.partial(_rank_kernel, tm=tm, n_valid=n_valid, trash=trash),
        grid=(2, n_rows // tm),
        in_specs=[pl.BlockSpec((tm, SMALL_W), lambda p, i: (i + off, 0))],
        out_specs=(
            pl.BlockSpec((8, tm), lambda p, i: (0, i * p)),
            pl.BlockSpec((1, SMALL_W), lambda p, i: (0, 0)),
        ),
        out_shape=(
            jax.ShapeDtypeStruct((8, n_rows), I32),
            jax.ShapeDtypeStruct((1, SMALL_W), F32),
        ),
        scratch_shapes=[pltpu.VMEM((1, SMALL_W), F32)],
        compiler_params=_cparams(("arbitrary", "arbitrary")),
        name="moe_rank",
    )(ti)


def _sc_mesh():
    return plsc.VectorSubcoreMesh(core_axis_name="core", subcore_axis_name="subcore")


def _sc_scatter_rows(x, dest_km, n_out, row0=0):
    kk, n = dest_km.shape
    d = x.shape[1]
    nb = n // SC_WINDOW
    off = row0 // SC_WINDOW
    assert (kk * nb) % SC_WORKERS == 0 and d % SC_COLS == 0 and row0 % SC_WINDOW == 0

    @pl.kernel(out_type=jax.ShapeDtypeStruct((n_out, d), x.dtype), mesh=_sc_mesh())
    def scatter_kernel(x_hbm, i_hbm, o_hbm):
        def body(x_vmem, i_vmem):
            j = pl.program_id(1)
            pltpu.sync_copy(x_vmem, o_hbm.at[i_vmem.at[0], pl.ds(j * SC_COLS, SC_COLS)])

        pltpu.emit_pipeline(
            body,
            grid=(kk * nb, d // SC_COLS),
            in_specs=[pl.BlockSpec((SC_WINDOW, SC_COLS), lambda g, j: (g % nb + off, j)),
                      pl.BlockSpec((1, SC_WINDOW), lambda g, j: (g // nb, g % nb))],
            out_specs=[],
            core_axis_name=("core", "subcore"),
            dimension_semantics=(pltpu.PARALLEL, pltpu.ARBITRARY),
        )(x_hbm, i_hbm)

    return scatter_kernel(x, dest_km)


def _sc_gather_rows(x, idx):
    n = idx.shape[0]
    d = x.shape[1]
    assert (n // SC_WINDOW) % SC_WORKERS == 0 and d % SC_COLS == 0

    @pl.kernel(out_type=jax.ShapeDtypeStruct((n, d), x.dtype), mesh=_sc_mesh())
    def gather_kernel(x_hbm, i_hbm, o_hbm):
        def body(i_vmem, o_vmem):
            j = pl.program_id(1)
            pltpu.sync_copy(x_hbm.at[i_vmem.at[0], pl.ds(j * SC_COLS, SC_COLS)], o_vmem)

        pltpu.emit_pipeline(
            body,
            grid=(n // SC_WINDOW, d // SC_COLS),
            in_specs=[pl.BlockSpec((1, SC_WINDOW), lambda i, j: (0, i))],
            out_specs=[pl.BlockSpec((SC_WINDOW, SC_COLS), lambda i, j: (i, j))],
            core_axis_name=("core", "subcore"),
            dimension_semantics=(pltpu.PARALLEL, pltpu.ARBITRARY),
        )(i_hbm, o_hbm)

    return gather_kernel(x, idx.reshape(1, n))


def _ffn_kernel(be_ref, nu_ref, x_ref, wgu_ref, bgu_ref, wdn_ref, bdn_ref, y_ref, wgu_b, wdn_b):
    j = pl.program_id(0)
    e = be_ref[j]
    prev = be_ref[jnp.maximum(j - 1, 0)]

    @pl.when((j == 0) | (e != prev))
    def _():
        wgu_b[...] = wgu_ref[...].astype(BF16)
        wdn_b[...] = wdn_ref[...].astype(BF16)

    @pl.when(j < nu_ref[0])
    def _():
        xp = x_ref[...]
        half = xp.shape[1]
        x_lo, x_hi = _unpack_bf16_pairs(xp)
        gu = (jnp.dot(x_lo, wgu_b[0:half, :], preferred_element_type=F32)
              + jnp.dot(x_hi, wgu_b[half:, :], preferred_element_type=F32) + bgu_ref[...])
        g = jnp.minimum(gu[:, 0:D_FF], SWIGLU_LIMIT)
        u = jnp.clip(gu[:, D_FF:], -SWIGLU_LIMIT, SWIGLU_LIMIT)
        act = g * jax.nn.sigmoid(SWIGLU_ALPHA * g)
        hmid = ((u + 1.0) * act).astype(BF16)
        y_ref[...] = jnp.dot(hmid, wdn_b[...], preferred_element_type=F32) + bdn_ref[...]

    @pl.when(j >= nu_ref[0])
    def _():
        y_ref[...] = jnp.zeros_like(y_ref)


def _ffn_call(layer, blk_e, n_used, xs, w_gu, b_gu, w_dn, b_dn):
    n_blocks = blk_e.shape[0]
    d = 2 * xs.shape[1]
    bm = MOE_BLOCK
    grid_spec = pltpu.PrefetchScalarGridSpec(
        num_scalar_prefetch=2,
        grid=(n_blocks,),
        in_specs=[
            pl.BlockSpec((bm, d // 2), lambda j, be, nu: (j, 0)),
            pl.BlockSpec((None, None, d, 2 * D_FF), lambda j, be, nu: (layer, be[j], 0, 0)),
            pl.BlockSpec((None, None, 1, 2 * D_FF), lambda j, be, nu: (layer, be[j], 0, 0)),
            pl.BlockSpec((None, None, D_FF, d), lambda j, be, nu: (layer, be[j], 0, 0)),
            pl.BlockSpec((None, None, 1, d), lambda j, be, nu: (layer, be[j], 0, 0)),
        ],
        out_specs=pl.BlockSpec((bm, d), lambda j, be, nu: (j, 0)),
        scratch_shapes=[pltpu.VMEM((d, 2 * D_FF), BF16), pltpu.VMEM((D_FF, d), BF16)],
    )
    return pl.pallas_call(
        _ffn_kernel,
        grid_spec=grid_spec,
        out_shape=jax.ShapeDtypeStruct((n_blocks * bm, d), F32),
        compiler_params=_cparams(("arbitrary",)),
        name="expert_ffn",
    )(blk_e, n_used, xs, w_gu, b_gu, w_dn, b_dn)


def _combine_kernel(y1_ref, yk_ref, tw_ref, mod_ref, gf_ref, o_ref, *, final):
    tw = tw_ref[...]
    acc = tw[:, 0:1] * yk_ref[0]
    for kk in range(1, TOP_K):
        acc = acc + tw[:, kk:kk + 1] * yk_ref[kk]
    y2 = y1_ref[...] + mod_ref[5] * acc
    if final:
        y2 = y2 * lax.rsqrt(jnp.mean(y2 * y2, axis=-1, keepdims=True) + NORM_EPS) * gf_ref[...]
    o_ref[...] = y2


def _combine_call(y1, yk, tw, mod, g_final, tm, seq_len, y_row0, n_rows, yk_row0, tw_row0, final):
    n, d = y1.shape
    tiles_per_seq = max(seq_len // tm, 1)
    y_off, yk_off, tw_off = y_row0 // tm, yk_row0 // tm, tw_row0 // tm
    return pl.pallas_call(
        functools.partial(_combine_kernel, final=final),
        grid=(n_rows // tm,),
        in_specs=[
            pl.BlockSpec((tm, d), lambda i: (i + y_off, 0)),
            pl.BlockSpec((TOP_K, tm, d), lambda i: (0, i + yk_off, 0)),
            pl.BlockSpec((tm, SMALL_W), lambda i: (i + tw_off, 0)),
            _mod_spec(mod, tm, tiles_per_seq, tile_off=y_off),
            pl.BlockSpec((1, d), lambda i: (0, 0)),
        ],
        out_specs=pl.BlockSpec((tm, d), lambda i: (i + y_off, 0)),
        out_shape=jax.ShapeDtypeStruct((n, d), F32),
        input_output_aliases={0: 0},
        compiler_params=_cparams(("arbitrary",)),
        name="moe_combine",
    )(y1, yk, tw, mod, g_final)


def _block_diag(w):
    nb, bw, _ = w.shape
    eye = jnp.eye(nb, dtype=w.dtype)
    return (eye[:, None, :, None] * w[:, :, None, :]).reshape(nb * bw, nb * bw)


def _layer_weights(l, w_in, b_forget, conv_w, conv_b, lru_lambda, lru_wa, lru_ba, lru_wx, lru_bx,
                   gla_w2, gla_b2, gla_gnorm, w_out, w_router, b_router):
    wi = w_in[l]
    d = wi.shape[0]
    w_ff = wi[:, _O_FF:_O_LX]
    w_ga = wi[:, _O_GA:_O_GOG]
    w_rec = jnp.concatenate([wi[:, _O_LX:_O_GA], wi[:, _O_GOG:]], axis=1)
    w2 = jnp.zeros((SMALL_W, GLA_W), F32).at[FOX_HEADS:FOX_HEADS + GLA_RANK].set(gla_w2[l])
    inw = {
        "w_q": wi[:, _O_FQ:_O_FK].astype(BF16),
        "w_qx": jnp.pad(wi[:, _O_FQ:_O_FK].reshape(d, FOX_HEADS, HEAD_DIM),
                        ((0, 0), (0, 0), (0, FOX_BLK - HEAD_DIM))).reshape(d, FOX_QX).astype(BF16),
        "w_kvt": wi[:, _O_FK:_O_FF].T.astype(BF16),
        "w_rec": w_rec.astype(BF16),
        "w_sm": jnp.concatenate([w_ff, w_ga, jnp.zeros((d, SMALL_W - FOX_HEADS - GLA_RANK), F32)], axis=1).astype(BF16),
        "w_smt": jnp.concatenate([w_ff, w_ga, jnp.zeros((d, SMALL_T - FOX_HEADS - GLA_RANK), F32)], axis=1).T.astype(BF16),
        "b_sm": jnp.zeros((1, SMALL_W), F32).at[0, 0:FOX_HEADS].set(b_forget[l]),
        "b_fc": b_forget[l].reshape(FOX_HEADS, 1),
        "w2": w2.astype(BF16),
        "b2": gla_b2[l].reshape(1, GLA_W),
        "w_gt": w_rec[:, 2 * GLA_W:].T.astype(BF16),
        "w2t": w2[0:SMALL_T].T.astype(BF16),
        "b2c": gla_b2[l].reshape(GLA_W, 1),
    }
    lw = {
        "conv_w": conv_w[l], "conv_b": conv_b[l].reshape(1, LRU_W),
        "wa": _block_diag(lru_wa[l]).astype(BF16), "ba": lru_ba[l].reshape(1, LRU_W),
        "wx": _block_diag(lru_wx[l]).astype(BF16), "bx": lru_bx[l].reshape(1, LRU_W),
        "lam": lru_lambda[l].reshape(1, LRU_W),
    }
    ow = {
        "gg": gla_gnorm[l].reshape(1, GLA_W),
        "ggc": gla_gnorm[l].reshape(GLA_W, 1),
        "w_out": w_out[l].astype(BF16),
        "w_r": jnp.concatenate([w_router[l], jnp.zeros((d, SMALL_W - N_EXPERTS), F32)], axis=1).astype(BF16),
        "b_r": jnp.zeros((1, SMALL_W), F32).at[0, 0:N_EXPERTS].set(b_router[l]),
    }
    return inw, lw, ow


def kernel(x_prompt, x_sample, cache_fox_k, cache_fox_v, cache_fox_logf, state_conv, state_lru, state_gla, page_table, c_prompt, c_sample, w_ada, b_ada, g_norm1, g_norm2, w_in, b_forget, conv_w, conv_b, lru_lambda, lru_wa, lru_ba, lru_wx, lru_bx, gla_w2, gla_b2, gla_gnorm, w_out, w_router, b_router, w_gu, b_gu, w_down, b_down, g_final):
    n_layers = w_ada.shape[0]
    bp, seq, d = x_prompt.shape
    bs = x_sample.shape[0]
    n_p = bp * seq
    n_tot = n_p + bs
    n_pool = cache_fox_k.shape[1]
    bm = MOE_BLOCK

    mod = _ada_call(jnp.concatenate([c_prompt, c_sample], axis=0), w_ada, b_ada)
    mod_p = mod[:, :bp].reshape(n_layers, bp, 6, 1, d).transpose(0, 2, 1, 3, 4)
    mod_s = mod[:, bp:].reshape(n_layers, 1, bs, 6, d).transpose(0, 3, 1, 2, 4)

    ckt = cache_fox_k.transpose(0, 1, 3, 4, 2).reshape(n_layers, n_pool, FOX_W, PAGE_SIZE)
    cvt = cache_fox_v.transpose(0, 1, 3, 4, 2).reshape(n_layers, n_pool, FOX_W, PAGE_SIZE)
    cft = cache_fox_logf.transpose(0, 1, 3, 2)
    conv_t = state_conv.transpose(0, 2, 1, 3)
    s_view = state_gla.transpose(0, 2, 3, 4, 1)
    b_gu4 = b_gu.reshape(n_layers, N_EXPERTS, 1, 2 * D_FF)
    b_dn4 = b_down.reshape(n_layers, N_EXPERTS, 1, d)
    gf = g_final.reshape(1, d)

    row_quant = SC_WINDOW * SC_WORKERS // TOP_K
    n_buf = -(-n_tot // row_quant) * row_quant
    n_a = (n_p // 2) // row_quant * row_quant
    parts = ((0, n_a, n_a), (n_a, n_buf - n_a, n_tot - n_a))

    def moe_scatter(l, h2, ti, part):
        row0, n_part, n_valid = part
        n_blocks = -(-(n_valid * TOP_K + N_EXPERTS * (bm - 1)) // bm)
        trash = n_blocks * bm
        dest, cnt = _rank_call(ti, row0, n_part, n_valid, trash, 512)
        counts = cnt[0, 0:N_EXPERTS].astype(I32)
        pad_end = jnp.cumsum((counts + bm - 1) // bm * bm)
        blk_e = jnp.minimum(jnp.sum((jnp.arange(n_blocks, dtype=I32)[:, None] * bm >= pad_end[None, :]).astype(I32),
                                    axis=1), N_EXPERTS - 1).astype(I32)
        n_used = (pad_end[-1] // bm).astype(I32).reshape(1)
        dest_km = dest[0:TOP_K]
        xs = _sc_scatter_rows(h2, dest_km, (n_blocks + 1) * bm, row0)
        return xs, blk_e, n_used, jnp.minimum(dest_km, trash - 1)

    def moe_experts(l, sc):
        xs, blk_e, n_used, src = sc
        ye = _ffn_call(l, blk_e, n_used, xs, w_gu, b_gu4, w_down, b_dn4)
        return _sc_gather_rows(ye, src.reshape(-1)).reshape(TOP_K, src.shape[1], d)

    yp = x_prompt.reshape(n_p, d)
    ys = x_sample.reshape(bs, d)
    kv_p = (jnp.zeros((n_layers, bp, FOX_W, seq), F32), jnp.zeros((n_layers, bp, FOX_W, seq), F32),
            jnp.zeros((n_layers, bp, FOX_HEADS, seq), F32))
    outs_p = [[] for _ in range(3)]
    outs_s = [[] for _ in range(6)]
    for l in range(n_layers):
        inw, lw, ow = _layer_weights(l, w_in, b_forget, conv_w, conv_b, lru_lambda, lru_wa, lru_ba, lru_wx,
                                     lru_bx, gla_w2, gla_b2, gla_gnorm, w_out, w_router, b_router)
        g1 = g_norm1[l].reshape(1, d)
        g2 = g_norm2[l].reshape(1, d)

        kt_p, vt_p, lft_p, rz, qx, kx, vx = _inproj_call(yp, mod_p[l], g1, inw, seq, 256, l, n_layers, kv_bufs=kv_p)
        kv_p = (kt_p, vt_p, lft_p)
        fo = _fox_call(qx, kx, vx, bp, seq, 512)
        lo, conv_p, hlast_p = _lru_call(rz, lw, bp, seq, 256)
        go, st_p = _gla_call(rz, ow["gg"], bp, seq, 256)
        y1p, h2, ti, tw = _outproj_call(yp, fo, lo, go, mod_p[l], g2, ow["w_out"], ow["w_r"], ow["b_r"],
                                        256, seq, n_buf, 0)
        sc_a = moe_scatter(l, h2, ti, parts[0])
        st_p = st_p.reshape(bp, GLA_HEADS, HEAD_DIM, GLA_HEADS, HEAD_DIM)
        st_p = jnp.stack([st_p[:, hh, :, hh, :] for hh in range(GLA_HEADS)], axis=1).transpose(0, 1, 3, 2)
        outs_p[0].append(conv_p)
        outs_p[1].append(hlast_p.reshape(bp, LRU_W))
        outs_p[2].append(st_p)

        kts, vts, lfts, rzs, qs, ks, vs, logfs, gts = _inproj_call(
            ys, mod_s[l], g1, inw, bs, bs, 0, 1, sample=True)
        fos = _fox_dec_call(l, page_table, qs, ks, vs, logfs, ckt, cvt, cft)
        los, conv_s, h_s = _lru_step_call(l, rzs, conv_t, state_lru, lw)
        gost, s_s = _gla_step_call(l, gts, s_view, ow["ggc"])
        y1s, h2, ti, tw = _outproj_call(ys, fos, los, gost.T, mod_s[l], g2, ow["w_out"], ow["w_r"], ow["b_r"],
                                        bs, bs, n_buf, n_p, prev=(h2, ti, tw))
        outs_s[0].append(kts[0, 0])
        outs_s[1].append(vts[0, 0])
        outs_s[2].append(lfts[0, 0])
        outs_s[3].append(conv_s)
        outs_s[4].append(h_s)
        outs_s[5].append(s_s)

        sc_b = moe_scatter(l, h2, ti, parts[1])
        yk_a = moe_experts(l, sc_a)
        yk_b = moe_experts(l, sc_b)
        final = l == n_layers - 1
        yp = _combine_call(y1p, yk_a, tw, mod_p[l], gf, 256, seq, 0, n_a, 0, 0, final)
        yp = _combine_call(yp, yk_b, tw, mod_p[l], gf, 256, seq, n_a, n_p - n_a, 0, n_a, final)
        ys = _combine_call(y1s, yk_b, tw, mod_s[l], gf, bs, bs, 0, bs, n_p - n_a, n_p, final)

    kt_p, vt_p, lft_p = kv_p
    fox_k_p = kt_p.reshape(n_layers, bp, FOX_HEADS, HEAD_DIM, seq).transpose(0, 1, 4, 2, 3)
    fox_v_p = vt_p.reshape(n_layers, bp, FOX_HEADS, HEAD_DIM, seq).transpose(0, 1, 4, 2, 3)
    fox_f_p = lft_p.transpose(0, 1, 3, 2)
    fox_k_s = jnp.stack(outs_s[0]).reshape(n_layers, FOX_HEADS, HEAD_DIM, bs).transpose(0, 3, 1, 2)[:, :, None]
    fox_v_s = jnp.stack(outs_s[1]).reshape(n_layers, FOX_HEADS, HEAD_DIM, bs).transpose(0, 3, 1, 2)[:, :, None]
    fox_f_s = jnp.stack(outs_s[2]).transpose(0, 2, 1)[:, :, None]
    return (yp.reshape(bp, seq, d), ys.reshape(bs, 1, d),
            fox_k_p, fox_v_p, fox_f_p,
            jnp.stack(outs_p[0]), jnp.stack(outs_p[1]), jnp.stack(outs_p[2]),
            fox_k_s, fox_v_s, fox_f_s,
            jnp.stack(outs_s[3]).transpose(0, 2, 1, 3), jnp.stack(outs_s[4]),
            jnp.stack(outs_s[5]).transpose(0, 4, 1, 2, 3))
```

```python
import functools

import jax
import jax.numpy as jnp
from jax import lax
from jax.experimental import pallas as pl
from jax.experimental.pallas import tpu as pltpu
from jax.experimental.pallas import tpu_sc as plsc

F32 = jnp.float32
BF16 = jnp.bfloat16
I32 = jnp.int32

D_MODEL = 1024
HEAD_DIM = 64
FOX_W = 512
FOX_HEADS = 8
LRU_W = 256
GLA_W = 256
GLA_HEADS = 4
GLA_RANK = 16
GLA_CHUNK = 64
GLA_TAU = 16.0
LRU_C = 8.0
CONV_W = 4
N_EXPERTS = 32
TOP_K = 4
D_FF = 1024
SWIGLU_LIMIT = 7.0
SWIGLU_ALPHA = 1.702
NORM_EPS = 1e-6
PAGE_SIZE = 128
QK_SCALE = HEAD_DIM ** -0.5

_O_FQ, _O_FK, _O_FV, _O_FF = 0, 512, 1024, 1536
_O_LX, _O_LG, _O_GQ, _O_GK, _O_GV, _O_GA, _O_GOG = 1544, 1800, 2056, 2312, 2568, 2824, 2840
REC_W = 1792
GLA_T_ROWS = 5 * GLA_W
SMALL_W = 128
SMALL_T = 32
FOX_BLK = 2 * HEAD_DIM
FOX_QX = FOX_HEADS * FOX_BLK

VMEM_LIMIT = 56 * 1024 * 1024
MOE_BLOCK = 256
SC_WINDOW = 128
SC_COLS = 256
SC_WORKERS = 32

_NT = (((1,), (1,)), ((), ()))
_TN = (((0,), (0,)), ((), ()))


def _cparams(sem, vmem=VMEM_LIMIT):
    return pltpu.CompilerParams(dimension_semantics=sem, vmem_limit_bytes=vmem)


def _log_sigmoid(x):
    return jnp.minimum(x, 0.0) - jnp.log1p(jnp.exp(-jnp.abs(x)))


def _softplus(x):
    return jnp.maximum(x, 0.0) + jnp.log1p(jnp.exp(-jnp.abs(x)))


def _cumsum(x, axis):
    n = x.shape[axis]
    idx = lax.broadcasted_iota(I32, x.shape, axis)
    s = 1
    while s < n:
        x = x + jnp.where(idx >= s, pltpu.roll(x, s, axis), 0.0)
        s *= 2
    return x


_HI16 = -65536


def _pack_bf16_pairs(lo, hi):
    lo_bits = lax.bitcast_convert_type(lo.astype(BF16).astype(F32), I32)
    hi_bits = lax.bitcast_convert_type(hi.astype(BF16).astype(F32), I32)
    return lax.shift_right_logical(lo_bits, jnp.full(lo_bits.shape, 16, I32)) | (hi_bits & _HI16)


def _unpack_bf16_pairs(packed, dtype=BF16):
    lo = lax.bitcast_convert_type(lax.shift_left(packed, jnp.full(packed.shape, 16, I32)), F32)
    hi = lax.bitcast_convert_type(packed & _HI16, F32)
    return lo.astype(dtype), hi.astype(dtype)


def _mod_spec(mod, tm, tiles_per_seq, last_tile=None, tile_off=0):
    d = mod.shape[-1]
    clamp = (lambda i: i + tile_off) if last_tile is None else (lambda i: jnp.minimum(i, last_tile))
    if mod.shape[2] != 1:
        return pl.BlockSpec((6, None, tm, d), lambda i: (0, 0, clamp(i), 0))
    return pl.BlockSpec((6, None, 1, d), lambda i: (0, clamp(i) // tiles_per_seq, 0, 0))


def _ada_kernel(c_ref, w_ref, b_ref, o_ref):
    c = c_ref[...]
    a = (c * jax.nn.sigmoid(c)).astype(BF16)
    o_ref[...] = jnp.dot(a, w_ref[...].astype(BF16), preferred_element_type=F32) + b_ref[...]


def _ada_call(c_all, w_ada, b_ada):
    n_layers, d, w = w_ada.shape
    r = c_all.shape[0]
    tn = 1536
    return pl.pallas_call(
        _ada_kernel,
        grid=(n_layers, w // tn),
        in_specs=[
            pl.BlockSpec((r, d), lambda l, j: (0, 0)),
            pl.BlockSpec((None, d, tn), lambda l, j: (l, 0, j)),
            pl.BlockSpec((None, 1, tn), lambda l, j: (l, 0, j)),
        ],
        out_specs=pl.BlockSpec((None, r, tn), lambda l, j: (l, 0, j)),
        out_shape=jax.ShapeDtypeStruct((n_layers, r, w), F32),
        compiler_params=_cparams(("arbitrary", "arbitrary")),
        name="ada_mod",
    )(c_all, w_ada, b_ada.reshape(n_layers, 1, w))


def _inproj_kernel(*refs, tiles_per_seq, sample, n_alias):
    (y_ref, mod_ref, g_ref, wq_ref, wkvt_ref, wrec_ref, wsm_ref, wsmt_ref,
     bsm_ref, bfc_ref, w2_ref, b2_ref) = refs[:12]
    if sample:
        wgt_ref, w2t_ref, b2c_ref = refs[12:15]
        pos = 15 + n_alias
    else:
        eq_ref, oneq_ref, ek_ref, onek_ref = refs[12:16]
        pos = 16 + n_alias
    kt_ref, vt_ref, lft_ref, rz_ref = refs[pos:pos + 4]
    pos += 4
    if sample:
        q_ref, k_ref, v_ref, logf_ref, gt_ref = refs[pos:pos + 5]
        pos += 5
    else:
        qx_ref, kx_ref, vx_ref = refs[pos:pos + 3]
        pos += 3
    carry_c, carry_r = refs[pos:pos + 2]
    i = pl.program_id(0)

    @pl.when(i % tiles_per_seq == 0)
    def _():
        carry_c[...] = jnp.zeros_like(carry_c)
        carry_r[...] = jnp.zeros_like(carry_r)

    x = y_ref[...]
    xn = x * lax.rsqrt(jnp.mean(x * x, axis=-1, keepdims=True) + NORM_EPS) * g_ref[...]
    h = (xn * (1.0 + mod_ref[1]) + mod_ref[0]).astype(BF16)

    kt = lax.dot_general(wkvt_ref[0:FOX_W, :], h, _NT, preferred_element_type=F32)
    vt = lax.dot_general(wkvt_ref[FOX_W:2 * FOX_W, :], h, _NT, preferred_element_type=F32)
    kt_ref[...] = kt
    vt_ref[...] = vt
    rz_ref[:, 0:REC_W - GLA_W] = jnp.dot(h, wrec_ref[...], preferred_element_type=F32)

    sm = jnp.dot(h, wsm_ref[...], preferred_element_type=F32)
    lane = lax.broadcasted_iota(I32, sm.shape, 1)
    logf = jnp.where(lane < FOX_HEADS, _log_sigmoid(sm + bsm_ref[...]), 0.0)
    cum = _cumsum(logf, 0) + carry_c[...]
    carry_c[...] = cum[cum.shape[0] - 1:, :]
    glin =jnp.dot(sm.astype(BF16), w2_ref[...], preferred_element_type=F32) + b2_ref[...]
    rz_ref[:, REC_W - GLA_W:REC_W] = _log_sigmoid(glin) * (1.0 / GLA_TAU)

    smt = lax.dot_general(wsmt_ref[...], h, _NT, preferred_element_type=F32)
    lft = _log_sigmoid(smt[0:FOX_HEADS, :] + bfc_ref[...])
    lft_ref[...] = lft
    cumt = _cumsum(lft, 1) + carry_r[...]
    carry_r[...] = cumt[:, cumt.shape[1] - 1:]

    q = jnp.dot(h, wq_ref[...], preferred_element_type=F32) * QK_SCALE
    if not sample:
        def split3(c):
            hi = c.astype(BF16)
            r1 = c - hi.astype(F32)
            mid = r1.astype(BF16)
            lo = (r1 - mid.astype(F32)).astype(BF16)
            return [hi, mid, lo]

        cq = jnp.concatenate(split3(cum), axis=1)
        qx_ref[...] = (q + jnp.dot(cq, eq_ref[...], preferred_element_type=F32) + oneq_ref[...]).astype(BF16)
        ck = jnp.concatenate(split3(cumt) + [jnp.zeros(cumt.shape, BF16)], axis=0)
        kb = jnp.dot(ek_ref[...], ck, preferred_element_type=F32) + onek_ref[...]
        parts = []
        for hh in range(FOX_HEADS):
            parts += [kt[hh * HEAD_DIM:(hh + 1) * HEAD_DIM, :], kb[hh * HEAD_DIM:(hh + 1) * HEAD_DIM, :]]
        kx_ref[...] = jnp.concatenate(parts, axis=0).astype(BF16)
        vx_ref[...] = vt.astype(BF16)

    if sample:
        q_ref[...] = q.astype(BF16)
        k_ref[...] = kt.T
        v_ref[...] = vt.T
        logf_ref[...] = logf[:, 0:FOX_HEADS]
        gt_ref[0:4 * GLA_W, :] = lax.dot_general(wgt_ref[...], h, _NT, preferred_element_type=F32)
        glt = jnp.dot(w2t_ref[...], smt.astype(BF16), preferred_element_type=F32) + b2c_ref[...]
        gt_ref[4 * GLA_W:GLA_T_ROWS, :] = _log_sigmoid(glt) * (1.0 / GLA_TAU)


def _inproj_call(y, mod, g, wts, seq_len, tm, layer, n_layers, kv_bufs=None, sample=False):
    n, d = y.shape
    tiles_per_seq = seq_len // tm
    n_seq = n // seq_len
    qw = FOX_W if sample else FOX_QX
    const = lambda i: (0, 0)
    row = lambda i: (i, 0)
    seq_t = lambda i: (layer, i // tiles_per_seq, 0, i % tiles_per_seq)
    seq_t3 = lambda i: (i // tiles_per_seq, 0, i % tiles_per_seq)
    in_specs = [
        pl.BlockSpec((tm, d), row),
        _mod_spec(mod, tm, tiles_per_seq),
        pl.BlockSpec((1, d), const),
        pl.BlockSpec((d, qw), const),
        pl.BlockSpec((2 * FOX_W, d), const),
        pl.BlockSpec((d, REC_W - GLA_W), const),
        pl.BlockSpec((d, SMALL_W), const),
        pl.BlockSpec((SMALL_T, d), const),
        pl.BlockSpec((1, SMALL_W), const),
        pl.BlockSpec((FOX_HEADS, 1), const),
        pl.BlockSpec((SMALL_W, GLA_W), const),
        pl.BlockSpec((1, GLA_W), const),
    ]
    args = [y, mod, g, wts["w_q" if sample else "w_qx"], wts["w_kvt"], wts["w_rec"], wts["w_sm"], wts["w_smt"],
            wts["b_sm"], wts["b_fc"], wts["w2"], wts["b2"]]
    if sample:
        in_specs += [pl.BlockSpec((4 * GLA_W, d), const), pl.BlockSpec((GLA_W, SMALL_T), const),
                     pl.BlockSpec((GLA_W, 1), const)]
        args += [wts["w_gt"], wts["w2t"], wts["b2c"]]
    else:
        in_specs += [pl.BlockSpec((3 * SMALL_W, FOX_QX), const), pl.BlockSpec((1, FOX_QX), const),
                     pl.BlockSpec((FOX_W, SMALL_T), const), pl.BlockSpec((FOX_W, 1), const)]
        args += list(_bias_fold_consts())
    aliases = {}
    n_alias = 0
    if kv_bufs is not None:
        n_alias = 3
        first = len(args)
        in_specs += [pl.BlockSpec(memory_space=pl.ANY)] * 3
        args += list(kv_bufs)
        aliases = {first: 0, first + 1: 1, first + 2: 2}
    out_specs = [
        pl.BlockSpec((None, None, FOX_W, tm), seq_t),
        pl.BlockSpec((None, None, FOX_W, tm), seq_t),
        pl.BlockSpec((None, None, FOX_HEADS, tm), seq_t),
        pl.BlockSpec((tm, REC_W), row),
    ]
    out_shape = [
        jax.ShapeDtypeStruct((n_layers, n_seq, FOX_W, seq_len), F32),
        jax.ShapeDtypeStruct((n_layers, n_seq, FOX_W, seq_len), F32),
        jax.ShapeDtypeStruct((n_layers, n_seq, FOX_HEADS, seq_len), F32),
        jax.ShapeDtypeStruct((n, REC_W), F32),
    ]
    if sample:
        out_specs += [pl.BlockSpec((tm, FOX_W), row), pl.BlockSpec((tm, FOX_W), row), pl.BlockSpec((tm, FOX_W), row),
                      pl.BlockSpec((tm, FOX_HEADS), row), pl.BlockSpec((GLA_T_ROWS, tm), lambda i: (0, i))]
        out_shape += [jax.ShapeDtypeStruct((n, FOX_W), BF16), jax.ShapeDtypeStruct((n, FOX_W), F32),
                      jax.ShapeDtypeStruct((n, FOX_W), F32), jax.ShapeDtypeStruct((n, FOX_HEADS), F32),
                      jax.ShapeDtypeStruct((GLA_T_ROWS, n), F32)]
    else:
        out_specs += [pl.BlockSpec((tm, FOX_QX), row), pl.BlockSpec((None, FOX_QX, tm), seq_t3),
                      pl.BlockSpec((None, FOX_W, tm), seq_t3)]
        out_shape += [jax.ShapeDtypeStruct((n, FOX_QX), BF16), jax.ShapeDtypeStruct((n_seq, FOX_QX, seq_len), BF16),
                      jax.ShapeDtypeStruct((n_seq, FOX_W, seq_len), BF16)]
    return pl.pallas_call(
        functools.partial(_inproj_kernel, tiles_per_seq=tiles_per_seq, sample=sample, n_alias=n_alias),
        grid=(n // tm,),
        in_specs=in_specs,
        out_specs=tuple(out_specs),
        out_shape=tuple(out_shape),
        input_output_aliases=aliases,
        scratch_shapes=[pltpu.VMEM((1, SMALL_W), F32), pltpu.VMEM((FOX_HEADS, 1), F32)],
        compiler_params=_cparams(("arbitrary",)),
        name="in_proj",
    )(*args)


def _bias_fold_consts():
    h = jnp.arange(FOX_HEADS)
    eq = jnp.zeros((3 * SMALL_W, FOX_QX), F32)
    ek = jnp.zeros((FOX_W, SMALL_T), F32)
    oneq = jnp.zeros((1, FOX_QX), F32)
    onek = jnp.zeros((FOX_W, 1), F32)
    for piece in range(3):
        eq = eq.at[piece * SMALL_W + h, h * FOX_BLK + HEAD_DIM + piece].set(1.0)
        oneq = oneq.at[0, h * FOX_BLK + HEAD_DIM + 3 + piece].set(1.0)
        onek = onek.at[h * HEAD_DIM + piece, 0].set(1.0)
        ek = ek.at[h * HEAD_DIM + 3 + piece, piece * FOX_HEADS + h].set(-1.0)
    return eq.astype(BF16), oneq, ek.astype(BF16), onek


def _fox_kernel(qi_ref, ki_ref, qx_ref, kx_ref, vx_ref, o_ref, m_sc, l_sc, acc_sc, *, tq, tk):
    qi = qi_ref[pl.program_id(2)]
    ki = ki_ref[pl.program_id(2)]
    w = 2 * HEAD_DIM
    nc = tk // w

    @pl.when(ki == 0)
    def _():
        m_sc[...] = jnp.full_like(m_sc, -jnp.inf)
        l_sc[...] = jnp.zeros_like(l_sc)
        acc_sc[...] = jnp.zeros_like(acc_sc)

    def step(masked):
        vt = vx_ref[...]
        if masked:
            causal = lax.broadcasted_iota(I32, (tq, tk), 1) <= lax.broadcasted_iota(I32, (tq, tk), 0)
        for j in range(2):
            s = jnp.dot(qx_ref[:, j * FOX_BLK:(j + 1) * FOX_BLK], kx_ref[j * FOX_BLK:(j + 1) * FOX_BLK, :],
                        preferred_element_type=F32)
            if masked:
                s = jnp.where(causal, s, -jnp.inf)
            sc = [s[:, c * w:(c + 1) * w] for c in range(nc)]
            mb = sc[0]
            for c in range(1, nc):
                mb = jnp.maximum(mb, sc[c])
            m_prev = m_sc[j]
            m_new = jnp.maximum(m_prev, jnp.broadcast_to(jnp.max(mb, axis=1, keepdims=True), (tq, w)))
            ps = [jnp.exp(sc[c] - m_new) for c in range(nc)]
            lsum = ps[0]
            for c in range(1, nc):
                lsum = lsum + ps[c]
            alpha = jnp.exp(m_prev - m_new)
            l_sc[j] = alpha * l_sc[j] + jnp.broadcast_to(jnp.sum(lsum, axis=1, keepdims=True), (tq, w))
            p = jnp.concatenate([pc.astype(BF16) for pc in ps], axis=1)
            acc_sc[j] = alpha * acc_sc[j] + lax.dot_general(p, vt, _NT, preferred_element_type=F32)
            m_sc[j] = m_new

    @pl.when(ki < qi)
    def _():
        step(False)

    @pl.when(ki == qi)
    def _():
        step(True)
        lane = lax.broadcasted_iota(I32, (tq, w), 1)
        o0 = acc_sc[0] / l_sc[0]
        o1 = acc_sc[1] / l_sc[1]
        o_ref[...] = jnp.where(lane < HEAD_DIM, o0, o1).astype(BF16)


def _fox_call(qx, kx, vx, n_seq, seq_len, tq):
    n = qx.shape[0]
    nt = seq_len // tq
    tk = tq
    hp = FOX_HEADS // 2
    w = 2 * HEAD_DIM
    pairs = [(i, j) for i in range(nt) for j in range(i + 1)]
    qi_tab = jnp.asarray([p[0] for p in pairs], I32)
    ki_tab = jnp.asarray([p[1] for p in pairs], I32)
    past = lambda b, h, s, qt, kt: (b, h, kt[s])
    grid_spec = pltpu.PrefetchScalarGridSpec(
        num_scalar_prefetch=2,
        grid=(n_seq, hp, len(pairs)),
        in_specs=[
            pl.BlockSpec((tq, 2 * FOX_BLK), lambda b, h, s, qt, kt: (b * nt + qt[s], h)),
            pl.BlockSpec((None, 2 * FOX_BLK, tk), past),
            pl.BlockSpec((None, w, tk), past),
        ],
        out_specs=pl.BlockSpec((tq, w), lambda b, h, s, qt, kt: (b * nt + qt[s], h)),
        scratch_shapes=[pltpu.VMEM((2, tq, w), F32), pltpu.VMEM((2, tq, w), F32), pltpu.VMEM((2, tq, w), F32)],
    )
    return pl.pallas_call(
        functools.partial(_fox_kernel, tq=tq, tk=tk),
        grid_spec=grid_spec,
        out_shape=jax.ShapeDtypeStruct((n, FOX_W), BF16),
        compiler_params=_cparams(("arbitrary", "arbitrary", "arbitrary")),
        name="fox_prompt",
    )(qi_tab, ki_tab, qx, kx, vx)


def _lru_gates(xc, wa_ref, ba_ref, wx_ref, bx_ref, lam_ref):
    xb = xc.astype(BF16)
    r = jax.nn.sigmoid(jnp.dot(xb, wa_ref[...], preferred_element_type=F32) + ba_ref[...])
    gi = jax.nn.sigmoid(jnp.dot(xb, wx_ref[...], preferred_element_type=F32) + bx_ref[...])
    log_a = -LRU_C * r * _softplus(-lam_ref[...])
    a = jnp.exp(log_a)
    mult = jnp.sqrt(-jnp.tanh(log_a) * (a * a + 1.0))
    return a, mult, gi


def _lru_kernel(lx_ref, lg_ref, cw_ref, cb_ref, wa_ref, ba_ref, wx_ref, bx_ref, lam_ref,
                lo_ref, conv_ref, hlast_ref, xbuf, hcar, *, tt):
    ti = pl.program_id(1)
    nt = pl.num_programs(1)

    @pl.when(ti == 0)
    def _():
        xbuf[0:8, :] = jnp.zeros((8, LRU_W), F32)
        hcar[...] = jnp.zeros_like(hcar)

    x = lx_ref[...]
    xbuf[8:8 + tt, :] = x
    xc = cb_ref[...] + cw_ref[3:4, :] * x
    for j in range(CONV_W - 1):
        xc = xc + cw_ref[j:j + 1, :] * xbuf[5 + j:5 + j + tt, :]
    xbuf[0:8, :] = x[tt - 8:tt, :]

    a, mult, gi = _lru_gates(xc, wa_ref, ba_ref, wx_ref, bx_ref, lam_ref)
    row = lax.broadcasted_iota(I32, (tt, LRU_W), 0)
    mult = jnp.where((row == 0) & (ti == 0), 1.0, mult)
    b = mult * gi * xc
    s = 1
    while s < tt:
        keep = row >= s
        a_sh = jnp.where(keep, pltpu.roll(a, s, 0), 1.0)
        b_sh = jnp.where(keep, pltpu.roll(b, s, 0), 0.0)
        b = a * b_sh + b
        a = a * a_sh
        s *= 2
    h = a * hcar[...] + b
    hcar[...] = h[tt - 1:tt, :]
    lo_ref[...] = (h * jax.nn.gelu(lg_ref[...])).astype(BF16)

    @pl.when(ti == nt - 1)
    def _():
        conv_ref[...] = x[tt - (CONV_W - 1):tt, :]
        hlast_ref[...] = h[tt - 1:tt, :]


def _lru_weight_specs(const):
    return [
        pl.BlockSpec((CONV_W, LRU_W), const), pl.BlockSpec((1, LRU_W), const),
        pl.BlockSpec((LRU_W, LRU_W), const), pl.BlockSpec((1, LRU_W), const),
        pl.BlockSpec((LRU_W, LRU_W), const), pl.BlockSpec((1, LRU_W), const),
        pl.BlockSpec((1, LRU_W), const),
    ]


def _lru_weight_args(lw):
    return [lw["conv_w"], lw["conv_b"], lw["wa"], lw["ba"], lw["wx"], lw["bx"], lw["lam"]]


def _lru_call(rz, lw, n_seq, seq_len, tt):
    n = rz.shape[0]
    nt = seq_len // tt
    return pl.pallas_call(
        functools.partial(_lru_kernel, tt=tt),
        grid=(n_seq, nt),
        in_specs=[
            pl.BlockSpec((tt, LRU_W), lambda b, t: (b * nt + t, 0)),
            pl.BlockSpec((tt, LRU_W), lambda b, t: (b * nt + t, 1)),
        ] + _lru_weight_specs(lambda b, t: (0, 0)),
        out_specs=(
            pl.BlockSpec((tt, LRU_W), lambda b, t: (b * nt + t, 0)),
            pl.BlockSpec((None, CONV_W - 1, LRU_W), lambda b, t: (b, 0, 0)),
            pl.BlockSpec((None, 1, LRU_W), lambda b, t: (b, 0, 0)),
        ),
        out_shape=(
            jax.ShapeDtypeStruct((n, LRU_W), BF16),
            jax.ShapeDtypeStruct((n_seq, CONV_W - 1, LRU_W), F32),
            jax.ShapeDtypeStruct((n_seq, 1, LRU_W), F32),
        ),
        scratch_shapes=[pltpu.VMEM((tt + 8, LRU_W), F32), pltpu.VMEM((1, LRU_W), F32)],
        compiler_params=_cparams(("arbitrary", "arbitrary")),
        name="lru_prompt",
    )(rz, rz, *_lru_weight_args(lw))


def _head_rms_gate(o, gg_ref, gog):
    lane = lax.broadcasted_iota(I32, o.shape, 1)
    o2 = o * o
    rs = jnp.zeros_like(o)
    for hh in range(GLA_HEADS):
        mh = lane // HEAD_DIM == hh
        ms = jnp.sum(jnp.where(mh, o2, 0.0), axis=1, keepdims=True) * (1.0 / HEAD_DIM)
        rs = jnp.where(mh, lax.rsqrt(ms + NORM_EPS), rs)
    return o * rs * gg_ref[...] * (gog * jax.nn.sigmoid(gog))


def _gla_kernel(gq_ref, gk_ref, gv_ref, gog_ref, gl_ref, gg_ref, go_ref, st_ref, s_sc, *, tt):
    ti = pl.program_id(1)
    nt = pl.num_programs(1)
    c = GLA_CHUNK

    @pl.when(ti == 0)
    def _():
        s_sc[...] = jnp.zeros_like(s_sc)

    lane = lax.broadcasted_iota(I32, (c, GLA_W), 1)
    r2 = lax.broadcasted_iota(I32, (GLA_W, GLA_W), 0)
    c2 = lax.broadcasted_iota(I32, (GLA_W, GLA_W), 1)
    same_head = (r2 // HEAD_DIM) == (c2 // HEAD_DIM)
    tril = lax.broadcasted_iota(I32, (c, c), 1) <= lax.broadcasted_iota(I32, (c, c), 0)

    for ci in range(tt // c):
        sl = slice(ci * c, (ci + 1) * c)
        q = gq_ref[sl, :] * QK_SCALE
        k = gk_ref[sl, :]
        v = gv_ref[sl, :].astype(BF16)
        bc = _cumsum(gl_ref[sl, :], 0)
        b_last = bc[c - 1:c, :]
        qd = (q * jnp.exp(bc)).astype(BF16)
        kinv = (k * jnp.exp(-bc)).astype(BF16)
        kdec = (k * jnp.exp(b_last - bc)).astype(BF16)
        s_prev = s_sc[...]
        o = lax.dot_general(qd, s_prev.astype(BF16), _NT, preferred_element_type=F32)
        for hh in range(GLA_HEADS):
            mh = lane // HEAD_DIM == hh
            att = lax.dot_general(jnp.where(mh, qd, jnp.zeros_like(qd)), kinv, _NT, preferred_element_type=F32)
            att = jnp.where(tril, att, 0.0).astype(BF16)
            o = o + jnp.dot(att, jnp.where(mh, v, jnp.zeros_like(v)), preferred_element_type=F32)
        ut = lax.dot_general(v, kdec, _TN, preferred_element_type=F32)
        s_sc[...] = s_prev * jnp.exp(b_last) + jnp.where(same_head, ut, 0.0)
        go_ref[sl, :] = _head_rms_gate(o, gg_ref, gog_ref[sl, :]).astype(BF16)

    @pl.when(ti == nt - 1)
    def _():
        st_ref[...] = s_sc[...]


def _gla_call(rz, gg, n_seq, seq_len, tt):
    n = rz.shape[0]
    nt = seq_len // tt

    def col(j):
        return pl.BlockSpec((tt, GLA_W), lambda b, t: (b * nt + t, j))

    return pl.pallas_call(
        functools.partial(_gla_kernel, tt=tt),
        grid=(n_seq, nt),
        in_specs=[col(2), col(3), col(4), col(5), col(6), pl.BlockSpec((1, GLA_W), lambda b, t: (0, 0))],
        out_specs=(
            pl.BlockSpec((tt, GLA_W), lambda b, t: (b * nt + t, 0)),
            pl.BlockSpec((None, GLA_W, GLA_W), lambda b, t: (b, 0, 0)),
        ),
        out_shape=(
            jax.ShapeDtypeStruct((n, GLA_W), BF16),
            jax.ShapeDtypeStruct((n_seq, GLA_W, GLA_W), F32),
        ),
        scratch_shapes=[pltpu.VMEM((GLA_W, GLA_W), F32)],
        compiler_params=_cparams(("arbitrary", "arbitrary")),
        name="gla_prompt",
    )(rz, rz, rz, rz, rz, gg)


def _fox_dec_kernel(pt_ref, q_ref, kn_ref, vn_ref, dn_ref, *refs, n_pages):
    del pt_ref
    k_refs = refs[0:n_pages]
    v_refs = refs[n_pages:2 * n_pages]
    f_refs = refs[2 * n_pages:3 * n_pages]
    o_ref = refs[3 * n_pages]
    w = FOX_W
    hrow = lax.broadcasted_iota(I32, (FOX_HEADS, w), 0)
    hlane = lax.broadcasted_iota(I32, (FOX_HEADS, w), 1) // HEAD_DIM
    diag = hrow == hlane
    q = q_ref[...].astype(F32)
    qbd = jnp.where(diag, jnp.broadcast_to(q, (FOX_HEADS, w)), 0.0).astype(BF16)
    s = jnp.concatenate(
        [jnp.dot(qbd, k_refs[p][...].astype(BF16), preferred_element_type=F32) for p in range(n_pages)], axis=1)
    lf = jnp.concatenate([f_refs[p][...] for p in range(n_pages)], axis=1)
    cs = _cumsum(lf, 1)
    suffix = cs[:, cs.shape[1] - 1:] - cs
    s = s + dn_ref[...] + suffix
    s_new = jnp.sum(qbd.astype(F32) * kn_ref[...], axis=1, keepdims=True)
    m = jnp.maximum(jnp.max(s, axis=1, keepdims=True), s_new)
    p_past = jnp.exp(s - m)
    p_new = jnp.exp(s_new - m)
    denom = jnp.sum(p_past, axis=1, keepdims=True) + p_new
    acc = p_new * vn_ref[...]
    pb = p_past.astype(BF16)
    for p in range(n_pages):
        acc = acc + lax.dot_general(pb[:, p * PAGE_SIZE:(p + 1) * PAGE_SIZE], v_refs[p][...].astype(BF16), _NT,
                                    preferred_element_type=F32)
    out = jnp.where(diag, acc / denom, 0.0)
    o_ref[...] = jnp.sum(out, axis=0, keepdims=True).astype(BF16)


def _fox_dec_call(layer, page_table, q, k_new, v_new, logf_new, cache_kt, cache_vt, cache_ft):
    bd, n_pages = page_table.shape
    w = FOX_W

    def page_spec(rows, j):
        return pl.BlockSpec((None, None, rows, PAGE_SIZE), lambda b, pt, j=j: (layer, pt[b, j], 0, 0))

    row = lambda b, pt: (b, 0, 0)
    in_specs = [
        pl.BlockSpec((None, 1, w), row),
        pl.BlockSpec((None, 1, w), row),
        pl.BlockSpec((None, 1, w), row),
        pl.BlockSpec((None, FOX_HEADS, 1), row),
    ]
    in_specs += [page_spec(w, j) for j in range(n_pages)]
    in_specs += [page_spec(w, j) for j in range(n_pages)]
    in_specs += [page_spec(FOX_HEADS, j) for j in range(n_pages)]
    grid_spec = pltpu.PrefetchScalarGridSpec(
        num_scalar_prefetch=1,
        grid=(bd,),
        in_specs=in_specs,
        out_specs=pl.BlockSpec((None, 1, w), row),
    )
    out = pl.pallas_call(
        functools.partial(_fox_dec_kernel, n_pages=n_pages),
        grid_spec=grid_spec,
        out_shape=jax.ShapeDtypeStruct((bd, 1, w), BF16),
        compiler_params=_cparams(("arbitrary",)),
        name="fox_sample",
    )(page_table, q.reshape(bd, 1, w), k_new.reshape(bd, 1, w), v_new.reshape(bd, 1, w),
      logf_new.reshape(bd, FOX_HEADS, 1),
      *([cache_kt] * n_pages), *([cache_vt] * n_pages), *([cache_ft] * n_pages))
    return out.reshape(bd, w)


def _lru_step_kernel(lx_ref, lg_ref, conv_ref, h0_ref, cw_ref, cb_ref, wa_ref, ba_ref, wx_ref, bx_ref, lam_ref,
                     lo_ref, convn_ref, hn_ref):
    x = lx_ref[...]
    xc = cb_ref[...] + cw_ref[3:4, :] * x
    for j in range(CONV_W - 1):
        xc = xc + cw_ref[j:j + 1, :] * conv_ref[j]
    convn_ref[0] = conv_ref[1]
    convn_ref[1] = conv_ref[2]
    convn_ref[2] = x
    a, mult, gi = _lru_gates(xc, wa_ref, ba_ref, wx_ref, bx_ref, lam_ref)
    h = a * h0_ref[...] + mult * gi * xc
    hn_ref[...] = h
    lo_ref[...] = (h * jax.nn.gelu(lg_ref[...])).astype(BF16)


def _lru_step_call(layer, rz, conv_t, h0, lw):
    bd = rz.shape[0]
    return pl.pallas_call(
        _lru_step_kernel,
        grid=(1,),
        in_specs=[
            pl.BlockSpec((bd, LRU_W), lambda i: (0, 0)),
            pl.BlockSpec((bd, LRU_W), lambda i: (0, 1)),
            pl.BlockSpec((None, CONV_W - 1, bd, LRU_W), lambda i: (layer, 0, 0, 0)),
            pl.BlockSpec((None, bd, LRU_W), lambda i: (layer, 0, 0)),
        ] + _lru_weight_specs(lambda i: (0, 0)),
        out_specs=(
            pl.BlockSpec((bd, LRU_W), lambda i: (0, 0)),
            pl.BlockSpec((CONV_W - 1, bd, LRU_W), lambda i: (0, 0, 0)),
            pl.BlockSpec((bd, LRU_W), lambda i: (0, 0)),
        ),
        out_shape=(
            jax.ShapeDtypeStruct((bd, LRU_W), BF16),
            jax.ShapeDtypeStruct((CONV_W - 1, bd, LRU_W), F32),
            jax.ShapeDtypeStruct((bd, LRU_W), F32),
        ),
        compiler_params=_cparams(("arbitrary",)),
        name="lru_step",
    )(rz, rz, conv_t, h0, *_lru_weight_args(lw))


def _gla_step_kernel(q_ref, k_ref, v_ref, gog_ref, gl_ref, gg_ref, s_ref, go_ref, sn_ref):
    eg = jnp.exp(gl_ref[...])
    kt = k_ref[...]
    qt = q_ref[...] * QK_SCALE
    vt = v_ref[...]
    o = jnp.zeros_like(vt)
    for kk in range(HEAD_DIM):
        s_new = eg[kk:kk + 1, :] * s_ref[kk] + kt[kk:kk + 1, :] * vt
        sn_ref[kk] = s_new
        o = o + qt[kk:kk + 1, :] * s_new
    ms = jnp.mean(o * o, axis=0, keepdims=True)
    gog = gog_ref[...]
    go_ref[...] = o * lax.rsqrt(ms + NORM_EPS) * gg_ref[...] * (gog * jax.nn.sigmoid(gog))


def _gla_step_call(layer, gt, s_view, ggc):
    bd = gt.shape[1]
    hd = HEAD_DIM

    def part(j):
        return pl.BlockSpec((hd, bd), lambda h, j=j: (j * GLA_HEADS + h, 0))

    return pl.pallas_call(
        _gla_step_kernel,
        grid=(GLA_HEADS,),
        in_specs=[part(0), part(1), part(2), part(3), part(4),
                  pl.BlockSpec((hd, 1), lambda h: (h, 0)),
                  pl.BlockSpec((None, None, hd, hd, bd), lambda h: (layer, h, 0, 0, 0))],
        out_specs=(
            pl.BlockSpec((hd, bd), lambda h: (h, 0)),
            pl.BlockSpec((None, hd, hd, bd), lambda h: (h, 0, 0, 0)),
        ),
        out_shape=(
            jax.ShapeDtypeStruct((GLA_W, bd), F32),
            jax.ShapeDtypeStruct((GLA_HEADS, hd, hd, bd), F32),
        ),
        compiler_params=_cparams(("arbitrary",)),
        name="gla_step",
    )(gt, gt, gt, gt, gt, ggc, s_view)


def _outproj_kernel(*refs, n_alias, n_tiles):
    h2_ref, ti_ref, tw_ref = refs[10 + n_alias:13 + n_alias]
    i = pl.program_id(0)

    @pl.when(i < n_tiles)
    def _():
        _outproj_tile(*refs[0:9], *refs[9 + n_alias:13 + n_alias])

    @pl.when(i >= n_tiles)
    def _():
        h2_ref[...] = jnp.zeros_like(h2_ref)
        ti_ref[...] = jnp.zeros_like(ti_ref)
        tw_ref[...] = jnp.zeros_like(tw_ref)


def _outproj_tile(y_ref, fo_ref, lo_ref, go_ref, mod_ref, g2_ref, wo_ref, wr_ref, br_ref,
                  y1_ref, h2_ref, ti_ref, tw_ref):
    m = jnp.dot(fo_ref[...], wo_ref[0:FOX_W, :], preferred_element_type=F32)
    m = m + jnp.dot(lo_ref[...], wo_ref[FOX_W:FOX_W + LRU_W, :], preferred_element_type=F32)
    m = m + jnp.dot(go_ref[...].astype(BF16), wo_ref[FOX_W + LRU_W:, :], preferred_element_type=F32)
    y1 = y_ref[...] + mod_ref[2] * m
    y1_ref[...] = y1
    xn = y1 * lax.rsqrt(jnp.mean(y1 * y1, axis=-1, keepdims=True) + NORM_EPS) * g2_ref[...]
    h2 = xn * (1.0 + mod_ref[4]) + mod_ref[3]
    half = h2.shape[1] // 2
    h2_ref[...] = _pack_bf16_pairs(h2[:, 0:half], h2[:, half:])
    logits = jnp.dot(h2.astype(BF16), wr_ref[...], preferred_element_type=F32) + br_ref[...]
    lane = lax.broadcasted_iota(I32, logits.shape, 1)
    logits = jnp.where(lane < N_EXPERTS, logits, -jnp.inf)
    idx_out = jnp.zeros(logits.shape, I32)
    val_out = jnp.zeros(logits.shape, F32)
    vals = []
    for kk in range(TOP_K):
        mx = jnp.max(logits, axis=1, keepdims=True)
        sel = jnp.min(jnp.where(logits == mx, lane, SMALL_W), axis=1, keepdims=True)
        idx_out = jnp.where(lane == kk, sel, idx_out)
        vals.append(mx)
        logits = jnp.where(lane == sel, -jnp.inf, logits)
    es = [jnp.exp(vv - vals[0]) for vv in vals]
    tot = es[0] + es[1] + es[2] + es[3]
    for kk in range(TOP_K):
        val_out = jnp.where(lane == kk, es[kk] / tot, val_out)
    ti_ref[...] = idx_out
    tw_ref[...] = val_out


def _outproj_call(y, fo, lo, go, mod, g2, w_out, w_r, b_r, tm, seq_len, n_buf, row_off, prev=None):
    n, d = y.shape
    tiles_per_seq = max(seq_len // tm, 1)
    n_tiles = n // tm
    n_steps = n_tiles if prev is not None else n_buf // tm
    const = lambda i: (0, 0)
    row = lambda i: (jnp.minimum(i, n_tiles - 1), 0)
    off = row_off // tm
    orow = lambda i: (i + off, 0)
    in_specs = [
        pl.BlockSpec((tm, d), row),
        pl.BlockSpec((tm, FOX_W), row),
        pl.BlockSpec((tm, LRU_W), row),
        pl.BlockSpec((tm, GLA_W), row),
        _mod_spec(mod, tm, tiles_per_seq, n_tiles - 1),
        pl.BlockSpec((1, d), const),
        pl.BlockSpec((d, d), const),
        pl.BlockSpec((d, SMALL_W), const),
        pl.BlockSpec((1, SMALL_W), const),
    ]
    args = [y, fo, lo, go, mod, g2, w_out, w_r, b_r]
    aliases = {}
    if prev is not None:
        in_specs += [pl.BlockSpec(memory_space=pl.ANY)] * 3
        args += list(prev)
        aliases = {9: 1, 10: 2, 11: 3}
    return pl.pallas_call(
        functools.partial(_outproj_kernel, n_alias=0 if prev is None else 3, n_tiles=n_tiles),
        grid=(n_steps,),
        in_specs=in_specs,
        out_specs=(
            pl.BlockSpec((tm, d), row),
            pl.BlockSpec((tm, d // 2), orow),
            pl.BlockSpec((tm, SMALL_W), orow),
            pl.BlockSpec((tm, SMALL_W), orow),
        ),
        out_shape=(
            jax.ShapeDtypeStruct((n, d), F32),
            jax.ShapeDtypeStruct((n_buf, d // 2), I32),
            jax.ShapeDtypeStruct((n_buf, SMALL_W), I32),
            jax.ShapeDtypeStruct((n_buf, SMALL_W), F32),
        ),
        input_output_aliases=aliases,
        compiler_params=_cparams(("arbitrary",)),
        name="out_proj_router",
    )(*args)


def _rank_kernel(ti_ref, dest_ref, cnt_ref, carry, *, tm, n_valid, trash):
    p = pl.program_id(0)
    i = pl.program_id(1)
    nt = pl.num_programs(1)

    @pl.when((p == 0) & (i == 0))
    def _():
        carry[...] = jnp.zeros_like(carry)

    @pl.when((p == 1) & (i == 0))
    def _():
        cnt = carry[...]
        cnt_ref[...] = cnt
        padded = jnp.floor((cnt + (MOE_BLOCK - 1.0)) * (1.0 / MOE_BLOCK)) * MOE_BLOCK
        carry[...] = _cumsum(padded, 1) - padded

    t = ti_ref[...]
    lane = lax.broadcasted_iota(I32, (tm, SMALL_W), 1)
    valid = (lax.broadcasted_iota(I32, (tm, 1), 0) + i * tm) < n_valid
    ohs = [jnp.where(valid, (lane == t[:, kk:kk + 1]).astype(F32), 0.0) for kk in range(TOP_K)]

    @pl.when(p == 0)
    def _():
        carry[...] = carry[...] + jnp.sum(sum(ohs[1:], ohs[0]), axis=0, keepdims=True)

    @pl.when(p == 1)
    def _():
        before = (lax.broadcasted_iota(I32, (tm, tm), 1) < lax.broadcasted_iota(I32, (tm, tm), 0)).astype(BF16)
        base = carry[...]
        out = jnp.zeros((tm, SMALL_W), F32)
        for kk in range(TOP_K):
            oh = ohs[kk]
            pre = jnp.dot(before, oh.astype(BF16), preferred_element_type=F32)
            slot = jnp.sum(oh * (pre + base), axis=1, keepdims=True)
            out = jnp.where(lane == kk, jnp.where(valid, slot, float(trash)), out)
            base = base + jnp.sum(oh, axis=0, keepdims=True)
        carry[...] = base
        dest_ref[...] = out.T[0:8, :].astype(I32)


def _rank_call(ti, row0, n_rows, n_valid, trash, tm):
    off = row0 // tm
    return pl.pallas_call(
        functools.partial(_rank_kernel, tm=tm, n_valid=n_valid, trash=trash),
        grid=(2, n_rows // tm),
        in_specs=[pl.BlockSpec((tm, SMALL_W), lambda p, i: (i + off, 0))],
        out_specs=(
            pl.BlockSpec((8, tm), lambda p, i: (0, i * p)),
            pl.BlockSpec((1, SMALL_W), lambda p, i: (0, 0)),
        ),
        out_shape=(
            jax.ShapeDtypeStruct((8, n_rows), I32),
            jax.ShapeDtypeStruct((1, SMALL_W), F32),
        ),
        scratch_shapes=[pltpu.VMEM((1, SMALL_W), F32)],
        compiler_params=_cparams(("arbitrary", "arbitrary")),
        name="moe_rank",
    )(ti)


def _sc_mesh():
    return plsc.VectorSubcoreMesh(core_axis_name="core", subcore_axis_name="subcore")


def _sc_scatter_rows(x, dest_km, n_out, row0=0):
    kk, n = dest_km.shape
    d = x.shape[1]
    nb = n // SC_WINDOW
    off = row0 // SC_WINDOW
    assert (kk * nb) % SC_WORKERS == 0 and d % SC_COLS == 0 and row0 % SC_WINDOW == 0

    @pl.kernel(out_type=jax.ShapeDtypeStruct((n_out, d), x.dtype), mesh=_sc_mesh())
    def scatter_kernel(x_hbm, i_hbm, o_hbm):
        def body(x_vmem, i_vmem):
            j = pl.program_id(1)
            pltpu.sync_copy(x_vmem, o_hbm.at[i_vmem.at[0], pl.ds(j * SC_COLS, SC_COLS)])

        pltpu.emit_pipeline(
            body,
            grid=(kk * nb, d // SC_COLS),
            in_specs=[pl.BlockSpec((SC_WINDOW, SC_COLS), lambda g, j: (g % nb + off, j)),
                      pl.BlockSpec((1, SC_WINDOW), lambda g, j: (g // nb, g % nb))],
            out_specs=[],
            core_axis_name=("core", "subcore"),
            dimension_semantics=(pltpu.PARALLEL, pltpu.ARBITRARY),
        )(x_hbm, i_hbm)

    return scatter_kernel(x, dest_km)


def _sc_gather_rows(x, idx):
    n = idx.shape[0]
    d = x.shape[1]
    assert (n // SC_WINDOW) % SC_WORKERS == 0 and d % SC_COLS == 0

    @pl.kernel(out_type=jax.ShapeDtypeStruct((n, d), x.dtype), mesh=_sc_mesh())
    def gather_kernel(x_hbm, i_hbm, o_hbm):
        def body(i_vmem, o_vmem):
            j = pl.program_id(1)
            pltpu.sync_copy(x_hbm.at[i_vmem.at[0], pl.ds(j * SC_COLS, SC_COLS)], o_vmem)

        pltpu.emit_pipeline(
            body,
            grid=(n // SC_WINDOW, d // SC_COLS),
            in_specs=[pl.BlockSpec((1, SC_WINDOW), lambda i, j: (0, i))],
            out_specs=[pl.BlockSpec((SC_WINDOW, SC_COLS), lambda i, j: (i, j))],
            core_axis_name=("core", "subcore"),
            dimension_semantics=(pltpu.PARALLEL, pltpu.ARBITRARY),
        )(i_hbm, o_hbm)

    return gather_kernel(x, idx.reshape(1, n))


def _ffn_kernel(be_ref, nu_ref, x_ref, wgu_ref, bgu_ref, wdn_ref, bdn_ref, y_ref, wgu_b, wdn_b):
    j = pl.program_id(0)
    e = be_ref[j]
    prev = be_ref[jnp.maximum(j - 1, 0)]

    @pl.when((j == 0) | (e != prev))
    def _():
        wgu_b[...] = wgu_ref[...].astype(BF16)
        wdn_b[...] = wdn_ref[...].astype(BF16)

    @pl.when(j < nu_ref[0])
    def _():
        xp = x_ref[...]
        half = xp.shape[1]
        x_lo, x_hi = _unpack_bf16_pairs(xp)
        gu = (jnp.dot(x_lo, wgu_b[0:half, :], preferred_element_type=F32)
              + jnp.dot(x_hi, wgu_b[half:, :], preferred_element_type=F32) + bgu_ref[...])
        g = jnp.minimum(gu[:, 0:D_FF], SWIGLU_LIMIT)
        u = jnp.clip(gu[:, D_FF:], -SWIGLU_LIMIT, SWIGLU_LIMIT)
        act = g * jax.nn.sigmoid(SWIGLU_ALPHA * g)
        hmid = ((u + 1.0) * act).astype(BF16)
        y = jnp.dot(hmid, wdn_b[...], preferred_element_type=F32) + bdn_ref[...]
        y_ref[...] = _pack_bf16_pairs(y[:, 0:half], y[:, half:])

    @pl.when(j >= nu_ref[0])
    def _():
        y_ref[...] = jnp.zeros_like(y_ref)


def _ffn_call(layer, blk_e, n_used, xs, w_gu, b_gu, w_dn, b_dn):
    n_blocks = blk_e.shape[0]
    d = 2 * xs.shape[1]
    bm = MOE_BLOCK
    grid_spec = pltpu.PrefetchScalarGridSpec(
        num_scalar_prefetch=2,
        grid=(n_blocks,),
        in_specs=[
            pl.BlockSpec((bm, d // 2), lambda j, be, nu: (j, 0)),
            pl.BlockSpec((None, None, d, 2 * D_FF), lambda j, be, nu: (layer, be[j], 0, 0)),
            pl.BlockSpec((None, None, 1, 2 * D_FF), lambda j, be, nu: (layer, be[j], 0, 0)),
            pl.BlockSpec((None, None, D_FF, d), lambda j, be, nu: (layer, be[j], 0, 0)),
            pl.BlockSpec((None, None, 1, d), lambda j, be, nu: (layer, be[j], 0, 0)),
        ],
        out_specs=pl.BlockSpec((bm, d // 2), lambda j, be, nu: (j, 0)),
        scratch_shapes=[pltpu.VMEM((d, 2 * D_FF), BF16), pltpu.VMEM((D_FF, d), BF16)],
    )
    return pl.pallas_call(
        _ffn_kernel,
        grid_spec=grid_spec,
        out_shape=jax.ShapeDtypeStruct((n_blocks * bm, d // 2), I32),
        compiler_params=_cparams(("arbitrary",)),
        name="expert_ffn",
    )(blk_e, n_used, xs, w_gu, b_gu, w_dn, b_dn)


def _combine_kernel(y1_ref, yk_ref, tw_ref, mod_ref, gf_ref, o_ref, *, final):
    tw = tw_ref[...]
    acc_lo, acc_hi = None, None
    for kk in range(TOP_K):
        lo, hi = _unpack_bf16_pairs(yk_ref[kk], F32)
        wk = tw[:, kk:kk + 1]
        acc_lo = wk * lo if acc_lo is None else acc_lo + wk * lo
        acc_hi = wk * hi if acc_hi is None else acc_hi + wk * hi
    y2 = y1_ref[...] + mod_ref[5] * jnp.concatenate([acc_lo, acc_hi], axis=1)
    if final:
        y2 = y2 * lax.rsqrt(jnp.mean(y2 * y2, axis=-1, keepdims=True) + NORM_EPS) * gf_ref[...]
    o_ref[...] = y2


def _combine_call(y1, yk, tw, mod, g_final, tm, seq_len, y_row0, n_rows, yk_row0, tw_row0, final):
    n, d = y1.shape
    tiles_per_seq = max(seq_len // tm, 1)
    y_off, yk_off, tw_off = y_row0 // tm, yk_row0 // tm, tw_row0 // tm
    return pl.pallas_call(
        functools.partial(_combine_kernel, final=final),
        grid=(n_rows // tm,),
        in_specs=[
            pl.BlockSpec((tm, d), lambda i: (i + y_off, 0)),
            pl.BlockSpec((TOP_K, tm, d // 2), lambda i: (0, i + yk_off, 0)),
            pl.BlockSpec((tm, SMALL_W), lambda i: (i + tw_off, 0)),
            _mod_spec(mod, tm, tiles_per_seq, tile_off=y_off),
            pl.BlockSpec((1, d), lambda i: (0, 0)),
        ],
        out_specs=pl.BlockSpec((tm, d), lambda i: (i + y_off, 0)),
        out_shape=jax.ShapeDtypeStruct((n, d), F32),
        input_output_aliases={0: 0},
        compiler_params=_cparams(("arbitrary",)),
        name="moe_combine",
    )(y1, yk, tw, mod, g_final)


def _block_diag(w):
    nb, bw, _ = w.shape
    eye = jnp.eye(nb, dtype=w.dtype)
    return (eye[:, None, :, None] * w[:, :, None, :]).reshape(nb * bw, nb * bw)


def _layer_weights(l, w_in, b_forget, conv_w, conv_b, lru_lambda, lru_wa, lru_ba, lru_wx, lru_bx,
                   gla_w2, gla_b2, gla_gnorm, w_out, w_router, b_router):
    wi = w_in[l]
    d = wi.shape[0]
    w_ff = wi[:, _O_FF:_O_LX]
    w_ga = wi[:, _O_GA:_O_GOG]
    w_rec = jnp.concatenate([wi[:, _O_LX:_O_GA], wi[:, _O_GOG:]], axis=1)
    w2 = jnp.zeros((SMALL_W, GLA_W), F32).at[FOX_HEADS:FOX_HEADS + GLA_RANK].set(gla_w2[l])
    inw = {
        "w_q": wi[:, _O_FQ:_O_FK].astype(BF16),
        "w_qx": jnp.pad(wi[:, _O_FQ:_O_FK].reshape(d, FOX_HEADS, HEAD_DIM),
                        ((0, 0), (0, 0), (0, FOX_BLK - HEAD_DIM))).reshape(d, FOX_QX).astype(BF16),
        "w_kvt": wi[:, _O_FK:_O_FF].T.astype(BF16),
        "w_rec": w_rec.astype(BF16),
        "w_sm": jnp.concatenate([w_ff, w_ga, jnp.zeros((d, SMALL_W - FOX_HEADS - GLA_RANK), F32)], axis=1).astype(BF16),
        "w_smt": jnp.concatenate([w_ff, w_ga, jnp.zeros((d, SMALL_T - FOX_HEADS - GLA_RANK), F32)], axis=1).T.astype(BF16),
        "b_sm": jnp.zeros((1, SMALL_W), F32).at[0, 0:FOX_HEADS].set(b_forget[l]),
        "b_fc": b_forget[l].reshape(FOX_HEADS, 1),
        "w2": w2.astype(BF16),
        "b2": gla_b2[l].reshape(1, GLA_W),
        "w_gt": w_rec[:, 2 * GLA_W:].T.astype(BF16),
        "w2t": w2[0:SMALL_T].T.astype(BF16),
        "b2c": gla_b2[l].reshape(GLA_W, 1),
    }
    lw = {
        "conv_w": conv_w[l], "conv_b": conv_b[l].reshape(1, LRU_W),
        "wa": _block_diag(lru_wa[l]).astype(BF16), "ba": lru_ba[l].reshape(1, LRU_W),
        "wx": _block_diag(lru_wx[l]).astype(BF16), "bx": lru_bx[l].reshape(1, LRU_W),
        "lam": lru_lambda[l].reshape(1, LRU_W),
    }
    ow = {
        "gg": gla_gnorm[l].reshape(1, GLA_W),
        "ggc": gla_gnorm[l].reshape(GLA_W, 1),
        "w_out": w_out[l].astype(BF16),
        "w_r": jnp.concatenate([w_router[l], jnp.zeros((d, SMALL_W - N_EXPERTS), F32)], axis=1).astype(BF16),
        "b_r": jnp.zeros((1, SMALL_W), F32).at[0, 0:N_EXPERTS].set(b_router[l]),
    }
    return inw, lw, ow


def kernel(x_prompt, x_sample, cache_fox_k, cache_fox_v, cache_fox_logf, state_conv, state_lru, state_gla, page_table, c_prompt, c_sample, w_ada, b_ada, g_norm1, g_norm2, w_in, b_forget, conv_w, conv_b, lru_lambda, lru_wa, lru_ba, lru_wx, lru_bx, gla_w2, gla_b2, gla_gnorm, w_out, w_router, b_router, w_gu, b_gu, w_down, b_down, g_final):
    n_layers = w_ada.shape[0]
    bp, seq, d = x_prompt.shape
    bs = x_sample.shape[0]
    n_p = bp * seq
    n_tot = n_p + bs
    n_pool = cache_fox_k.shape[1]
    bm = MOE_BLOCK

    mod = _ada_call(jnp.concatenate([c_prompt, c_sample], axis=0), w_ada, b_ada)
    mod_p = mod[:, :bp].reshape(n_layers, bp, 6, 1, d).transpose(0, 2, 1, 3, 4)
    mod_s = mod[:, bp:].reshape(n_layers, 1, bs, 6, d).transpose(0, 3, 1, 2, 4)

    ckt = cache_fox_k.transpose(0, 1, 3, 4, 2).reshape(n_layers, n_pool, FOX_W, PAGE_SIZE)
    cvt = cache_fox_v.transpose(0, 1, 3, 4, 2).reshape(n_layers, n_pool, FOX_W, PAGE_SIZE)
    cft = cache_fox_logf.transpose(0, 1, 3, 2)
    conv_t = state_conv.transpose(0, 2, 1, 3)
    s_view = state_gla.transpose(0, 2, 3, 4, 1)
    b_gu4 = b_gu.reshape(n_layers, N_EXPERTS, 1, 2 * D_FF)
    b_dn4 = b_down.reshape(n_layers, N_EXPERTS, 1, d)
    gf = g_final.reshape(1, d)

    row_quant = SC_WINDOW * SC_WORKERS // TOP_K
    n_buf = -(-n_tot // row_quant) * row_quant
    n_a = (n_p // 2) // row_quant * row_quant
    parts = ((0, n_a, n_a), (n_a, n_buf - n_a, n_tot - n_a))

    def moe_scatter(l, h2, ti, part):
        row0, n_part, n_valid = part
        n_blocks = -(-(n_valid * TOP_K + N_EXPERTS * (bm - 1)) // bm)
        trash = n_blocks * bm
        dest, cnt = _rank_call(ti, row0, n_part, n_valid, trash, 512)
        counts = cnt[0, 0:N_EXPERTS].astype(I32)
        pad_end = jnp.cumsum((counts + bm - 1) // bm * bm)
        blk_e = jnp.minimum(jnp.sum((jnp.arange(n_blocks, dtype=I32)[:, None] * bm >= pad_end[None, :]).astype(I32),
                                    axis=1), N_EXPERTS - 1).astype(I32)
        n_used = (pad_end[-1] // bm).astype(I32).reshape(1)
        dest_km = dest[0:TOP_K]
        xs = _sc_scatter_rows(h2, dest_km, (n_blocks + 1) * bm, row0)
        return xs, blk_e, n_used, jnp.minimum(dest_km, trash - 1)

    def moe_experts(l, sc):
        xs, blk_e, n_used, src = sc
        ye = _ffn_call(l, blk_e, n_used, xs, w_gu, b_gu4, w_down, b_dn4)
        return _sc_gather_rows(ye, src.reshape(-1)).reshape(TOP_K, src.shape[1], d // 2)

    yp = x_prompt.reshape(n_p, d)
    ys = x_sample.reshape(bs, d)
    kv_p = (jnp.zeros((n_layers, bp, FOX_W, seq), F32), jnp.zeros((n_layers, bp, FOX_W, seq), F32),
            jnp.zeros((n_layers, bp, FOX_HEADS, seq), F32))
    outs_p = [[] for _ in range(3)]
    outs_s = [[] for _ in range(6)]
    for l in range(n_layers):
        inw, lw, ow = _layer_weights(l, w_in, b_forget, conv_w, conv_b, lru_lambda, lru_wa, lru_ba, lru_wx,
                                     lru_bx, gla_w2, gla_b2, gla_gnorm, w_out, w_router, b_router)
        g1 = g_norm1[l].reshape(1, d)
        g2 = g_norm2[l].reshape(1, d)

        kt_p, vt_p, lft_p, rz, qx, kx, vx = _inproj_call(yp, mod_p[l], g1, inw, seq, 256, l, n_layers, kv_bufs=kv_p)
        kv_p = (kt_p, vt_p, lft_p)
        fo = _fox_call(qx, kx, vx, bp, seq, 512)
        lo, conv_p, hlast_p = _lru_call(rz, lw, bp, seq, 256)
        go, st_p = _gla_call(rz, ow["gg"], bp, seq, 256)
        y1p, h2, ti, tw = _outproj_call(yp, fo, lo, go, mod_p[l], g2, ow["w_out"], ow["w_r"], ow["b_r"],
                                        256, seq, n_buf, 0)
        sc_a = moe_scatter(l, h2, ti, parts[0])
        st_p = st_p.reshape(bp, GLA_HEADS, HEAD_DIM, GLA_HEADS, HEAD_DIM)
        st_p = jnp.stack([st_p[:, hh, :, hh, :] for hh in range(GLA_HEADS)], axis=1).transpose(0, 1, 3, 2)
        outs_p[0].append(conv_p)
        outs_p[1].append(hlast_p.reshape(bp, LRU_W))
        outs_p[2].append(st_p)

        kts, vts, lfts, rzs, qs, ks, vs, logfs, gts = _inproj_call(
            ys, mod_s[l], g1, inw, bs, bs, 0, 1, sample=True)
        fos = _fox_dec_call(l, page_table, qs, ks, vs, logfs, ckt, cvt, cft)
        los, conv_s, h_s = _lru_step_call(l, rzs, conv_t, state_lru, lw)
        gost, s_s = _gla_step_call(l, gts, s_view, ow["ggc"])
        y1s, h2, ti, tw = _outproj_call(ys, fos, los, gost.T, mod_s[l], g2, ow["w_out"], ow["w_r"], ow["b_r"],
                                        bs, bs, n_buf, n_p, prev=(h2, ti, tw))
        outs_s[0].append(kts[0, 0])
        outs_s[1].append(vts[0, 0])
        outs_s[2].append(lfts[0, 0])
        outs_s[3].append(conv_s)
        outs_s[4].append(h_s)
        outs_s[5].append(s_s)

        sc_b = moe_scatter(l, h2, ti, parts[1])
        yk_a = moe_experts(l, sc_a)
        yk_b = moe_experts(l, sc_b)
        final = l == n_layers - 1
        yp = _combine_call(y1p, yk_a, tw, mod_p[l], gf, 256, seq, 0, n_a, 0, 0, final)
        yp = _combine_call(yp, yk_b, tw, mod_p[l], gf, 256, seq, n_a, n_p - n_a, 0, n_a, final)
        ys = _combine_call(y1s, yk_b, tw, mod_s[l], gf, bs, bs, 0, bs, n_p - n_a, n_p, final)

    kt_p, vt_p, lft_p = kv_p
    fox_k_p = kt_p.reshape(n_layers, bp, FOX_HEADS, HEAD_DIM, seq).transpose(0, 1, 4, 2, 3)
    fox_v_p = vt_p.reshape(n_layers, bp, FOX_HEADS, HEAD_DIM, seq).transpose(0, 1, 4, 2, 3)
    fox_f_p = lft_p.transpose(0, 1, 3, 2)
    fox_k_s = jnp.stack(outs_s[0]).reshape(n_layers, FOX_HEADS, HEAD_DIM, bs).transpose(0, 3, 1, 2)[:, :, None]
    fox_v_s = jnp.stack(outs_s[1]).reshape(n_layers, FOX_HEADS, HEAD_DIM, bs).transpose(0, 3, 1, 2)[:, :, None]
    fox_f_s = jnp.stack(outs_s[2]).transpose(0, 2, 1)[:, :, None]
    return (yp.reshape(bp, seq, d), ys.reshape(bs, 1, d),
            fox_k_p, fox_v_p, fox_f_p,
            jnp.stack(outs_p[0]), jnp.stack(outs_p[1]), jnp.stack(outs_p[2]),
            fox_k_s, fox_v_s, fox_f_s,
            jnp.stack(outs_s[3]).transpose(0, 2, 1, 3), jnp.stack(outs_s[4]),
            jnp.stack(outs_s[5]).transpose(0, 4, 1, 2, 3))
```

```python
import functools

import jax
import jax.numpy as jnp
from jax import lax
from jax.experimental import pallas as pl
from jax.experimental.pallas import tpu as pltpu
from jax.experimental.pallas import tpu_sc as plsc

F32 = jnp.float32
BF16 = jnp.bfloat16
I32 = jnp.int32

D_MODEL = 1024
HEAD_DIM = 64
FOX_W = 512
FOX_HEADS = 8
LRU_W = 256
GLA_W = 256
GLA_HEADS = 4
GLA_RANK = 16
GLA_CHUNK = 64
GLA_TAU = 16.0
LRU_C = 8.0
CONV_W = 4
N_EXPERTS = 32
TOP_K = 4
D_FF = 1024
SWIGLU_LIMIT = 7.0
SWIGLU_ALPHA = 1.702
NORM_EPS = 1e-6
PAGE_SIZE = 128
QK_SCALE = HEAD_DIM ** -0.5

_O_FQ, _O_FK, _O_FV, _O_FF = 0, 512, 1024, 1536
_O_LX, _O_LG, _O_GQ, _O_GK, _O_GV, _O_GA, _O_GOG = 1544, 1800, 2056, 2312, 2568, 2824, 2840
REC_W = 1792
GLA_T_ROWS = 5 * GLA_W
SMALL_W = 128
SMALL_T = 32
FOX_BLK = 2 * HEAD_DIM
FOX_QX = FOX_HEADS * FOX_BLK

VMEM_LIMIT = 56 * 1024 * 1024
MOE_BLOCK = 256
SC_WINDOW = 128
SC_COLS = 256
SC_WORKERS = 32

_NT = (((1,), (1,)), ((), ()))
_TN = (((0,), (0,)), ((), ()))


def _cparams(sem, vmem=VMEM_LIMIT):
    return pltpu.CompilerParams(dimension_semantics=sem, vmem_limit_bytes=vmem)


def _log_sigmoid(x):
    return jnp.minimum(x, 0.0) - jnp.log1p(jnp.exp(-jnp.abs(x)))


def _softplus(x):
    return jnp.maximum(x, 0.0) + jnp.log1p(jnp.exp(-jnp.abs(x)))


def _cumsum(x, axis):
    n = x.shape[axis]
    idx = lax.broadcasted_iota(I32, x.shape, axis)
    s = 1
    while s < n:
        x = x + jnp.where(idx >= s, pltpu.roll(x, s, axis), 0.0)
        s *= 2
    return x


_HI16 = -65536


def _pack_bf16_pairs(lo, hi):
    lo_bits = lax.bitcast_convert_type(lo.astype(BF16).astype(F32), I32)
    hi_bits = lax.bitcast_convert_type(hi.astype(BF16).astype(F32), I32)
    return lax.shift_right_logical(lo_bits, jnp.full(lo_bits.shape, 16, I32)) | (hi_bits & _HI16)


def _unpack_bf16_pairs(packed, dtype=BF16):
    lo = lax.bitcast_convert_type(lax.shift_left(packed, jnp.full(packed.shape, 16, I32)), F32)
    hi = lax.bitcast_convert_type(packed & _HI16, F32)
    return lo.astype(dtype), hi.astype(dtype)


def _mod_spec(mod, tm, tiles_per_seq, last_tile=None, tile_off=0):
    d = mod.shape[-1]
    clamp = (lambda i: i + tile_off) if last_tile is None else (lambda i: jnp.minimum(i, last_tile))
    if mod.shape[2] != 1:
        return pl.BlockSpec((6, None, tm, d), lambda i: (0, 0, clamp(i), 0))
    return pl.BlockSpec((6, None, 1, d), lambda i: (0, clamp(i) // tiles_per_seq, 0, 0))


def _ada_kernel(c_ref, w_ref, b_ref, o_ref):
    c = c_ref[...]
    a = (c * jax.nn.sigmoid(c)).astype(BF16)
    o_ref[...] = jnp.dot(a, w_ref[...].astype(BF16), preferred_element_type=F32) + b_ref[...]


def _ada_call(c_all, w_ada, b_ada):
    n_layers, d, w = w_ada.shape
    r = c_all.shape[0]
    tn = 1536
    return pl.pallas_call(
        _ada_kernel,
        grid=(n_layers, w // tn),
        in_specs=[
            pl.BlockSpec((r, d), lambda l, j: (0, 0)),
            pl.BlockSpec((None, d, tn), lambda l, j: (l, 0, j)),
            pl.BlockSpec((None, 1, tn), lambda l, j: (l, 0, j)),
        ],
        out_specs=pl.BlockSpec((None, r, tn), lambda l, j: (l, 0, j)),
        out_shape=jax.ShapeDtypeStruct((n_layers, r, w), F32),
        compiler_params=_cparams(("arbitrary", "arbitrary")),
        name="ada_mod",
    )(c_all, w_ada, b_ada.reshape(n_layers, 1, w))


def _inproj_kernel(*refs, tiles_per_seq, sample, n_alias):
    (y_ref, mod_ref, g_ref, wq_ref, wkvt_ref, wrec_ref, wsm_ref, wsmt_ref,
     bsm_ref, bfc_ref, w2_ref, b2_ref) = refs[:12]
    if sample:
        wgt_ref, w2t_ref, b2c_ref = refs[12:15]
        pos = 15 + n_alias
    else:
        eq_ref, oneq_ref, ek_ref, onek_ref = refs[12:16]
        pos = 16 + n_alias
    kt_ref, vt_ref, lft_ref, rz_ref = refs[pos:pos + 4]
    pos += 4
    if sample:
        q_ref, k_ref, v_ref, logf_ref, gt_ref = refs[pos:pos + 5]
        pos += 5
    else:
        qx_ref, kx_ref, vx_ref = refs[pos:pos + 3]
        pos += 3
    carry_c, carry_r = refs[pos:pos + 2]
    i = pl.program_id(0)

    @pl.when(i % tiles_per_seq == 0)
    def _():
        carry_c[...] = jnp.zeros_like(carry_c)
        carry_r[...] = jnp.zeros_like(carry_r)

    x = y_ref[...]
    xn = x * lax.rsqrt(jnp.mean(x * x, axis=-1, keepdims=True) + NORM_EPS) * g_ref[...]
    h = (xn * (1.0 + mod_ref[1]) + mod_ref[0]).astype(BF16)

    kt = lax.dot_general(wkvt_ref[0:FOX_W, :], h, _NT, preferred_element_type=F32)
    vt = lax.dot_general(wkvt_ref[FOX_W:2 * FOX_W, :], h, _NT, preferred_element_type=F32)
    kt_ref[...] = kt
    vt_ref[...] = vt
    rz_ref[:, 0:REC_W - GLA_W] = jnp.dot(h, wrec_ref[...], preferred_element_type=F32)

    sm = jnp.dot(h, wsm_ref[...], preferred_element_type=F32)
    lane = lax.broadcasted_iota(I32, sm.shape, 1)
    logf = jnp.where(lane < FOX_HEADS, _log_sigmoid(sm + bsm_ref[...]), 0.0)
    cum = _cumsum(logf, 0) + carry_c[...]
    carry_c[...] = cum[cum.shape[0] - 1:, :]
    glin =jnp.dot(sm.astype(BF16), w2_ref[...], preferred_element_type=F32) + b2_ref[...]
    rz_ref[:, REC_W - GLA_W:REC_W] = _log_sigmoid(glin) * (1.0 / GLA_TAU)

    smt = lax.dot_general(wsmt_ref[...], h, _NT, preferred_element_type=F32)
    lft = _log_sigmoid(smt[0:FOX_HEADS, :] + bfc_ref[...])
    lft_ref[...] = lft
    cumt = _cumsum(lft, 1) + carry_r[...]
    carry_r[...] = cumt[:, cumt.shape[1] - 1:]

    q = jnp.dot(h, wq_ref[...], preferred_element_type=F32) * QK_SCALE
    if not sample:
        def split3(c):
            hi = c.astype(BF16)
            r1 = c - hi.astype(F32)
            mid = r1.astype(BF16)
            lo = (r1 - mid.astype(F32)).astype(BF16)
            return [hi, mid, lo]

        cq = jnp.concatenate(split3(cum), axis=1)
        qx_ref[...] = (q + jnp.dot(cq, eq_ref[...], preferred_element_type=F32) + oneq_ref[...]).astype(BF16)
        ck = jnp.concatenate(split3(cumt) + [jnp.zeros(cumt.shape, BF16)], axis=0)
        kb = jnp.dot(ek_ref[...], ck, preferred_element_type=F32) + onek_ref[...]
        parts = []
        for hh in range(FOX_HEADS):
            parts += [kt[hh * HEAD_DIM:(hh + 1) * HEAD_DIM, :], kb[hh * HEAD_DIM:(hh + 1) * HEAD_DIM, :]]
        kx_ref[...] = jnp.concatenate(parts, axis=0).astype(BF16)
        vx_ref[...] = vt.astype(BF16)

    if sample:
        q_ref[...] = q.astype(BF16)
        k_ref[...] = kt.T
        v_ref[...] = vt.T
        logf_ref[...] = logf[:, 0:FOX_HEADS]
        gt_ref[0:4 * GLA_W, :] = lax.dot_general(wgt_ref[...], h, _NT, preferred_element_type=F32)
        glt = jnp.dot(w2t_ref[...], smt.astype(BF16), preferred_element_type=F32) + b2c_ref[...]
        gt_ref[4 * GLA_W:GLA_T_ROWS, :] = _log_sigmoid(glt) * (1.0 / GLA_TAU)


def _inproj_call(y, mod, g, wts, seq_len, tm, layer, n_layers, kv_bufs=None, sample=False):
    n, d = y.shape
    tiles_per_seq = seq_len // tm
    n_seq = n // seq_len
    qw = FOX_W if sample else FOX_QX
    const = lambda i: (0, 0)
    row = lambda i: (i, 0)
    seq_t = lambda i: (layer, i // tiles_per_seq, 0, i % tiles_per_seq)
    seq_t3 = lambda i: (i // tiles_per_seq, 0, i % tiles_per_seq)
    in_specs = [
        pl.BlockSpec((tm, d), row),
        _mod_spec(mod, tm, tiles_per_seq),
        pl.BlockSpec((1, d), const),
        pl.BlockSpec((d, qw), const),
        pl.BlockSpec((2 * FOX_W, d), const),
        pl.BlockSpec((d, REC_W - GLA_W), const),
        pl.BlockSpec((d, SMALL_W), const),
        pl.BlockSpec((SMALL_T, d), const),
        pl.BlockSpec((1, SMALL_W), const),
        pl.BlockSpec((FOX_HEADS, 1), const),
        pl.BlockSpec((SMALL_W, GLA_W), const),
        pl.BlockSpec((1, GLA_W), const),
    ]
    args = [y, mod, g, wts["w_q" if sample else "w_qx"], wts["w_kvt"], wts["w_rec"], wts["w_sm"], wts["w_smt"],
            wts["b_sm"], wts["b_fc"], wts["w2"], wts["b2"]]
    if sample:
        in_specs += [pl.BlockSpec((4 * GLA_W, d), const), pl.BlockSpec((GLA_W, SMALL_T), const),
                     pl.BlockSpec((GLA_W, 1), const)]
        args += [wts["w_gt"], wts["w2t"], wts["b2c"]]
    else:
        in_specs += [pl.BlockSpec((3 * SMALL_W, FOX_QX), const), pl.BlockSpec((1, FOX_QX), const),
                     pl.BlockSpec((FOX_W, SMALL_T), const), pl.BlockSpec((FOX_W, 1), const)]
        args += list(_bias_fold_consts())
    aliases = {}
    n_alias = 0
    if kv_bufs is not None:
        n_alias = 3
        first = len(args)
        in_specs += [pl.BlockSpec(memory_space=pl.ANY)] * 3
        args += list(kv_bufs)
        aliases = {first: 0, first + 1: 1, first + 2: 2}
    out_specs = [
        pl.BlockSpec((None, None, FOX_W, tm), seq_t),
        pl.BlockSpec((None, None, FOX_W, tm), seq_t),
        pl.BlockSpec((None, None, FOX_HEADS, tm), seq_t),
        pl.BlockSpec((tm, REC_W), row),
    ]
    out_shape = [
        jax.ShapeDtypeStruct((n_layers, n_seq, FOX_W, seq_len), F32),
        jax.ShapeDtypeStruct((n_layers, n_seq, FOX_W, seq_len), F32),
        jax.ShapeDtypeStruct((n_layers, n_seq, FOX_HEADS, seq_len), F32),
        jax.ShapeDtypeStruct((n, REC_W), F32),
    ]
    if sample:
        out_specs += [pl.BlockSpec((tm, FOX_W), row), pl.BlockSpec((tm, FOX_W), row), pl.BlockSpec((tm, FOX_W), row),
                      pl.BlockSpec((tm, FOX_HEADS), row), pl.BlockSpec((GLA_T_ROWS, tm), lambda i: (0, i))]
        out_shape += [jax.ShapeDtypeStruct((n, FOX_W), BF16), jax.ShapeDtypeStruct((n, FOX_W), F32),
                      jax.ShapeDtypeStruct((n, FOX_W), F32), jax.ShapeDtypeStruct((n, FOX_HEADS), F32),
                      jax.ShapeDtypeStruct((GLA_T_ROWS, n), F32)]
    else:
        out_specs += [pl.BlockSpec((tm, FOX_QX), row), pl.BlockSpec((None, FOX_QX, tm), seq_t3),
                      pl.BlockSpec((None, FOX_W, tm), seq_t3)]
        out_shape += [jax.ShapeDtypeStruct((n, FOX_QX), BF16), jax.ShapeDtypeStruct((n_seq, FOX_QX, seq_len), BF16),
                      jax.ShapeDtypeStruct((n_seq, FOX_W, seq_len), BF16)]
    return pl.pallas_call(
        functools.partial(_inproj_kernel, tiles_per_seq=tiles_per_seq, sample=sample, n_alias=n_alias),
        grid=(n // tm,),
        in_specs=in_specs,
        out_specs=tuple(out_specs),
        out_shape=tuple(out_shape),
        input_output_aliases=aliases,
        scratch_shapes=[pltpu.VMEM((1, SMALL_W), F32), pltpu.VMEM((FOX_HEADS, 1), F32)],
        compiler_params=_cparams(("arbitrary",)),
        name="in_proj",
    )(*args)


def _bias_fold_consts():
    h = jnp.arange(FOX_HEADS)
    eq = jnp.zeros((3 * SMALL_W, FOX_QX), F32)
    ek = jnp.zeros((FOX_W, SMALL_T), F32)
    oneq = jnp.zeros((1, FOX_QX), F32)
    onek = jnp.zeros((FOX_W, 1), F32)
    for piece in range(3):
        eq = eq.at[piece * SMALL_W + h, h * FOX_BLK + HEAD_DIM + piece].set(1.0)
        oneq = oneq.at[0, h * FOX_BLK + HEAD_DIM + 3 + piece].set(1.0)
        onek = onek.at[h * HEAD_DIM + piece, 0].set(1.0)
        ek = ek.at[h * HEAD_DIM + 3 + piece, piece * FOX_HEADS + h].set(-1.0)
    return eq.astype(BF16), oneq, ek.astype(BF16), onek


def _fox_kernel(qi_ref, ki_ref, qx_ref, kx_ref, vx_ref, o_ref, m_sc, l_sc, acc_sc, *, tq, tk):
    qi = qi_ref[pl.program_id(2)]
    ki = ki_ref[pl.program_id(2)]
    w = 2 * HEAD_DIM
    nc = tk // w

    @pl.when(ki == 0)
    def _():
        m_sc[...] = jnp.full_like(m_sc, -jnp.inf)
        l_sc[...] = jnp.zeros_like(l_sc)
        acc_sc[...] = jnp.zeros_like(acc_sc)

    def step(masked):
        vt = vx_ref[...]
        if masked:
            causal = lax.broadcasted_iota(I32, (tq, tk), 1) <= lax.broadcasted_iota(I32, (tq, tk), 0)
        for j in range(2):
            s = jnp.dot(qx_ref[:, j * FOX_BLK:(j + 1) * FOX_BLK], kx_ref[j * FOX_BLK:(j + 1) * FOX_BLK, :],
                        preferred_element_type=F32)
            if masked:
                s = jnp.where(causal, s, -jnp.inf)
            sc = [s[:, c * w:(c + 1) * w] for c in range(nc)]
            mb = sc[0]
            for c in range(1, nc):
                mb = jnp.maximum(mb, sc[c])
            m_prev = m_sc[j]
            m_new = jnp.maximum(m_prev, jnp.broadcast_to(jnp.max(mb, axis=1, keepdims=True), (tq, w)))
            ps = [jnp.exp(sc[c] - m_new) for c in range(nc)]
            lsum = ps[0]
            for c in range(1, nc):
                lsum = lsum + ps[c]
            alpha = jnp.exp(m_prev - m_new)
            l_sc[j] = alpha * l_sc[j] + jnp.broadcast_to(jnp.sum(lsum, axis=1, keepdims=True), (tq, w))
            p = jnp.concatenate([pc.astype(BF16) for pc in ps], axis=1)
            acc_sc[j] = alpha * acc_sc[j] + lax.dot_general(p, vt, _NT, preferred_element_type=F32)
            m_sc[j] = m_new

    @pl.when(ki < qi)
    def _():
        step(False)

    @pl.when(ki == qi)
    def _():
        step(True)
        lane = lax.broadcasted_iota(I32, (tq, w), 1)
        o0 = acc_sc[0] / l_sc[0]
        o1 = acc_sc[1] / l_sc[1]
        o_ref[...] = jnp.where(lane < HEAD_DIM, o0, o1).astype(BF16)


def _fox_call(qx, kx, vx, n_seq, seq_len, tq):
    n = qx.shape[0]
    nt = seq_len // tq
    tk = tq
    hp = FOX_HEADS // 2
    w = 2 * HEAD_DIM
    pairs = [(i, j) for i in range(nt) for j in range(i + 1)]
    qi_tab = jnp.asarray([p[0] for p in pairs], I32)
    ki_tab = jnp.asarray([p[1] for p in pairs], I32)
    past = lambda b, h, s, qt, kt: (b, h, kt[s])
    grid_spec = pltpu.PrefetchScalarGridSpec(
        num_scalar_prefetch=2,
        grid=(n_seq, hp, len(pairs)),
        in_specs=[
            pl.BlockSpec((tq, 2 * FOX_BLK), lambda b, h, s, qt, kt: (b * nt + qt[s], h)),
            pl.BlockSpec((None, 2 * FOX_BLK, tk), past),
            pl.BlockSpec((None, w, tk), past),
        ],
        out_specs=pl.BlockSpec((tq, w), lambda b, h, s, qt, kt: (b * nt + qt[s], h)),
        scratch_shapes=[pltpu.VMEM((2, tq, w), F32), pltpu.VMEM((2, tq, w), F32), pltpu.VMEM((2, tq, w), F32)],
    )
    return pl.pallas_call(
        functools.partial(_fox_kernel, tq=tq, tk=tk),
        grid_spec=grid_spec,
        out_shape=jax.ShapeDtypeStruct((n, FOX_W), BF16),
        compiler_params=_cparams(("arbitrary", "arbitrary", "arbitrary")),
        name="fox_prompt",
    )(qi_tab, ki_tab, qx, kx, vx)


def _lru_gates(xc, wa_ref, ba_ref, wx_ref, bx_ref, lam_ref):
    xb = xc.astype(BF16)
    r = jax.nn.sigmoid(jnp.dot(xb, wa_ref[...], preferred_element_type=F32) + ba_ref[...])
    gi = jax.nn.sigmoid(jnp.dot(xb, wx_ref[...], preferred_element_type=F32) + bx_ref[...])
    log_a = -LRU_C * r * _softplus(-lam_ref[...])
    a = jnp.exp(log_a)
    mult = jnp.sqrt(-jnp.tanh(log_a) * (a * a + 1.0))
    return a, mult, gi


def _lru_kernel(lx_ref, lg_ref, cw_ref, cb_ref, wa_ref, ba_ref, wx_ref, bx_ref, lam_ref,
                lo_ref, conv_ref, hlast_ref, xbuf, hcar, *, tt):
    ti = pl.program_id(1)
    nt = pl.num_programs(1)

    @pl.when(ti == 0)
    def _():
        xbuf[0:8, :] = jnp.zeros((8, LRU_W), F32)
        hcar[...] = jnp.zeros_like(hcar)

    x = lx_ref[...]
    xbuf[8:8 + tt, :] = x
    xc = cb_ref[...] + cw_ref[3:4, :] * x
    for j in range(CONV_W - 1):
        xc = xc + cw_ref[j:j + 1, :] * xbuf[5 + j:5 + j + tt, :]
    xbuf[0:8, :] = x[tt - 8:tt, :]

    a, mult, gi = _lru_gates(xc, wa_ref, ba_ref, wx_ref, bx_ref, lam_ref)
    row = lax.broadcasted_iota(I32, (tt, LRU_W), 0)
    mult = jnp.where((row == 0) & (ti == 0), 1.0, mult)
    b = mult * gi * xc
    s = 1
    while s < tt:
        keep = row >= s
        a_sh = jnp.where(keep, pltpu.roll(a, s, 0), 1.0)
        b_sh = jnp.where(keep, pltpu.roll(b, s, 0), 0.0)
        b = a * b_sh + b
        a = a * a_sh
        s *= 2
    h = a * hcar[...] + b
    hcar[...] = h[tt - 1:tt, :]
    lo_ref[...] = (h * jax.nn.gelu(lg_ref[...])).astype(BF16)

    @pl.when(ti == nt - 1)
    def _():
        conv_ref[...] = x[tt - (CONV_W - 1):tt, :]
        hlast_ref[...] = h[tt - 1:tt, :]


def _lru_weight_specs(const):
    return [
        pl.BlockSpec((CONV_W, LRU_W), const), pl.BlockSpec((1, LRU_W), const),
        pl.BlockSpec((LRU_W, LRU_W), const), pl.BlockSpec((1, LRU_W), const),
        pl.BlockSpec((LRU_W, LRU_W), const), pl.BlockSpec((1, LRU_W), const),
        pl.BlockSpec((1, LRU_W), const),
    ]


def _lru_weight_args(lw):
    return [lw["conv_w"], lw["conv_b"], lw["wa"], lw["ba"], lw["wx"], lw["bx"], lw["lam"]]


def _lru_call(rz, lw, n_seq, seq_len, tt):
    n = rz.shape[0]
    nt = seq_len // tt
    return pl.pallas_call(
        functools.partial(_lru_kernel, tt=tt),
        grid=(n_seq, nt),
        in_specs=[
            pl.BlockSpec((tt, LRU_W), lambda b, t: (b * nt + t, 0)),
            pl.BlockSpec((tt, LRU_W), lambda b, t: (b * nt + t, 1)),
        ] + _lru_weight_specs(lambda b, t: (0, 0)),
        out_specs=(
            pl.BlockSpec((tt, LRU_W), lambda b, t: (b * nt + t, 0)),
            pl.BlockSpec((None, CONV_W - 1, LRU_W), lambda b, t: (b, 0, 0)),
            pl.BlockSpec((None, 1, LRU_W), lambda b, t: (b, 0, 0)),
        ),
        out_shape=(
            jax.ShapeDtypeStruct((n, LRU_W), BF16),
            jax.ShapeDtypeStruct((n_seq, CONV_W - 1, LRU_W), F32),
            jax.ShapeDtypeStruct((n_seq, 1, LRU_W), F32),
        ),
        scratch_shapes=[pltpu.VMEM((tt + 8, LRU_W), F32), pltpu.VMEM((1, LRU_W), F32)],
        compiler_params=_cparams(("arbitrary", "arbitrary")),
        name="lru_prompt",
    )(rz, rz, *_lru_weight_args(lw))


def _head_rms_gate(o, gg_ref, gog):
    lane = lax.broadcasted_iota(I32, o.shape, 1)
    o2 = o * o
    rs = jnp.zeros_like(o)
    for hh in range(GLA_HEADS):
        mh = lane // HEAD_DIM == hh
        ms = jnp.sum(jnp.where(mh, o2, 0.0), axis=1, keepdims=True) * (1.0 / HEAD_DIM)
        rs = jnp.where(mh, lax.rsqrt(ms + NORM_EPS), rs)
    return o * rs * gg_ref[...] * (gog * jax.nn.sigmoid(gog))


def _gla_kernel(gq_ref, gk_ref, gv_ref, gog_ref, gl_ref, gg_ref, go_ref, st_ref, s_sc, *, tt):
    ti = pl.program_id(1)
    nt = pl.num_programs(1)
    c = GLA_CHUNK

    @pl.when(ti == 0)
    def _():
        s_sc[...] = jnp.zeros_like(s_sc)

    lane = lax.broadcasted_iota(I32, (c, GLA_W), 1)
    r2 = lax.broadcasted_iota(I32, (GLA_W, GLA_W), 0)
    c2 = lax.broadcasted_iota(I32, (GLA_W, GLA_W), 1)
    same_head = (r2 // HEAD_DIM) == (c2 // HEAD_DIM)
    tril = lax.broadcasted_iota(I32, (c, c), 1) <= lax.broadcasted_iota(I32, (c, c), 0)

    for ci in range(tt // c):
        sl = slice(ci * c, (ci + 1) * c)
        q = gq_ref[sl, :] * QK_SCALE
        k = gk_ref[sl, :]
        v = gv_ref[sl, :].astype(BF16)
        bc = _cumsum(gl_ref[sl, :], 0)
        b_last = bc[c - 1:c, :]
        qd = (q * jnp.exp(bc)).astype(BF16)
        kinv = (k * jnp.exp(-bc)).astype(BF16)
        kdec = (k * jnp.exp(b_last - bc)).astype(BF16)
        s_prev = s_sc[...]
        o = lax.dot_general(qd, s_prev.astype(BF16), _NT, preferred_element_type=F32)
        for hh in range(GLA_HEADS):
            mh = lane // HEAD_DIM == hh
            att = lax.dot_general(jnp.where(mh, qd, jnp.zeros_like(qd)), kinv, _NT, preferred_element_type=F32)
            att = jnp.where(tril, att, 0.0).astype(BF16)
            o = o + jnp.dot(att, jnp.where(mh, v, jnp.zeros_like(v)), preferred_element_type=F32)
        ut = lax.dot_general(v, kdec, _TN, preferred_element_type=F32)
        s_sc[...] = s_prev * jnp.exp(b_last) + jnp.where(same_head, ut, 0.0)
        go_ref[sl, :] = _head_rms_gate(o, gg_ref, gog_ref[sl, :]).astype(BF16)

    @pl.when(ti == nt - 1)
    def _():
        st_ref[...] = s_sc[...]


def _gla_call(rz, gg, n_seq, seq_len, tt):
    n = rz.shape[0]
    nt = seq_len // tt

    def col(j):
        return pl.BlockSpec((tt, GLA_W), lambda b, t: (b * nt + t, j))

    return pl.pallas_call(
        functools.partial(_gla_kernel, tt=tt),
        grid=(n_seq, nt),
        in_specs=[col(2), col(3), col(4), col(5), col(6), pl.BlockSpec((1, GLA_W), lambda b, t: (0, 0))],
        out_specs=(
            pl.BlockSpec((tt, GLA_W), lambda b, t: (b * nt + t, 0)),
            pl.BlockSpec((None, GLA_W, GLA_W), lambda b, t: (b, 0, 0)),
        ),
        out_shape=(
            jax.ShapeDtypeStruct((n, GLA_W), BF16),
            jax.ShapeDtypeStruct((n_seq, GLA_W, GLA_W), F32),
        ),
        scratch_shapes=[pltpu.VMEM((GLA_W, GLA_W), F32)],
        compiler_params=_cparams(("arbitrary", "arbitrary")),
        name="gla_prompt",
    )(rz, rz, rz, rz, rz, gg)


def _fox_dec_kernel(pt_ref, q_ref, kn_ref, vn_ref, dn_ref, *refs, n_pages):
    del pt_ref
    k_refs = refs[0:n_pages]
    v_refs = refs[n_pages:2 * n_pages]
    f_refs = refs[2 * n_pages:3 * n_pages]
    o_ref = refs[3 * n_pages]
    w = FOX_W
    hrow = lax.broadcasted_iota(I32, (FOX_HEADS, w), 0)
    hlane = lax.broadcasted_iota(I32, (FOX_HEADS, w), 1) // HEAD_DIM
    diag = hrow == hlane
    q = q_ref[...].astype(F32)
    qbd = jnp.where(diag, jnp.broadcast_to(q, (FOX_HEADS, w)), 0.0).astype(BF16)
    s = jnp.concatenate(
        [jnp.dot(qbd, k_refs[p][...].astype(BF16), preferred_element_type=F32) for p in range(n_pages)], axis=1)
    lf = jnp.concatenate([f_refs[p][...] for p in range(n_pages)], axis=1)
    cs = _cumsum(lf, 1)
    suffix = cs[:, cs.shape[1] - 1:] - cs
    s = s + dn_ref[...] + suffix
    s_new = jnp.sum(qbd.astype(F32) * kn_ref[...], axis=1, keepdims=True)
    m = jnp.maximum(jnp.max(s, axis=1, keepdims=True), s_new)
    p_past = jnp.exp(s - m)
    p_new = jnp.exp(s_new - m)
    denom = jnp.sum(p_past, axis=1, keepdims=True) + p_new
    acc = p_new * vn_ref[...]
    pb = p_past.astype(BF16)
    for p in range(n_pages):
        acc = acc + lax.dot_general(pb[:, p * PAGE_SIZE:(p + 1) * PAGE_SIZE], v_refs[p][...].astype(BF16), _NT,
                                    preferred_element_type=F32)
    out = jnp.where(diag, acc / denom, 0.0)
    o_ref[...] = jnp.sum(out, axis=0, keepdims=True).astype(BF16)


def _fox_dec_call(layer, page_table, q, k_new, v_new, logf_new, cache_kt, cache_vt, cache_ft):
    bd, n_pages = page_table.shape
    w = FOX_W

    def page_spec(rows, j):
        return pl.BlockSpec((None, None, rows, PAGE_SIZE), lambda b, pt, j=j: (layer, pt[b, j], 0, 0))

    row = lambda b, pt: (b, 0, 0)
    in_specs = [
        pl.BlockSpec((None, 1, w), row),
        pl.BlockSpec((None, 1, w), row),
        pl.BlockSpec((None, 1, w), row),
        pl.BlockSpec((None, FOX_HEADS, 1), row),
    ]
    in_specs += [page_spec(w, j) for j in range(n_pages)]
    in_specs += [page_spec(w, j) for j in range(n_pages)]
    in_specs += [page_spec(FOX_HEADS, j) for j in range(n_pages)]
    grid_spec = pltpu.PrefetchScalarGridSpec(
        num_scalar_prefetch=1,
        grid=(bd,),
        in_specs=in_specs,
        out_specs=pl.BlockSpec((None, 1, w), row),
    )
    out = pl.pallas_call(
        functools.partial(_fox_dec_kernel, n_pages=n_pages),
        grid_spec=grid_spec,
        out_shape=jax.ShapeDtypeStruct((bd, 1, w), BF16),
        compiler_params=_cparams(("arbitrary",)),
        name="fox_sample",
    )(page_table, q.reshape(bd, 1, w), k_new.reshape(bd, 1, w), v_new.reshape(bd, 1, w),
      logf_new.reshape(bd, FOX_HEADS, 1),
      *([cache_kt] * n_pages), *([cache_vt] * n_pages), *([cache_ft] * n_pages))
    return out.reshape(bd, w)


def _lru_step_kernel(lx_ref, lg_ref, conv_ref, h0_ref, cw_ref, cb_ref, wa_ref, ba_ref, wx_ref, bx_ref, lam_ref,
                     lo_ref, convn_ref, hn_ref):
    x = lx_ref[...]
    xc = cb_ref[...] + cw_ref[3:4, :] * x
    for j in range(CONV_W - 1):
        xc = xc + cw_ref[j:j + 1, :] * conv_ref[j]
    convn_ref[0] = conv_ref[1]
    convn_ref[1] = conv_ref[2]
    convn_ref[2] = x
    a, mult, gi = _lru_gates(xc, wa_ref, ba_ref, wx_ref, bx_ref, lam_ref)
    h = a * h0_ref[...] + mult * gi * xc
    hn_ref[...] = h
    lo_ref[...] = (h * jax.nn.gelu(lg_ref[...])).astype(BF16)


def _lru_step_call(layer, rz, conv_t, h0, lw):
    bd = rz.shape[0]
    return pl.pallas_call(
        _lru_step_kernel,
        grid=(1,),
        in_specs=[
            pl.BlockSpec((bd, LRU_W), lambda i: (0, 0)),
            pl.BlockSpec((bd, LRU_W), lambda i: (0, 1)),
            pl.BlockSpec((None, CONV_W - 1, bd, LRU_W), lambda i: (layer, 0, 0, 0)),
            pl.BlockSpec((None, bd, LRU_W), lambda i: (layer, 0, 0)),
        ] + _lru_weight_specs(lambda i: (0, 0)),
        out_specs=(
            pl.BlockSpec((bd, LRU_W), lambda i: (0, 0)),
            pl.BlockSpec((CONV_W - 1, bd, LRU_W), lambda i: (0, 0, 0)),
            pl.BlockSpec((bd, LRU_W), lambda i: (0, 0)),
        ),
        out_shape=(
            jax.ShapeDtypeStruct((bd, LRU_W), BF16),
            jax.ShapeDtypeStruct((CONV_W - 1, bd, LRU_W), F32),
            jax.ShapeDtypeStruct((bd, LRU_W), F32),
        ),
        compiler_params=_cparams(("arbitrary",)),
        name="lru_step",
    )(rz, rz, conv_t, h0, *_lru_weight_args(lw))


def _gla_step_kernel(q_ref, k_ref, v_ref, gog_ref, gl_ref, gg_ref, s_ref, go_ref, sn_ref):
    eg = jnp.exp(gl_ref[...])
    kt = k_ref[...]
    qt = q_ref[...] * QK_SCALE
    vt = v_ref[...]
    o = jnp.zeros_like(vt)
    for kk in range(HEAD_DIM):
        s_new = eg[kk:kk + 1, :] * s_ref[kk] + kt[kk:kk + 1, :] * vt
        sn_ref[kk] = s_new
        o = o + qt[kk:kk + 1, :] * s_new
    ms = jnp.mean(o * o, axis=0, keepdims=True)
    gog = gog_ref[...]
    go_ref[...] = o * lax.rsqrt(ms + NORM_EPS) * gg_ref[...] * (gog * jax.nn.sigmoid(gog))


def _gla_step_call(layer, gt, s_view, ggc):
    bd = gt.shape[1]
    hd = HEAD_DIM

    def part(j):
        return pl.BlockSpec((hd, bd), lambda h, j=j: (j * GLA_HEADS + h, 0))

    return pl.pallas_call(
        _gla_step_kernel,
        grid=(GLA_HEADS,),
        in_specs=[part(0), part(1), part(2), part(3), part(4),
                  pl.BlockSpec((hd, 1), lambda h: (h, 0)),
                  pl.BlockSpec((None, None, hd, hd, bd), lambda h: (layer, h, 0, 0, 0))],
        out_specs=(
            pl.BlockSpec((hd, bd), lambda h: (h, 0)),
            pl.BlockSpec((None, hd, hd, bd), lambda h: (h, 0, 0, 0)),
        ),
        out_shape=(
            jax.ShapeDtypeStruct((GLA_W, bd), F32),
            jax.ShapeDtypeStruct((GLA_HEADS, hd, hd, bd), F32),
        ),
        compiler_params=_cparams(("arbitrary",)),
        name="gla_step",
    )(gt, gt, gt, gt, gt, ggc, s_view)


def _outproj_kernel(*refs, n_alias, n_tiles):
    h2_ref, ti_ref, tw_ref = refs[10 + n_alias:13 + n_alias]
    i = pl.program_id(0)

    @pl.when(i < n_tiles)
    def _():
        _outproj_tile(*refs[0:9], *refs[9 + n_alias:13 + n_alias])

    @pl.when(i >= n_tiles)
    def _():
        h2_ref[...] = jnp.zeros_like(h2_ref)
        ti_ref[...] = jnp.zeros_like(ti_ref)
        tw_ref[...] = jnp.zeros_like(tw_ref)


def _outproj_tile(y_ref, fo_ref, lo_ref, go_ref, mod_ref, g2_ref, wo_ref, wr_ref, br_ref,
                  y1_ref, h2_ref, ti_ref, tw_ref):
    m = jnp.dot(fo_ref[...], wo_ref[0:FOX_W, :], preferred_element_type=F32)
    m = m + jnp.dot(lo_ref[...], wo_ref[FOX_W:FOX_W + LRU_W, :], preferred_element_type=F32)
    m = m + jnp.dot(go_ref[...].astype(BF16), wo_ref[FOX_W + LRU_W:, :], preferred_element_type=F32)
    y1 = y_ref[...] + mod_ref[2] * m
    y1_ref[...] = y1
    xn = y1 * lax.rsqrt(jnp.mean(y1 * y1, axis=-1, keepdims=True) + NORM_EPS) * g2_ref[...]
    h2 = xn * (1.0 + mod_ref[4]) + mod_ref[3]
    half = h2.shape[1] // 2
    h2_ref[...] = _pack_bf16_pairs(h2[:, 0:half], h2[:, half:])
    logits = jnp.dot(h2.astype(BF16), wr_ref[...], preferred_element_type=F32) + br_ref[...]
    lane = lax.broadcasted_iota(I32, logits.shape, 1)
    logits = jnp.where(lane < N_EXPERTS, logits, -jnp.inf)
    idx_out = jnp.zeros(logits.shape, I32)
    val_out = jnp.zeros(logits.shape, F32)
    vals = []
    for kk in range(TOP_K):
        mx = jnp.max(logits, axis=1, keepdims=True)
        sel = jnp.min(jnp.where(logits == mx, lane, SMALL_W), axis=1, keepdims=True)
        idx_out = jnp.where(lane == kk, sel, idx_out)
        vals.append(mx)
        logits = jnp.where(lane == sel, -jnp.inf, logits)
    es = [jnp.exp(vv - vals[0]) for vv in vals]
    tot = es[0] + es[1] + es[2] + es[3]
    for kk in range(TOP_K):
        val_out = jnp.where(lane == kk, es[kk] / tot, val_out)
    ti_ref[...] = idx_out
    tw_ref[...] = val_out


def _outproj_call(y, fo, lo, go, mod, g2, w_out, w_r, b_r, tm, seq_len, n_buf, row_off, prev=None):
    n, d = y.shape
    tiles_per_seq = max(seq_len // tm, 1)
    n_tiles = n // tm
    n_steps = n_tiles if prev is not None else n_buf // tm
    const = lambda i: (0, 0)
    row = lambda i: (jnp.minimum(i, n_tiles - 1), 0)
    off = row_off // tm
    orow = lambda i: (i + off, 0)
    in_specs = [
        pl.BlockSpec((tm, d), row),
        pl.BlockSpec((tm, FOX_W), row),
        pl.BlockSpec((tm, LRU_W), row),
        pl.BlockSpec((tm, GLA_W), row),
        _mod_spec(mod, tm, tiles_per_seq, n_tiles - 1),
        pl.BlockSpec((1, d), const),
        pl.BlockSpec((d, d), const),
        pl.BlockSpec((d, SMALL_W), const),
        pl.BlockSpec((1, SMALL_W), const),
    ]
    args = [y, fo, lo, go, mod, g2, w_out, w_r, b_r]
    aliases = {}
    if prev is not None:
        in_specs += [pl.BlockSpec(memory_space=pl.ANY)] * 3
        args += list(prev)
        aliases = {9: 1, 10: 2, 11: 3}
    return pl.pallas_call(
        functools.partial(_outproj_kernel, n_alias=0 if prev is None else 3, n_tiles=n_tiles),
        grid=(n_steps,),
        in_specs=in_specs,
        out_specs=(
            pl.BlockSpec((tm, d), row),
            pl.BlockSpec((tm, d // 2), orow),
            pl.BlockSpec((tm, SMALL_W), orow),
            pl.BlockSpec((tm, SMALL_W), orow),
        ),
        out_shape=(
            jax.ShapeDtypeStruct((n, d), F32),
            jax.ShapeDtypeStruct((n_buf, d // 2), I32),
            jax.ShapeDtypeStruct((n_buf, SMALL_W), I32),
            jax.ShapeDtypeStruct((n_buf, SMALL_W), F32),
        ),
        input_output_aliases=aliases,
        compiler_params=_cparams(("arbitrary",)),
        name="out_proj_router",
    )(*args)


def _rank_kernel(ti_ref, dest_ref, cnt_ref, carry, *, tm, n_valid, trash):
    p = pl.program_id(0)
    i = pl.program_id(1)
    nt = pl.num_programs(1)

    @pl.when((p == 0) & (i == 0))
    def _():
        carry[...] = jnp.zeros_like(carry)

    @pl.when((p == 1) & (i == 0))
    def _():
        cnt = carry[...]
        cnt_ref[...] = cnt
        padded = jnp.floor((cnt + (MOE_BLOCK - 1.0)) * (1.0 / MOE_BLOCK)) * MOE_BLOCK
        carry[...] = _cumsum(padded, 1) - padded

    t = ti_ref[...]
    lane = lax.broadcasted_iota(I32, (tm, SMALL_W), 1)
    valid = (lax.broadcasted_iota(I32, (tm, 1), 0) + i * tm) < n_valid
    ohs = [jnp.where(valid, (lane == t[:, kk:kk + 1]).astype(F32), 0.0) for kk in range(TOP_K)]

    @pl.when(p == 0)
    def _():
        carry[...] = carry[...] + jnp.sum(sum(ohs[1:], ohs[0]), axis=0, keepdims=True)

    @pl.when(p == 1)
    def _():
        before = (lax.broadcasted_iota(I32, (tm, tm), 1) < lax.broadcasted_iota(I32, (tm, tm), 0)).astype(BF16)
        base = carry[...]
        out = jnp.zeros((tm, SMALL_W), F32)
        for kk in range(TOP_K):
            oh = ohs[kk]
            pre = jnp.dot(before, oh.astype(BF16), preferred_element_type=F32)
            slot = jnp.sum(oh * (pre + base), axis=1, keepdims=True)
            out = jnp.where(lane == kk, jnp.where(valid, slot, float(trash)), out)
            base = base + jnp.sum(oh, axis=0, keepdims=True)
        carry[...] = base
        dest_ref[...] = out.T[0:8, :].astype(I32)


def _rank_call(ti, row0, n_rows, n_valid, trash, tm):
    off = row0 // tm
    return pl.pallas_call(
        functools.partial(_rank_kernel, tm=tm, n_valid=n_valid, trash=trash),
        grid=(2, n_rows // tm),
        in_specs=[pl.BlockSpec((tm, SMALL_W), lambda p, i: (i + off, 0))],
        out_specs=(
            pl.BlockSpec((8, tm), lambda p, i: (0, i * p)),
            pl.BlockSpec((1, SMALL_W), lambda p, i: (0, 0)),
        ),
        out_shape=(
            jax.ShapeDtypeStruct((8, n_rows), I32),
            jax.ShapeDtypeStruct((1, SMALL_W), F32),
        ),
        scratch_shapes=[pltpu.VMEM((1, SMALL_W), F32)],
        compiler_params=_cparams(("arbitrary", "arbitrary")),
        name="moe_rank",
    )(ti)


def _sc_mesh():
    return plsc.VectorSubcoreMesh(core_axis_name="core", subcore_axis_name="subcore")


def _sc_scatter_rows(x, dest_km, n_out, row0=0):
    kk, n = dest_km.shape
    d = x.shape[1]
    nb = n // SC_WINDOW
    off = row0 // SC_WINDOW
    assert (kk * nb) % SC_WORKERS == 0 and d % SC_COLS == 0 and row0 % SC_WINDOW == 0

    @pl.kernel(out_type=jax.ShapeDtypeStruct((n_out, d), x.dtype), mesh=_sc_mesh())
    def scatter_kernel(x_hbm, i_hbm, o_hbm):
        def body(x_vmem, i_vmem):
            j = pl.program_id(1)
            pltpu.sync_copy(x_vmem, o_hbm.at[i_vmem.at[0], pl.ds(j * SC_COLS, SC_COLS)])

        pltpu.emit_pipeline(
            body,
            grid=(kk * nb, d // SC_COLS),
            in_specs=[pl.BlockSpec((SC_WINDOW, SC_COLS), lambda g, j: (g % nb + off, j)),
                      pl.BlockSpec((1, SC_WINDOW), lambda g, j: (g // nb, g % nb))],
            out_specs=[],
            core_axis_name=("core", "subcore"),
            dimension_semantics=(pltpu.PARALLEL, pltpu.ARBITRARY),
        )(x_hbm, i_hbm)

    return scatter_kernel(x, dest_km)


def _sc_gather_rows(x, idx):
    n = idx.shape[0]
    d = x.shape[1]
    assert (n // SC_WINDOW) % SC_WORKERS == 0 and d % SC_COLS == 0

    @pl.kernel(out_type=jax.ShapeDtypeStruct((n, d), x.dtype), mesh=_sc_mesh())
    def gather_kernel(x_hbm, i_hbm, o_hbm):
        def body(i_vmem, o_vmem):
            j = pl.program_id(1)
            pltpu.sync_copy(x_hbm.at[i_vmem.at[0], pl.ds(j * SC_COLS, SC_COLS)], o_vmem)

        pltpu.emit_pipeline(
            body,
            grid=(n // SC_WINDOW, d // SC_COLS),
            in_specs=[pl.BlockSpec((1, SC_WINDOW), lambda i, j: (0, i))],
            out_specs=[pl.BlockSpec((SC_WINDOW, SC_COLS), lambda i, j: (i, j))],
            core_axis_name=("core", "subcore"),
            dimension_semantics=(pltpu.PARALLEL, pltpu.ARBITRARY),
        )(i_hbm, o_hbm)

    return gather_kernel(x, idx.reshape(1, n))


def _ffn_kernel(be_ref, slot_ref, nxt_ref, nu_ref, x_ref, wgu_hbm, bgu_ref, wdn_hbm, bdn_ref, y_ref,
                wgu_f, wdn_f, wgu_b, wdn_b, sem, *, layer):
    j = pl.program_id(0)
    e = be_ref[j]
    slot = slot_ref[j]
    used = j < nu_ref[0]
    first = used & ((j == 0) | (e != be_ref[jnp.maximum(j - 1, 0)]))

    def weight_copies(expert, s):
        return (pltpu.make_async_copy(wgu_hbm.at[layer, expert], wgu_f.at[s], sem.at[0, s]),
                pltpu.make_async_copy(wdn_hbm.at[layer, expert], wdn_f.at[s], sem.at[1, s]))

    @pl.when(first)
    def _():
        @pl.when(j == 0)
        def _():
            for cp in weight_copies(e, slot):
                cp.start()

        for cp in weight_copies(e, slot):
            cp.wait()
        wgu_b[...] = wgu_f[slot].astype(BF16)
        wdn_b[...] = wdn_f[slot].astype(BF16)
        nxt = nxt_ref[j]

        @pl.when(nxt >= 0)
        def _():
            for cp in weight_copies(nxt, 1 - slot):
                cp.start()

    @pl.when(used)
    def _():
        xp = x_ref[...]
        half = xp.shape[1]
        x_lo, x_hi = _unpack_bf16_pairs(xp)
        gu = (jnp.dot(x_lo, wgu_b[0:half, :], preferred_element_type=F32)
              + jnp.dot(x_hi, wgu_b[half:, :], preferred_element_type=F32) + bgu_ref[...])
        g = jnp.minimum(gu[:, 0:D_FF], SWIGLU_LIMIT)
        u = jnp.clip(gu[:, D_FF:], -SWIGLU_LIMIT, SWIGLU_LIMIT)
        act = g * jax.nn.sigmoid(SWIGLU_ALPHA * g)
        hmid = ((u + 1.0) * act).astype(BF16)
        y = jnp.dot(hmid, wdn_b[...], preferred_element_type=F32) + bdn_ref[...]
        y_ref[...] = _pack_bf16_pairs(y[:, 0:half], y[:, half:])

    @pl.when(j >= nu_ref[0])
    def _():
        y_ref[...] = jnp.zeros_like(y_ref)


def _ffn_call(layer, blk_e, blk_slot, blk_next, n_used, xs, w_gu, b_gu, w_dn, b_dn):
    n_blocks = blk_e.shape[0]
    d = 2 * xs.shape[1]
    bm = MOE_BLOCK
    pre = lambda f: (lambda j, be, sl, nx, nu: f(j, be))
    grid_spec = pltpu.PrefetchScalarGridSpec(
        num_scalar_prefetch=4,
        grid=(n_blocks,),
        in_specs=[
            pl.BlockSpec((bm, d // 2), pre(lambda j, be: (j, 0))),
            pl.BlockSpec(memory_space=pl.ANY),
            pl.BlockSpec((None, None, 1, 2 * D_FF), pre(lambda j, be: (layer, be[j], 0, 0))),
            pl.BlockSpec(memory_space=pl.ANY),
            pl.BlockSpec((None, None, 1, d), pre(lambda j, be: (layer, be[j], 0, 0))),
        ],
        out_specs=pl.BlockSpec((bm, d // 2), pre(lambda j, be: (j, 0))),
        scratch_shapes=[pltpu.VMEM((2, d, 2 * D_FF), F32), pltpu.VMEM((2, D_FF, d), F32),
                        pltpu.VMEM((d, 2 * D_FF), BF16), pltpu.VMEM((D_FF, d), BF16),
                        pltpu.SemaphoreType.DMA((2, 2))],
    )
    return pl.pallas_call(
        functools.partial(_ffn_kernel, layer=layer),
        grid_spec=grid_spec,
        out_shape=jax.ShapeDtypeStruct((n_blocks * bm, d // 2), I32),
        compiler_params=_cparams(("arbitrary",)),
        name="expert_ffn",
    )(blk_e, blk_slot, blk_next, n_used, xs, w_gu, b_gu, w_dn, b_dn)


def _combine_kernel(y1_ref, yk_ref, tw_ref, mod_ref, gf_ref, o_ref, *, final):
    tw = tw_ref[...]
    acc_lo, acc_hi = None, None
    for kk in range(TOP_K):
        lo, hi = _unpack_bf16_pairs(yk_ref[kk], F32)
        wk = tw[:, kk:kk + 1]
        acc_lo = wk * lo if acc_lo is None else acc_lo + wk * lo
        acc_hi = wk * hi if acc_hi is None else acc_hi + wk * hi
    y2 = y1_ref[...] + mod_ref[5] * jnp.concatenate([acc_lo, acc_hi], axis=1)
    if final:
        y2 = y2 * lax.rsqrt(jnp.mean(y2 * y2, axis=-1, keepdims=True) + NORM_EPS) * gf_ref[...]
    o_ref[...] = y2


def _combine_call(y1, yk, tw, mod, g_final, tm, seq_len, y_row0, n_rows, yk_row0, tw_row0, final):
    n, d = y1.shape
    tiles_per_seq = max(seq_len // tm, 1)
    y_off, yk_off, tw_off = y_row0 // tm, yk_row0 // tm, tw_row0 // tm
    return pl.pallas_call(
        functools.partial(_combine_kernel, final=final),
        grid=(n_rows // tm,),
        in_specs=[
            pl.BlockSpec((tm, d), lambda i: (i + y_off, 0)),
            pl.BlockSpec((TOP_K, tm, d // 2), lambda i: (0, i + yk_off, 0)),
            pl.BlockSpec((tm, SMALL_W), lambda i: (i + tw_off, 0)),
            _mod_spec(mod, tm, tiles_per_seq, tile_off=y_off),
            pl.BlockSpec((1, d), lambda i: (0, 0)),
        ],
        out_specs=pl.BlockSpec((tm, d), lambda i: (i + y_off, 0)),
        out_shape=jax.ShapeDtypeStruct((n, d), F32),
        input_output_aliases={0: 0},
        compiler_params=_cparams(("arbitrary",)),
        name="moe_combine",
    )(y1, yk, tw, mod, g_final)


def _block_diag(w):
    nb, bw, _ = w.shape
    eye = jnp.eye(nb, dtype=w.dtype)
    return (eye[:, None, :, None] * w[:, :, None, :]).reshape(nb * bw, nb * bw)


def _layer_weights(l, w_in, b_forget, conv_w, conv_b, lru_lambda, lru_wa, lru_ba, lru_wx, lru_bx,
                   gla_w2, gla_b2, gla_gnorm, w_out, w_router, b_router):
    wi = w_in[l]
    d = wi.shape[0]
    w_ff = wi[:, _O_FF:_O_LX]
    w_ga = wi[:, _O_GA:_O_GOG]
    w_rec = jnp.concatenate([wi[:, _O_LX:_O_GA], wi[:, _O_GOG:]], axis=1)
    w2 = jnp.zeros((SMALL_W, GLA_W), F32).at[FOX_HEADS:FOX_HEADS + GLA_RANK].set(gla_w2[l])
    inw = {
        "w_q": wi[:, _O_FQ:_O_FK].astype(BF16),
        "w_qx": jnp.pad(wi[:, _O_FQ:_O_FK].reshape(d, FOX_HEADS, HEAD_DIM),
                        ((0, 0), (0, 0), (0, FOX_BLK - HEAD_DIM))).reshape(d, FOX_QX).astype(BF16),
        "w_kvt": wi[:, _O_FK:_O_FF].T.astype(BF16),
        "w_rec": w_rec.astype(BF16),
        "w_sm": jnp.concatenate([w_ff, w_ga, jnp.zeros((d, SMALL_W - FOX_HEADS - GLA_RANK), F32)], axis=1).astype(BF16),
        "w_smt": jnp.concatenate([w_ff, w_ga, jnp.zeros((d, SMALL_T - FOX_HEADS - GLA_RANK), F32)], axis=1).T.astype(BF16),
        "b_sm": jnp.zeros((1, SMALL_W), F32).at[0, 0:FOX_HEADS].set(b_forget[l]),
        "b_fc": b_forget[l].reshape(FOX_HEADS, 1),
        "w2": w2.astype(BF16),
        "b2": gla_b2[l].reshape(1, GLA_W),
        "w_gt": w_rec[:, 2 * GLA_W:].T.astype(BF16),
        "w2t": w2[0:SMALL_T].T.astype(BF16),
        "b2c": gla_b2[l].reshape(GLA_W, 1),
    }
    lw = {
        "conv_w": conv_w[l], "conv_b": conv_b[l].reshape(1, LRU_W),
        "wa": _block_diag(lru_wa[l]).astype(BF16), "ba": lru_ba[l].reshape(1, LRU_W),
        "wx": _block_diag(lru_wx[l]).astype(BF16), "bx": lru_bx[l].reshape(1, LRU_W),
        "lam": lru_lambda[l].reshape(1, LRU_W),
    }
    ow = {
        "gg": gla_gnorm[l].reshape(1, GLA_W),
        "ggc": gla_gnorm[l].reshape(GLA_W, 1),
        "w_out": w_out[l].astype(BF16),
        "w_r": jnp.concatenate([w_router[l], jnp.zeros((d, SMALL_W - N_EXPERTS), F32)], axis=1).astype(BF16),
        "b_r": jnp.zeros((1, SMALL_W), F32).at[0, 0:N_EXPERTS].set(b_router[l]),
    }
    return inw, lw, ow


def kernel(x_prompt, x_sample, cache_fox_k, cache_fox_v, cache_fox_logf, state_conv, state_lru, state_gla, page_table, c_prompt, c_sample, w_ada, b_ada, g_norm1, g_norm2, w_in, b_forget, conv_w, conv_b, lru_lambda, lru_wa, lru_ba, lru_wx, lru_bx, gla_w2, gla_b2, gla_gnorm, w_out, w_router, b_router, w_gu, b_gu, w_down, b_down, g_final):
    n_layers = w_ada.shape[0]
    bp, seq, d = x_prompt.shape
    bs = x_sample.shape[0]
    n_p = bp * seq
    n_tot = n_p + bs
    n_pool = cache_fox_k.shape[1]
    bm = MOE_BLOCK

    mod = _ada_call(jnp.concatenate([c_prompt, c_sample], axis=0), w_ada, b_ada)
    mod_p = mod[:, :bp].reshape(n_layers, bp, 6, 1, d).transpose(0, 2, 1, 3, 4)
    mod_s = mod[:, bp:].reshape(n_layers, 1, bs, 6, d).transpose(0, 3, 1, 2, 4)

    ckt = cache_fox_k.transpose(0, 1, 3, 4, 2).reshape(n_layers, n_pool, FOX_W, PAGE_SIZE)
    cvt = cache_fox_v.transpose(0, 1, 3, 4, 2).reshape(n_layers, n_pool, FOX_W, PAGE_SIZE)
    cft = cache_fox_logf.transpose(0, 1, 3, 2)
    conv_t = state_conv.transpose(0, 2, 1, 3)
    s_view = state_gla.transpose(0, 2, 3, 4, 1)
    b_gu4 = b_gu.reshape(n_layers, N_EXPERTS, 1, 2 * D_FF)
    b_dn4 = b_down.reshape(n_layers, N_EXPERTS, 1, d)
    gf = g_final.reshape(1, d)

    row_quant = SC_WINDOW * SC_WORKERS // TOP_K
    n_buf = -(-n_tot // row_quant) * row_quant
    n_a = (n_p // 2) // row_quant * row_quant
    parts = ((0, n_a, n_a), (n_a, n_buf - n_a, n_tot - n_a))

    def moe_scatter(l, h2, ti, part):
        row0, n_part, n_valid = part
        n_blocks = -(-(n_valid * TOP_K + N_EXPERTS * (bm - 1)) // bm)
        trash = n_blocks * bm
        dest, cnt = _rank_call(ti, row0, n_part, n_valid, trash, 512)
        counts = cnt[0, 0:N_EXPERTS].astype(I32)
        pad_end = jnp.cumsum((counts + bm - 1) // bm * bm)
        blk_e = jnp.minimum(jnp.sum((jnp.arange(n_blocks, dtype=I32)[:, None] * bm >= pad_end[None, :]).astype(I32),
                                    axis=1), N_EXPERTS - 1).astype(I32)
        n_used = (pad_end[-1] // bm).astype(I32).reshape(1)
        ids = jnp.arange(N_EXPERTS, dtype=I32)
        present = counts > 0
        e_slot = (jnp.cumsum(present.astype(I32)) - 1) % 2
        later = jnp.where(present[None, :] & (ids[None, :] > ids[:, None]), ids[None, :], N_EXPERTS)
        e_next = jnp.min(later, axis=1)
        e_next = jnp.where(e_next < N_EXPERTS, e_next, -1)
        onehot = (blk_e[:, None] == ids[None, :]).astype(I32)
        blk_slot = jnp.sum(onehot * e_slot[None, :], axis=1).astype(I32)
        blk_next = jnp.sum(onehot * e_next[None, :], axis=1).astype(I32)
        dest_km = dest[0:TOP_K]
        xs = _sc_scatter_rows(h2, dest_km, (n_blocks + 1) * bm, row0)
        return xs, (blk_e, blk_slot, blk_next, n_used), jnp.minimum(dest_km, trash - 1)

    def moe_experts(l, sc):
        xs, tables, src = sc
        ye = _ffn_call(l, *tables, xs, w_gu, b_gu4, w_down, b_dn4)
        return _sc_gather_rows(ye, src.reshape(-1)).reshape(TOP_K, src.shape[1], d // 2)

    yp = x_prompt.reshape(n_p, d)
    ys = x_sample.reshape(bs, d)
    kv_p = (jnp.zeros((n_layers, bp, FOX_W, seq), F32), jnp.zeros((n_layers, bp, FOX_W, seq), F32),
            jnp.zeros((n_layers, bp, FOX_HEADS, seq), F32))
    outs_p = [[] for _ in range(3)]
    outs_s = [[] for _ in range(6)]
    for l in range(n_layers):
        inw, lw, ow = _layer_weights(l, w_in, b_forget, conv_w, conv_b, lru_lambda, lru_wa, lru_ba, lru_wx,
                                     lru_bx, gla_w2, gla_b2, gla_gnorm, w_out, w_router, b_router)
        g1 = g_norm1[l].reshape(1, d)
        g2 = g_norm2[l].reshape(1, d)

        kt_p, vt_p, lft_p, rz, qx, kx, vx = _inproj_call(yp, mod_p[l], g1, inw, seq, 256, l, n_layers, kv_bufs=kv_p)
        kv_p = (kt_p, vt_p, lft_p)
        fo = _fox_call(qx, kx, vx, bp, seq, 512)
        lo, conv_p, hlast_p = _lru_call(rz, lw, bp, seq, 256)
        go, st_p = _gla_call(rz, ow["gg"], bp, seq, 256)
        y1p, h2, ti, tw = _outproj_call(yp, fo, lo, go, mod_p[l], g2, ow["w_out"], ow["w_r"], ow["b_r"],
                                        256, seq, n_buf, 0)
        sc_a = moe_scatter(l, h2, ti, parts[0])
        st_p = st_p.reshape(bp, GLA_HEADS, HEAD_DIM, GLA_HEADS, HEAD_DIM)
        st_p = jnp.stack([st_p[:, hh, :, hh, :] for hh in range(GLA_HEADS)], axis=1).transpose(0, 1, 3, 2)
        outs_p[0].append(conv_p)
        outs_p[1].append(hlast_p.reshape(bp, LRU_W))
        outs_p[2].append(st_p)

        kts, vts, lfts, rzs, qs, ks, vs, logfs, gts = _inproj_call(
            ys, mod_s[l], g1, inw, bs, bs, 0, 1, sample=True)
        fos = _fox_dec_call(l, page_table, qs, ks, vs, logfs, ckt, cvt, cft)
        los, conv_s, h_s = _lru_step_call(l, rzs, conv_t, state_lru, lw)
        gost, s_s = _gla_step_call(l, gts, s_view, ow["ggc"])
        y1s, h2, ti, tw = _outproj_call(ys, fos, los, gost.T, mod_s[l], g2, ow["w_out"], ow["w_r"], ow["b_r"],
                                        bs, bs, n_buf, n_p, prev=(h2, ti, tw))
        outs_s[0].append(kts[0, 0])
        outs_s[1].append(vts[0, 0])
        outs_s[2].append(lfts[0, 0])
        outs_s[3].append(conv_s)
        outs_s[4].append(h_s)
        outs_s[5].append(s_s)

        sc_b = moe_scatter(l, h2, ti, parts[1])
        yk_a = moe_experts(l, sc_a)
        yk_b = moe_experts(l, sc_b)
        final = l == n_layers - 1
        yp = _combine_call(y1p, yk_a, tw, mod_p[l], gf, 256, seq, 0, n_a, 0, 0, final)
        yp = _combine_call(yp, yk_b, tw, mod_p[l], gf, 256, seq, n_a, n_p - n_a, 0, n_a, final)
        ys = _combine_call(y1s, yk_b, tw, mod_s[l], gf, bs, bs, 0, bs, n_p - n_a, n_p, final)

    kt_p, vt_p, lft_p = kv_p
    fox_k_p = kt_p.reshape(n_layers, bp, FOX_HEADS, HEAD_DIM, seq).transpose(0, 1, 4, 2, 3)
    fox_v_p = vt_p.reshape(n_layers, bp, FOX_HEADS, HEAD_DIM, seq).transpose(0, 1, 4, 2, 3)
    fox_f_p = lft_p.transpose(0, 1, 3, 2)
    fox_k_s = jnp.stack(outs_s[0]).reshape(n_layers, FOX_HEADS, HEAD_DIM, bs).transpose(0, 3, 1, 2)[:, :, None]
    fox_v_s = jnp.stack(outs_s[1]).reshape(n_layers, FOX_HEADS, HEAD_DIM, bs).transpose(0, 3, 1, 2)[:, :, None]
    fox_f_s = jnp.stack(outs_s[2]).transpose(0, 2, 1)[:, :, None]
    return (yp.reshape(bp, seq, d), ys.reshape(bs, 1, d),
            fox_k_p, fox_v_p, fox_f_p,
            jnp.stack(outs_p[0]), jnp.stack(outs_p[1]), jnp.stack(outs_p[2]),
            fox_k_s, fox_v_s, fox_f_s,
            jnp.stack(outs_s[3]).transpose(0, 2, 1, 3), jnp.stack(outs_s[4]),
            jnp.stack(outs_s[5]).transpose(0, 4, 1, 2, 3))
```

```python
import functools

import jax
import jax.numpy as jnp
from jax import lax
from jax.experimental import pallas as pl
from jax.experimental.pallas import tpu as pltpu
from jax.experimental.pallas import tpu_sc as plsc

F32 = jnp.float32
BF16 = jnp.bfloat16
I32 = jnp.int32

D_MODEL = 1024
HEAD_DIM = 64
FOX_W = 512
FOX_HEADS = 8
LRU_W = 256
GLA_W = 256
GLA_HEADS = 4
GLA_RANK = 16
GLA_CHUNK = 64
GLA_TAU = 16.0
LRU_C = 8.0
CONV_W = 4
N_EXPERTS = 32
TOP_K = 4
D_FF = 1024
SWIGLU_LIMIT = 7.0
SWIGLU_ALPHA = 1.702
NORM_EPS = 1e-6
PAGE_SIZE = 128
QK_SCALE = HEAD_DIM ** -0.5

_O_FQ, _O_FK, _O_FV, _O_FF = 0, 512, 1024, 1536
_O_LX, _O_LG, _O_GQ, _O_GK, _O_GV, _O_GA, _O_GOG = 1544, 1800, 2056, 2312, 2568, 2824, 2840
REC_W = 1792
GLA_T_ROWS = 5 * GLA_W
SMALL_W = 128
SMALL_T = 32
ROUTE_ROWS = 8
FOX_BLK = 2 * HEAD_DIM
FOX_QX = FOX_HEADS * FOX_BLK

VMEM_LIMIT = 56 * 1024 * 1024
MOE_BLOCK = 256
SC_WINDOW = 128
SC_COLS = 256
SC_WORKERS = 32

_NT = (((1,), (1,)), ((), ()))
_TN = (((0,), (0,)), ((), ()))


def _cparams(sem, vmem=VMEM_LIMIT):
    return pltpu.CompilerParams(dimension_semantics=sem, vmem_limit_bytes=vmem)


def _log_sigmoid(x):
    return jnp.minimum(x, 0.0) - jnp.log1p(jnp.exp(-jnp.abs(x)))


def _softplus(x):
    return jnp.maximum(x, 0.0) + jnp.log1p(jnp.exp(-jnp.abs(x)))


def _cumsum(x, axis):
    n = x.shape[axis]
    idx = lax.broadcasted_iota(I32, x.shape, axis)
    s = 1
    while s < n:
        x = x + jnp.where(idx >= s, pltpu.roll(x, s, axis), 0.0)
        s *= 2
    return x


_HI16 = -65536


def _pack_bf16_pairs(lo, hi):
    lo_bits = lax.bitcast_convert_type(lo.astype(BF16).astype(F32), I32)
    hi_bits = lax.bitcast_convert_type(hi.astype(BF16).astype(F32), I32)
    return lax.shift_right_logical(lo_bits, jnp.full(lo_bits.shape, 16, I32)) | (hi_bits & _HI16)


def _unpack_bf16_pairs(packed, dtype=BF16):
    lo = lax.bitcast_convert_type(lax.shift_left(packed, jnp.full(packed.shape, 16, I32)), F32)
    hi = lax.bitcast_convert_type(packed & _HI16, F32)
    return lo.astype(dtype), hi.astype(dtype)


def _mod_spec(mod, tm, tiles_per_seq, last_tile=None, tile_off=0):
    d = mod.shape[-1]
    clamp = (lambda i: i + tile_off) if last_tile is None else (lambda i: jnp.minimum(i, last_tile))
    if mod.shape[2] != 1:
        return pl.BlockSpec((6, None, tm, d), lambda i: (0, 0, clamp(i), 0))
    return pl.BlockSpec((6, None, 1, d), lambda i: (0, clamp(i) // tiles_per_seq, 0, 0))


def _ada_kernel(c_ref, w_ref, b_ref, o_ref):
    c = c_ref[...]
    a = (c * jax.nn.sigmoid(c)).astype(BF16)
    o_ref[...] = jnp.dot(a, w_ref[...].astype(BF16), preferred_element_type=F32) + b_ref[...]


def _ada_call(c_all, w_ada, b_ada):
    n_layers, d, w = w_ada.shape
    r = c_all.shape[0]
    tn = 1536
    return pl.pallas_call(
        _ada_kernel,
        grid=(n_layers, w // tn),
        in_specs=[
            pl.BlockSpec((r, d), lambda l, j: (0, 0)),
            pl.BlockSpec((None, d, tn), lambda l, j: (l, 0, j)),
            pl.BlockSpec((None, 1, tn), lambda l, j: (l, 0, j)),
        ],
        out_specs=pl.BlockSpec((None, r, tn), lambda l, j: (l, 0, j)),
        out_shape=jax.ShapeDtypeStruct((n_layers, r, w), F32),
        compiler_params=_cparams(("arbitrary", "arbitrary")),
        name="ada_mod",
    )(c_all, w_ada, b_ada.reshape(n_layers, 1, w))


def _inproj_kernel(*refs, tiles_per_seq, sample, n_alias):
    (y_ref, mod_ref, g_ref, wq_ref, wkvt_ref, wrec_ref, wsm_ref, wsmt_ref,
     bsm_ref, bfc_ref, w2_ref, b2_ref) = refs[:12]
    if sample:
        wgt_ref, w2t_ref, b2c_ref = refs[12:15]
        pos = 15 + n_alias
    else:
        eq_ref, oneq_ref, ek_ref, onek_ref = refs[12:16]
        pos = 16 + n_alias
    kt_ref, vt_ref, lft_ref, rz_ref = refs[pos:pos + 4]
    pos += 4
    if sample:
        q_ref, k_ref, v_ref, logf_ref, gt_ref = refs[pos:pos + 5]
        pos += 5
    else:
        qx_ref, kx_ref, vx_ref = refs[pos:pos + 3]
        pos += 3
    carry_c, carry_r = refs[pos:pos + 2]
    i = pl.program_id(0)

    @pl.when(i % tiles_per_seq == 0)
    def _():
        carry_c[...] = jnp.zeros_like(carry_c)
        carry_r[...] = jnp.zeros_like(carry_r)

    x = y_ref[...]
    xn = x * lax.rsqrt(jnp.mean(x * x, axis=-1, keepdims=True) + NORM_EPS) * g_ref[...]
    h = (xn * (1.0 + mod_ref[1]) + mod_ref[0]).astype(BF16)

    kt = lax.dot_general(wkvt_ref[0:FOX_W, :], h, _NT, preferred_element_type=F32)
    vt = lax.dot_general(wkvt_ref[FOX_W:2 * FOX_W, :], h, _NT, preferred_element_type=F32)
    kt_ref[...] = kt
    vt_ref[...] = vt
    rz_ref[:, 0:REC_W - GLA_W] = jnp.dot(h, wrec_ref[...], preferred_element_type=F32)

    sm = jnp.dot(h, wsm_ref[...], preferred_element_type=F32)
    lane = lax.broadcasted_iota(I32, sm.shape, 1)
    logf = jnp.where(lane < FOX_HEADS, _log_sigmoid(sm + bsm_ref[...]), 0.0)
    cum = _cumsum(logf, 0) + carry_c[...]
    carry_c[...] = cum[cum.shape[0] - 1:, :]
    glin =jnp.dot(sm.astype(BF16), w2_ref[...], preferred_element_type=F32) + b2_ref[...]
    rz_ref[:, REC_W - GLA_W:REC_W] = _log_sigmoid(glin) * (1.0 / GLA_TAU)

    smt = lax.dot_general(wsmt_ref[...], h, _NT, preferred_element_type=F32)
    lft = _log_sigmoid(smt[0:FOX_HEADS, :] + bfc_ref[...])
    lft_ref[...] = lft
    cumt = _cumsum(lft, 1) + carry_r[...]
    carry_r[...] = cumt[:, cumt.shape[1] - 1:]

    q = jnp.dot(h, wq_ref[...], preferred_element_type=F32) * QK_SCALE
    if not sample:
        def split3(c):
            hi = c.astype(BF16)
            r1 = c - hi.astype(F32)
            mid = r1.astype(BF16)
            lo = (r1 - mid.astype(F32)).astype(BF16)
            return [hi, mid, lo]

        cq = jnp.concatenate(split3(cum), axis=1)
        qx_ref[...] = (q + jnp.dot(cq, eq_ref[...], preferred_element_type=F32) + oneq_ref[...]).astype(BF16)
        ck = jnp.concatenate(split3(cumt) + [jnp.zeros(cumt.shape, BF16)], axis=0)
        kb = jnp.dot(ek_ref[...], ck, preferred_element_type=F32) + onek_ref[...]
        parts = []
        for hh in range(FOX_HEADS):
            parts += [kt[hh * HEAD_DIM:(hh + 1) * HEAD_DIM, :], kb[hh * HEAD_DIM:(hh + 1) * HEAD_DIM, :]]
        kx_ref[...] = jnp.concatenate(parts, axis=0).astype(BF16)
        vx_ref[...] = vt.astype(BF16)

    if sample:
        q_ref[...] = q.astype(BF16)
        k_ref[...] = kt.T
        v_ref[...] = vt.T
        logf_ref[...] = logf[:, 0:FOX_HEADS]
        gt_ref[0:4 * GLA_W, :] = lax.dot_general(wgt_ref[...], h, _NT, preferred_element_type=F32)
        glt = jnp.dot(w2t_ref[...], smt.astype(BF16), preferred_element_type=F32) + b2c_ref[...]
        gt_ref[4 * GLA_W:GLA_T_ROWS, :] = _log_sigmoid(glt) * (1.0 / GLA_TAU)


def _inproj_call(y, mod, g, wts, seq_len, tm, layer, n_layers, kv_bufs=None, sample=False):
    n, d = y.shape
    tiles_per_seq = seq_len // tm
    n_seq = n // seq_len
    qw = FOX_W if sample else FOX_QX
    const = lambda i: (0, 0)
    row = lambda i: (i, 0)
    seq_t = lambda i: (layer, i // tiles_per_seq, 0, i % tiles_per_seq)
    seq_t3 = lambda i: (i // tiles_per_seq, 0, i % tiles_per_seq)
    in_specs = [
        pl.BlockSpec((tm, d), row),
        _mod_spec(mod, tm, tiles_per_seq),
        pl.BlockSpec((1, d), const),
        pl.BlockSpec((d, qw), const),
        pl.BlockSpec((2 * FOX_W, d), const),
        pl.BlockSpec((d, REC_W - GLA_W), const),
        pl.BlockSpec((d, SMALL_W), const),
        pl.BlockSpec((SMALL_T, d), const),
        pl.BlockSpec((1, SMALL_W), const),
        pl.BlockSpec((FOX_HEADS, 1), const),
        pl.BlockSpec((SMALL_W, GLA_W), const),
        pl.BlockSpec((1, GLA_W), const),
    ]
    args = [y, mod, g, wts["w_q" if sample else "w_qx"], wts["w_kvt"], wts["w_rec"], wts["w_sm"], wts["w_smt"],
            wts["b_sm"], wts["b_fc"], wts["w2"], wts["b2"]]
    if sample:
        in_specs += [pl.BlockSpec((4 * GLA_W, d), const), pl.BlockSpec((GLA_W, SMALL_T), const),
                     pl.BlockSpec((GLA_W, 1), const)]
        args += [wts["w_gt"], wts["w2t"], wts["b2c"]]
    else:
        in_specs += [pl.BlockSpec((3 * SMALL_W, FOX_QX), const), pl.BlockSpec((1, FOX_QX), const),
                     pl.BlockSpec((FOX_W, SMALL_T), const), pl.BlockSpec((FOX_W, 1), const)]
        args += list(_bias_fold_consts())
    aliases = {}
    n_alias = 0
    if kv_bufs is not None:
        n_alias = 3
        first = len(args)
        in_specs += [pl.BlockSpec(memory_space=pl.ANY)] * 3
        args += list(kv_bufs)
        aliases = {first: 0, first + 1: 1, first + 2: 2}
    out_specs = [
        pl.BlockSpec((None, None, FOX_W, tm), seq_t),
        pl.BlockSpec((None, None, FOX_W, tm), seq_t),
        pl.BlockSpec((None, None, FOX_HEADS, tm), seq_t),
        pl.BlockSpec((tm, REC_W), row),
    ]
    out_shape = [
        jax.ShapeDtypeStruct((n_layers, n_seq, FOX_W, seq_len), F32),
        jax.ShapeDtypeStruct((n_layers, n_seq, FOX_W, seq_len), F32),
        jax.ShapeDtypeStruct((n_layers, n_seq, FOX_HEADS, seq_len), F32),
        jax.ShapeDtypeStruct((n, REC_W), F32),
    ]
    if sample:
        out_specs += [pl.BlockSpec((tm, FOX_W), row), pl.BlockSpec((tm, FOX_W), row), pl.BlockSpec((tm, FOX_W), row),
                      pl.BlockSpec((tm, FOX_HEADS), row), pl.BlockSpec((GLA_T_ROWS, tm), lambda i: (0, i))]
        out_shape += [jax.ShapeDtypeStruct((n, FOX_W), BF16), jax.ShapeDtypeStruct((n, FOX_W), F32),
                      jax.ShapeDtypeStruct((n, FOX_W), F32), jax.ShapeDtypeStruct((n, FOX_HEADS), F32),
                      jax.ShapeDtypeStruct((GLA_T_ROWS, n), F32)]
    else:
        out_specs += [pl.BlockSpec((tm, FOX_QX), row), pl.BlockSpec((None, FOX_QX, tm), seq_t3),
                      pl.BlockSpec((None, FOX_W, tm), seq_t3)]
        out_shape += [jax.ShapeDtypeStruct((n, FOX_QX), BF16), jax.ShapeDtypeStruct((n_seq, FOX_QX, seq_len), BF16),
                      jax.ShapeDtypeStruct((n_seq, FOX_W, seq_len), BF16)]
    return pl.pallas_call(
        functools.partial(_inproj_kernel, tiles_per_seq=tiles_per_seq, sample=sample, n_alias=n_alias),
        grid=(n // tm,),
        in_specs=in_specs,
        out_specs=tuple(out_specs),
        out_shape=tuple(out_shape),
        input_output_aliases=aliases,
        scratch_shapes=[pltpu.VMEM((1, SMALL_W), F32), pltpu.VMEM((FOX_HEADS, 1), F32)],
        compiler_params=_cparams(("arbitrary",)),
        name="in_proj",
    )(*args)


def _bias_fold_consts():
    h = jnp.arange(FOX_HEADS)
    eq = jnp.zeros((3 * SMALL_W, FOX_QX), F32)
    ek = jnp.zeros((FOX_W, SMALL_T), F32)
    oneq = jnp.zeros((1, FOX_QX), F32)
    onek = jnp.zeros((FOX_W, 1), F32)
    for piece in range(3):
        eq = eq.at[piece * SMALL_W + h, h * FOX_BLK + HEAD_DIM + piece].set(1.0)
        oneq = oneq.at[0, h * FOX_BLK + HEAD_DIM + 3 + piece].set(1.0)
        onek = onek.at[h * HEAD_DIM + piece, 0].set(1.0)
        ek = ek.at[h * HEAD_DIM + 3 + piece, piece * FOX_HEADS + h].set(-1.0)
    return eq.astype(BF16), oneq, ek.astype(BF16), onek


def _fox_kernel(qi_ref, ki_ref, qx_ref, kx_ref, vx_ref, o_ref, m_sc, l_sc, acc_sc, *, tq, tk):
    qi = qi_ref[pl.program_id(2)]
    ki = ki_ref[pl.program_id(2)]
    w = 2 * HEAD_DIM
    nc = tk // w

    @pl.when(ki == 0)
    def _():
        m_sc[...] = jnp.full_like(m_sc, -jnp.inf)
        l_sc[...] = jnp.zeros_like(l_sc)
        acc_sc[...] = jnp.zeros_like(acc_sc)

    def step(masked):
        vt = vx_ref[...]
        if masked:
            causal = lax.broadcasted_iota(I32, (tq, tk), 1) <= lax.broadcasted_iota(I32, (tq, tk), 0)
        for j in range(2):
            s = jnp.dot(qx_ref[:, j * FOX_BLK:(j + 1) * FOX_BLK], kx_ref[j * FOX_BLK:(j + 1) * FOX_BLK, :],
                        preferred_element_type=F32)
            if masked:
                s = jnp.where(causal, s, -jnp.inf)
            sc = [s[:, c * w:(c + 1) * w] for c in range(nc)]
            mb = sc[0]
            for c in range(1, nc):
                mb = jnp.maximum(mb, sc[c])
            m_prev = m_sc[j]
            m_new = jnp.maximum(m_prev, jnp.broadcast_to(jnp.max(mb, axis=1, keepdims=True), (tq, w)))
            ps = [jnp.exp(sc[c] - m_new) for c in range(nc)]
            lsum = ps[0]
            for c in range(1, nc):
                lsum = lsum + ps[c]
            alpha = jnp.exp(m_prev - m_new)
            l_sc[j] = alpha * l_sc[j] + jnp.broadcast_to(jnp.sum(lsum, axis=1, keepdims=True), (tq, w))
            p = jnp.concatenate([pc.astype(BF16) for pc in ps], axis=1)
            acc_sc[j] = alpha * acc_sc[j] + lax.dot_general(p, vt, _NT, preferred_element_type=F32)
            m_sc[j] = m_new

    @pl.when(ki < qi)
    def _():
        step(False)

    @pl.when(ki == qi)
    def _():
        step(True)
        lane = lax.broadcasted_iota(I32, (tq, w), 1)
        o0 = acc_sc[0] / l_sc[0]
        o1 = acc_sc[1] / l_sc[1]
        o_ref[...] = jnp.where(lane < HEAD_DIM, o0, o1).astype(BF16)


def _fox_call(qx, kx, vx, n_seq, seq_len, tq):
    n = qx.shape[0]
    nt = seq_len // tq
    tk = tq
    hp = FOX_HEADS // 2
    w = 2 * HEAD_DIM
    pairs = [(i, j) for i in range(nt) for j in range(i + 1)]
    qi_tab = jnp.asarray([p[0] for p in pairs], I32)
    ki_tab = jnp.asarray([p[1] for p in pairs], I32)
    past = lambda b, h, s, qt, kt: (b, h, kt[s])
    grid_spec = pltpu.PrefetchScalarGridSpec(
        num_scalar_prefetch=2,
        grid=(n_seq, hp, len(pairs)),
        in_specs=[
            pl.BlockSpec((tq, 2 * FOX_BLK), lambda b, h, s, qt, kt: (b * nt + qt[s], h)),
            pl.BlockSpec((None, 2 * FOX_BLK, tk), past),
            pl.BlockSpec((None, w, tk), past),
        ],
        out_specs=pl.BlockSpec((tq, w), lambda b, h, s, qt, kt: (b * nt + qt[s], h)),
        scratch_shapes=[pltpu.VMEM((2, tq, w), F32), pltpu.VMEM((2, tq, w), F32), pltpu.VMEM((2, tq, w), F32)],
    )
    return pl.pallas_call(
        functools.partial(_fox_kernel, tq=tq, tk=tk),
        grid_spec=grid_spec,
        out_shape=jax.ShapeDtypeStruct((n, FOX_W), BF16),
        compiler_params=_cparams(("arbitrary", "arbitrary", "arbitrary")),
        name="fox_prompt",
    )(qi_tab, ki_tab, qx, kx, vx)


def _lru_gates(xc, wa_ref, ba_ref, wx_ref, bx_ref, lam_ref):
    xb = xc.astype(BF16)
    r = jax.nn.sigmoid(jnp.dot(xb, wa_ref[...], preferred_element_type=F32) + ba_ref[...])
    gi = jax.nn.sigmoid(jnp.dot(xb, wx_ref[...], preferred_element_type=F32) + bx_ref[...])
    log_a = -LRU_C * r * _softplus(-lam_ref[...])
    a = jnp.exp(log_a)
    mult = jnp.sqrt(-jnp.tanh(log_a) * (a * a + 1.0))
    return a, mult, gi


def _lru_kernel(lx_ref, lg_ref, cw_ref, cb_ref, wa_ref, ba_ref, wx_ref, bx_ref, lam_ref,
                lo_ref, conv_ref, hlast_ref, xbuf, hcar, *, tt):
    ti = pl.program_id(1)
    nt = pl.num_programs(1)

    @pl.when(ti == 0)
    def _():
        xbuf[0:8, :] = jnp.zeros((8, LRU_W), F32)
        hcar[...] = jnp.zeros_like(hcar)

    x = lx_ref[...]
    xbuf[8:8 + tt, :] = x
    xc = cb_ref[...] + cw_ref[3:4, :] * x
    for j in range(CONV_W - 1):
        xc = xc + cw_ref[j:j + 1, :] * xbuf[5 + j:5 + j + tt, :]
    xbuf[0:8, :] = x[tt - 8:tt, :]

    a, mult, gi = _lru_gates(xc, wa_ref, ba_ref, wx_ref, bx_ref, lam_ref)
    row = lax.broadcasted_iota(I32, (tt, LRU_W), 0)
    mult = jnp.where((row == 0) & (ti == 0), 1.0, mult)
    b = mult * gi * xc
    s = 1
    while s < tt:
        keep = row >= s
        a_sh = jnp.where(keep, pltpu.roll(a, s, 0), 1.0)
        b_sh = jnp.where(keep, pltpu.roll(b, s, 0), 0.0)
        b = a * b_sh + b
        a = a * a_sh
        s *= 2
    h = a * hcar[...] + b
    hcar[...] = h[tt - 1:tt, :]
    lo_ref[...] = (h * jax.nn.gelu(lg_ref[...])).astype(BF16)

    @pl.when(ti == nt - 1)
    def _():
        conv_ref[...] = x[tt - (CONV_W - 1):tt, :]
        hlast_ref[...] = h[tt - 1:tt, :]


def _lru_weight_specs(const):
    return [
        pl.BlockSpec((CONV_W, LRU_W), const), pl.BlockSpec((1, LRU_W), const),
        pl.BlockSpec((LRU_W, LRU_W), const), pl.BlockSpec((1, LRU_W), const),
        pl.BlockSpec((LRU_W, LRU_W), const), pl.BlockSpec((1, LRU_W), const),
        pl.BlockSpec((1, LRU_W), const),
    ]


def _lru_weight_args(lw):
    return [lw["conv_w"], lw["conv_b"], lw["wa"], lw["ba"], lw["wx"], lw["bx"], lw["lam"]]


def _lru_call(rz, lw, n_seq, seq_len, tt):
    n = rz.shape[0]
    nt = seq_len // tt
    return pl.pallas_call(
        functools.partial(_lru_kernel, tt=tt),
        grid=(n_seq, nt),
        in_specs=[
            pl.BlockSpec((tt, LRU_W), lambda b, t: (b * nt + t, 0)),
            pl.BlockSpec((tt, LRU_W), lambda b, t: (b * nt + t, 1)),
        ] + _lru_weight_specs(lambda b, t: (0, 0)),
        out_specs=(
            pl.BlockSpec((tt, LRU_W), lambda b, t: (b * nt + t, 0)),
            pl.BlockSpec((None, CONV_W - 1, LRU_W), lambda b, t: (b, 0, 0)),
            pl.BlockSpec((None, 1, LRU_W), lambda b, t: (b, 0, 0)),
        ),
        out_shape=(
            jax.ShapeDtypeStruct((n, LRU_W), BF16),
            jax.ShapeDtypeStruct((n_seq, CONV_W - 1, LRU_W), F32),
            jax.ShapeDtypeStruct((n_seq, 1, LRU_W), F32),
        ),
        scratch_shapes=[pltpu.VMEM((tt + 8, LRU_W), F32), pltpu.VMEM((1, LRU_W), F32)],
        compiler_params=_cparams(("arbitrary", "arbitrary")),
        name="lru_prompt",
    )(rz, rz, *_lru_weight_args(lw))


def _head_rms_gate(o, gg_ref, gog):
    lane = lax.broadcasted_iota(I32, o.shape, 1)
    o2 = o * o
    rs = jnp.zeros_like(o)
    for hh in range(GLA_HEADS):
        mh = lane // HEAD_DIM == hh
        ms = jnp.sum(jnp.where(mh, o2, 0.0), axis=1, keepdims=True) * (1.0 / HEAD_DIM)
        rs = jnp.where(mh, lax.rsqrt(ms + NORM_EPS), rs)
    return o * rs * gg_ref[...] * (gog * jax.nn.sigmoid(gog))


def _gla_kernel(gq_ref, gk_ref, gv_ref, gog_ref, gl_ref, gg_ref, go_ref, st_ref, s_sc, *, tt):
    ti = pl.program_id(1)
    nt = pl.num_programs(1)
    c = GLA_CHUNK

    @pl.when(ti == 0)
    def _():
        s_sc[...] = jnp.zeros_like(s_sc)

    lane = lax.broadcasted_iota(I32, (c, GLA_W), 1)
    r2 = lax.broadcasted_iota(I32, (GLA_W, GLA_W), 0)
    c2 = lax.broadcasted_iota(I32, (GLA_W, GLA_W), 1)
    same_head = (r2 // HEAD_DIM) == (c2 // HEAD_DIM)
    tril = lax.broadcasted_iota(I32, (c, c), 1) <= lax.broadcasted_iota(I32, (c, c), 0)

    for ci in range(tt // c):
        sl = slice(ci * c, (ci + 1) * c)
        q = gq_ref[sl, :] * QK_SCALE
        k = gk_ref[sl, :]
        v = gv_ref[sl, :].astype(BF16)
        bc = _cumsum(gl_ref[sl, :], 0)
        b_last = bc[c - 1:c, :]
        qd = (q * jnp.exp(bc)).astype(BF16)
        kinv = (k * jnp.exp(-bc)).astype(BF16)
        kdec = (k * jnp.exp(b_last - bc)).astype(BF16)
        s_prev = s_sc[...]
        o = lax.dot_general(qd, s_prev.astype(BF16), _NT, preferred_element_type=F32)
        for hh in range(GLA_HEADS):
            mh = lane // HEAD_DIM == hh
            att = lax.dot_general(jnp.where(mh, qd, jnp.zeros_like(qd)), kinv, _NT, preferred_element_type=F32)
            att = jnp.where(tril, att, 0.0).astype(BF16)
            o = o + jnp.dot(att, jnp.where(mh, v, jnp.zeros_like(v)), preferred_element_type=F32)
        ut = lax.dot_general(v, kdec, _TN, preferred_element_type=F32)
        s_sc[...] = s_prev * jnp.exp(b_last) + jnp.where(same_head, ut, 0.0)
        go_ref[sl, :] = _head_rms_gate(o, gg_ref, gog_ref[sl, :]).astype(BF16)

    @pl.when(ti == nt - 1)
    def _():
        st_ref[...] = s_sc[...]


def _gla_call(rz, gg, n_seq, seq_len, tt):
    n = rz.shape[0]
    nt = seq_len // tt

    def col(j):
        return pl.BlockSpec((tt, GLA_W), lambda b, t: (b * nt + t, j))

    return pl.pallas_call(
        functools.partial(_gla_kernel, tt=tt),
        grid=(n_seq, nt),
        in_specs=[col(2), col(3), col(4), col(5), col(6), pl.BlockSpec((1, GLA_W), lambda b, t: (0, 0))],
        out_specs=(
            pl.BlockSpec((tt, GLA_W), lambda b, t: (b * nt + t, 0)),
            pl.BlockSpec((None, GLA_W, GLA_W), lambda b, t: (b, 0, 0)),
        ),
        out_shape=(
            jax.ShapeDtypeStruct((n, GLA_W), BF16),
            jax.ShapeDtypeStruct((n_seq, GLA_W, GLA_W), F32),
        ),
        scratch_shapes=[pltpu.VMEM((GLA_W, GLA_W), F32)],
        compiler_params=_cparams(("arbitrary", "arbitrary")),
        name="gla_prompt",
    )(rz, rz, rz, rz, rz, gg)


def _fox_dec_kernel(pt_ref, q_ref, kn_ref, vn_ref, dn_ref, *refs, n_pages):
    del pt_ref
    k_refs = refs[0:n_pages]
    v_refs = refs[n_pages:2 * n_pages]
    f_refs = refs[2 * n_pages:3 * n_pages]
    o_ref = refs[3 * n_pages]
    w = FOX_W
    hrow = lax.broadcasted_iota(I32, (FOX_HEADS, w), 0)
    hlane = lax.broadcasted_iota(I32, (FOX_HEADS, w), 1) // HEAD_DIM
    diag = hrow == hlane
    q = q_ref[...].astype(F32)
    qbd = jnp.where(diag, jnp.broadcast_to(q, (FOX_HEADS, w)), 0.0).astype(BF16)
    s = jnp.concatenate(
        [jnp.dot(qbd, k_refs[p][...].astype(BF16), preferred_element_type=F32) for p in range(n_pages)], axis=1)
    lf = jnp.concatenate([f_refs[p][...] for p in range(n_pages)], axis=1)
    cs = _cumsum(lf, 1)
    suffix = cs[:, cs.shape[1] - 1:] - cs
    s = s + dn_ref[...] + suffix
    s_new = jnp.sum(qbd.astype(F32) * kn_ref[...], axis=1, keepdims=True)
    m = jnp.maximum(jnp.max(s, axis=1, keepdims=True), s_new)
    p_past = jnp.exp(s - m)
    p_new = jnp.exp(s_new - m)
    denom = jnp.sum(p_past, axis=1, keepdims=True) + p_new
    acc = p_new * vn_ref[...]
    pb = p_past.astype(BF16)
    for p in range(n_pages):
        acc = acc + lax.dot_general(pb[:, p * PAGE_SIZE:(p + 1) * PAGE_SIZE], v_refs[p][...].astype(BF16), _NT,
                                    preferred_element_type=F32)
    out = jnp.where(diag, acc / denom, 0.0)
    o_ref[...] = jnp.sum(out, axis=0, keepdims=True).astype(BF16)


def _fox_dec_call(layer, page_table, q, k_new, v_new, logf_new, cache_kt, cache_vt, cache_ft):
    bd, n_pages = page_table.shape
    w = FOX_W

    def page_spec(rows, j):
        return pl.BlockSpec((None, None, rows, PAGE_SIZE), lambda b, pt, j=j: (layer, pt[b, j], 0, 0))

    row = lambda b, pt: (b, 0, 0)
    in_specs = [
        pl.BlockSpec((None, 1, w), row),
        pl.BlockSpec((None, 1, w), row),
        pl.BlockSpec((None, 1, w), row),
        pl.BlockSpec((None, FOX_HEADS, 1), row),
    ]
    in_specs += [page_spec(w, j) for j in range(n_pages)]
    in_specs += [page_spec(w, j) for j in range(n_pages)]
    in_specs += [page_spec(FOX_HEADS, j) for j in range(n_pages)]
    grid_spec = pltpu.PrefetchScalarGridSpec(
        num_scalar_prefetch=1,
        grid=(bd,),
        in_specs=in_specs,
        out_specs=pl.BlockSpec((None, 1, w), row),
    )
    out = pl.pallas_call(
        functools.partial(_fox_dec_kernel, n_pages=n_pages),
        grid_spec=grid_spec,
        out_shape=jax.ShapeDtypeStruct((bd, 1, w), BF16),
        compiler_params=_cparams(("arbitrary",)),
        name="fox_sample",
    )(page_table, q.reshape(bd, 1, w), k_new.reshape(bd, 1, w), v_new.reshape(bd, 1, w),
      logf_new.reshape(bd, FOX_HEADS, 1),
      *([cache_kt] * n_pages), *([cache_vt] * n_pages), *([cache_ft] * n_pages))
    return out.reshape(bd, w)


def _lru_step_kernel(lx_ref, lg_ref, conv_ref, h0_ref, cw_ref, cb_ref, wa_ref, ba_ref, wx_ref, bx_ref, lam_ref,
                     lo_ref, convn_ref, hn_ref):
    x = lx_ref[...]
    xc = cb_ref[...] + cw_ref[3:4, :] * x
    for j in range(CONV_W - 1):
        xc = xc + cw_ref[j:j + 1, :] * conv_ref[j]
    convn_ref[0] = conv_ref[1]
    convn_ref[1] = conv_ref[2]
    convn_ref[2] = x
    a, mult, gi = _lru_gates(xc, wa_ref, ba_ref, wx_ref, bx_ref, lam_ref)
    h = a * h0_ref[...] + mult * gi * xc
    hn_ref[...] = h
    lo_ref[...] = (h * jax.nn.gelu(lg_ref[...])).astype(BF16)


def _lru_step_call(layer, rz, conv_t, h0, lw):
    bd = rz.shape[0]
    return pl.pallas_call(
        _lru_step_kernel,
        grid=(1,),
        in_specs=[
            pl.BlockSpec((bd, LRU_W), lambda i: (0, 0)),
            pl.BlockSpec((bd, LRU_W), lambda i: (0, 1)),
            pl.BlockSpec((None, CONV_W - 1, bd, LRU_W), lambda i: (layer, 0, 0, 0)),
            pl.BlockSpec((None, bd, LRU_W), lambda i: (layer, 0, 0)),
        ] + _lru_weight_specs(lambda i: (0, 0)),
        out_specs=(
            pl.BlockSpec((bd, LRU_W), lambda i: (0, 0)),
            pl.BlockSpec((CONV_W - 1, bd, LRU_W), lambda i: (0, 0, 0)),
            pl.BlockSpec((bd, LRU_W), lambda i: (0, 0)),
        ),
        out_shape=(
            jax.ShapeDtypeStruct((bd, LRU_W), BF16),
            jax.ShapeDtypeStruct((CONV_W - 1, bd, LRU_W), F32),
            jax.ShapeDtypeStruct((bd, LRU_W), F32),
        ),
        compiler_params=_cparams(("arbitrary",)),
        name="lru_step",
    )(rz, rz, conv_t, h0, *_lru_weight_args(lw))


def _gla_step_kernel(q_ref, k_ref, v_ref, gog_ref, gl_ref, gg_ref, s_ref, go_ref, sn_ref):
    eg = jnp.exp(gl_ref[...])
    kt = k_ref[...]
    qt = q_ref[...] * QK_SCALE
    vt = v_ref[...]
    o = jnp.zeros_like(vt)
    for kk in range(HEAD_DIM):
        s_new = eg[kk:kk + 1, :] * s_ref[kk] + kt[kk:kk + 1, :] * vt
        sn_ref[kk] = s_new
        o = o + qt[kk:kk + 1, :] * s_new
    ms = jnp.mean(o * o, axis=0, keepdims=True)
    gog = gog_ref[...]
    go_ref[...] = o * lax.rsqrt(ms + NORM_EPS) * gg_ref[...] * (gog * jax.nn.sigmoid(gog))


def _gla_step_call(layer, gt, s_view, ggc):
    bd = gt.shape[1]
    hd = HEAD_DIM

    def part(j):
        return pl.BlockSpec((hd, bd), lambda h, j=j: (j * GLA_HEADS + h, 0))

    return pl.pallas_call(
        _gla_step_kernel,
        grid=(GLA_HEADS,),
        in_specs=[part(0), part(1), part(2), part(3), part(4),
                  pl.BlockSpec((hd, 1), lambda h: (h, 0)),
                  pl.BlockSpec((None, None, hd, hd, bd), lambda h: (layer, h, 0, 0, 0))],
        out_specs=(
            pl.BlockSpec((hd, bd), lambda h: (h, 0)),
            pl.BlockSpec((None, hd, hd, bd), lambda h: (h, 0, 0, 0)),
        ),
        out_shape=(
            jax.ShapeDtypeStruct((GLA_W, bd), F32),
            jax.ShapeDtypeStruct((GLA_HEADS, hd, hd, bd), F32),
        ),
        compiler_params=_cparams(("arbitrary",)),
        name="gla_step",
    )(gt, gt, gt, gt, gt, ggc, s_view)


def _outproj_kernel(*refs, n_alias, n_tiles):
    h2_ref, ti_ref, tw_ref = refs[10 + n_alias:13 + n_alias]
    i = pl.program_id(0)

    @pl.when(i < n_tiles)
    def _():
        _outproj_tile(*refs[0:9], *refs[9 + n_alias:13 + n_alias])

    @pl.when(i >= n_tiles)
    def _():
        h2_ref[...] = jnp.zeros_like(h2_ref)
        ti_ref[...] = jnp.zeros_like(ti_ref)
        tw_ref[...] = jnp.zeros_like(tw_ref)


def _outproj_tile(y_ref, fo_ref, lo_ref, go_ref, mod_ref, g2_ref, wo_ref, wr_ref, br_ref,
                  y1_ref, h2_ref, ti_ref, tw_ref):
    m = jnp.dot(fo_ref[...], wo_ref[0:FOX_W, :], preferred_element_type=F32)
    m = m + jnp.dot(lo_ref[...], wo_ref[FOX_W:FOX_W + LRU_W, :], preferred_element_type=F32)
    m = m + jnp.dot(go_ref[...].astype(BF16), wo_ref[FOX_W + LRU_W:, :], preferred_element_type=F32)
    y1 = y_ref[...] + mod_ref[2] * m
    y1_ref[...] = y1
    xn = y1 * lax.rsqrt(jnp.mean(y1 * y1, axis=-1, keepdims=True) + NORM_EPS) * g2_ref[...]
    h2 = xn * (1.0 + mod_ref[4]) + mod_ref[3]
    half = h2.shape[1] // 2
    h2_ref[...] = _pack_bf16_pairs(h2[:, 0:half], h2[:, half:])
    logits = jnp.dot(h2.astype(BF16), wr_ref[...], preferred_element_type=F32) + br_ref[...]
    tm = logits.shape[0]
    lt = logits.T[0:N_EXPERTS, :]
    eidx = lax.broadcasted_iota(I32, lt.shape, 0)
    sels, vals = [], []
    for kk in range(TOP_K):
        mx = jnp.max(lt, axis=0, keepdims=True)
        sel = jnp.min(jnp.where(lt == mx, eidx, N_EXPERTS), axis=0, keepdims=True)
        sels.append(sel)
        vals.append(mx)
        lt = jnp.where(eidx == sel, -jnp.inf, lt)
    es = [jnp.exp(vv - vals[0]) for vv in vals]
    tot = es[0] + es[1] + es[2] + es[3]
    row = lax.broadcasted_iota(I32, (ROUTE_ROWS, tm), 0)
    idx_out = jnp.zeros((ROUTE_ROWS, tm), I32)
    val_out = jnp.zeros((ROUTE_ROWS, tm), F32)
    for kk in range(TOP_K):
        idx_out = jnp.where(row == kk, sels[kk], idx_out)
        val_out = jnp.where(row == kk, es[kk] / tot, val_out)
    ti_ref[...] = idx_out
    tw_ref[...] = val_out


def _outproj_call(y, fo, lo, go, mod, g2, w_out, w_r, b_r, tm, seq_len, n_buf, row_off, prev=None):
    n, d = y.shape
    tiles_per_seq = max(seq_len // tm, 1)
    n_tiles = n // tm
    n_steps = n_tiles if prev is not None else n_buf // tm
    const = lambda i: (0, 0)
    row = lambda i: (jnp.minimum(i, n_tiles - 1), 0)
    off = row_off // tm
    orow = lambda i: (i + off, 0)
    in_specs = [
        pl.BlockSpec((tm, d), row),
        pl.BlockSpec((tm, FOX_W), row),
        pl.BlockSpec((tm, LRU_W), row),
        pl.BlockSpec((tm, GLA_W), row),
        _mod_spec(mod, tm, tiles_per_seq, n_tiles - 1),
        pl.BlockSpec((1, d), const),
        pl.BlockSpec((d, d), const),
        pl.BlockSpec((d, SMALL_W), const),
        pl.BlockSpec((1, SMALL_W), const),
    ]
    args = [y, fo, lo, go, mod, g2, w_out, w_r, b_r]
    aliases = {}
    if prev is not None:
        in_specs += [pl.BlockSpec(memory_space=pl.ANY)] * 3
        args += list(prev)
        aliases = {9: 1, 10: 2, 11: 3}
    return pl.pallas_call(
        functools.partial(_outproj_kernel, n_alias=0 if prev is None else 3, n_tiles=n_tiles),
        grid=(n_steps,),
        in_specs=in_specs,
        out_specs=(
            pl.BlockSpec((tm, d), row),
            pl.BlockSpec((tm, d // 2), orow),
            pl.BlockSpec((ROUTE_ROWS, tm), lambda i: (0, i + off)),
            pl.BlockSpec((ROUTE_ROWS, tm), lambda i: (0, i + off)),
        ),
        out_shape=(
            jax.ShapeDtypeStruct((n, d), F32),
            jax.ShapeDtypeStruct((n_buf, d // 2), I32),
            jax.ShapeDtypeStruct((ROUTE_ROWS, n_buf), I32),
            jax.ShapeDtypeStruct((ROUTE_ROWS, n_buf), F32),
        ),
        input_output_aliases=aliases,
        compiler_params=_cparams(("arbitrary",)),
        name="out_proj_router",
    )(*args)


def _rank_kernel(ti_ref, dest_ref, cnt_ref, carry, *, tm, n_valid, trash):
    p = pl.program_id(0)
    i = pl.program_id(1)
    w = SMALL_W

    @pl.when((p == 0) & (i == 0))
    def _():
        carry[...] = jnp.zeros_like(carry)

    @pl.when((p == 1) & (i == 0))
    def _():
        cnt = carry[...]
        cnt_ref[...] = cnt
        padded = jnp.floor((cnt + (MOE_BLOCK - 1.0)) * (1.0 / MOE_BLOCK)) * MOE_BLOCK
        carry[...] = _cumsum(padded, 0) - padded

    t = ti_ref[...]
    eidx = lax.broadcasted_iota(I32, (N_EXPERTS, tm), 0)
    valid = (lax.broadcasted_iota(I32, (1, tm), 1) + i * tm) < n_valid
    ohs = [jnp.where(valid & (eidx == t[kk:kk + 1, :]), 1.0, 0.0) for kk in range(TOP_K)]

    @pl.when(p == 0)
    def _():
        tile_cnt = jnp.sum(sum(ohs[1:], ohs[0]), axis=1, keepdims=True)
        carry[...] = carry[...] + jnp.broadcast_to(tile_cnt, (N_EXPERTS, w))

    @pl.when(p == 1)
    def _():
        earlier = (lax.broadcasted_iota(I32, (tm, tm), 0) < lax.broadcasted_iota(I32, (tm, tm), 1)).astype(BF16)
        stack = jnp.concatenate(ohs, axis=0).astype(BF16)
        pre = jnp.dot(stack, earlier, preferred_element_type=F32)
        base = carry[:, 0:1]
        row = lax.broadcasted_iota(I32, (ROUTE_ROWS, tm), 0)
        out = jnp.zeros((ROUTE_ROWS, tm), F32)
        for kk in range(TOP_K):
            oh = ohs[kk]
            slot = jnp.sum(oh * (pre[kk * N_EXPERTS:(kk + 1) * N_EXPERTS, :] + base), axis=0, keepdims=True)
            out = jnp.where(row == kk, jnp.where(valid, slot, float(trash)), out)
            base = base + jnp.sum(oh, axis=1, keepdims=True)
        carry[...] = jnp.broadcast_to(base, (N_EXPERTS, w))
        dest_ref[...] = out.astype(I32)


def _rank_call(ti, col0, n_cols, n_valid, trash, tm):
    off = col0 // tm
    return pl.pallas_call(
        functools.partial(_rank_kernel, tm=tm, n_valid=n_valid, trash=trash),
        grid=(2, n_cols // tm),
        in_specs=[pl.BlockSpec((ROUTE_ROWS, tm), lambda p, i: (0, i + off))],
        out_specs=(
            pl.BlockSpec((ROUTE_ROWS, tm), lambda p, i: (0, i * p)),
            pl.BlockSpec((N_EXPERTS, SMALL_W), lambda p, i: (0, 0)),
        ),
        out_shape=(
            jax.ShapeDtypeStruct((ROUTE_ROWS, n_cols), I32),
            jax.ShapeDtypeStruct((N_EXPERTS, SMALL_W), F32),
        ),
        scratch_shapes=[pltpu.VMEM((N_EXPERTS, SMALL_W), F32)],
        compiler_params=_cparams(("arbitrary", "arbitrary")),
        name="moe_rank",
    )(ti)


def _sc_mesh():
    return plsc.VectorSubcoreMesh(core_axis_name="core", subcore_axis_name="subcore")


def _sc_scatter_rows(x, dest_km, n_out, row0=0):
    kk, n = dest_km.shape
    d = x.shape[1]
    nb = n // SC_WINDOW
    off = row0 // SC_WINDOW
    assert (kk * nb) % SC_WORKERS == 0 and d % SC_COLS == 0 and row0 % SC_WINDOW == 0

    @pl.kernel(out_type=jax.ShapeDtypeStruct((n_out, d), x.dtype), mesh=_sc_mesh())
    def scatter_kernel(x_hbm, i_hbm, o_hbm):
        def body(x_vmem, i_vmem):
            j = pl.program_id(1)
            pltpu.sync_copy(x_vmem, o_hbm.at[i_vmem.at[0], pl.ds(j * SC_COLS, SC_COLS)])

        pltpu.emit_pipeline(
            body,
            grid=(kk * nb, d // SC_COLS),
            in_specs=[pl.BlockSpec((SC_WINDOW, SC_COLS), lambda g, j: (g % nb + off, j)),
                      pl.BlockSpec((1, SC_WINDOW), lambda g, j: (g // nb, g % nb))],
            out_specs=[],
            core_axis_name=("core", "subcore"),
            dimension_semantics=(pltpu.PARALLEL, pltpu.ARBITRARY),
        )(x_hbm, i_hbm)

    return scatter_kernel(x, dest_km)


def _sc_gather_rows(x, idx):
    n = idx.shape[0]
    d = x.shape[1]
    assert (n // SC_WINDOW) % SC_WORKERS == 0 and d % SC_COLS == 0

    @pl.kernel(out_type=jax.ShapeDtypeStruct((n, d), x.dtype), mesh=_sc_mesh())
    def gather_kernel(x_hbm, i_hbm, o_hbm):
        def body(i_vmem, o_vmem):
            j = pl.program_id(1)
            pltpu.sync_copy(x_hbm.at[i_vmem.at[0], pl.ds(j * SC_COLS, SC_COLS)], o_vmem)

        pltpu.emit_pipeline(
            body,
            grid=(n // SC_WINDOW, d // SC_COLS),
            in_specs=[pl.BlockSpec((1, SC_WINDOW), lambda i, j: (0, i))],
            out_specs=[pl.BlockSpec((SC_WINDOW, SC_COLS), lambda i, j: (i, j))],
            core_axis_name=("core", "subcore"),
            dimension_semantics=(pltpu.PARALLEL, pltpu.ARBITRARY),
        )(i_hbm, o_hbm)

    return gather_kernel(x, idx.reshape(1, n))


def _ffn_kernel(be_ref, slot_ref, nxt_ref, nu_ref, x_ref, wgu_hbm, bgu_ref, wdn_hbm, bdn_ref, y_ref,
                wgu_f, wdn_f, wgu_b, wdn_b, sem, *, layer):
    j = pl.program_id(0)
    e = be_ref[j]
    slot = slot_ref[j]
    used = j < nu_ref[0]
    first = used & ((j == 0) | (e != be_ref[jnp.maximum(j - 1, 0)]))

    def weight_copies(expert, s):
        return (pltpu.make_async_copy(wgu_hbm.at[layer, expert], wgu_f.at[s], sem.at[0, s]),
                pltpu.make_async_copy(wdn_hbm.at[layer, expert], wdn_f.at[s], sem.at[1, s]))

    @pl.when(first)
    def _():
        @pl.when(j == 0)
        def _():
            for cp in weight_copies(e, slot):
                cp.start()

        for cp in weight_copies(e, slot):
            cp.wait()
        wgu_b[...] = wgu_f[slot].astype(BF16)
        wdn_b[...] = wdn_f[slot].astype(BF16)
        nxt = nxt_ref[j]

        @pl.when(nxt >= 0)
        def _():
            for cp in weight_copies(nxt, 1 - slot):
                cp.start()

    @pl.when(used)
    def _():
        xp = x_ref[...]
        half = xp.shape[1]
        x_lo, x_hi = _unpack_bf16_pairs(xp)
        gu = (jnp.dot(x_lo, wgu_b[0:half, :], preferred_element_type=F32)
              + jnp.dot(x_hi, wgu_b[half:, :], preferred_element_type=F32) + bgu_ref[...])
        g = jnp.minimum(gu[:, 0:D_FF], SWIGLU_LIMIT)
        u = jnp.clip(gu[:, D_FF:], -SWIGLU_LIMIT, SWIGLU_LIMIT)
        act = g * jax.nn.sigmoid(SWIGLU_ALPHA * g)
        hmid = ((u + 1.0) * act).astype(BF16)
        y = jnp.dot(hmid, wdn_b[...], preferred_element_type=F32) + bdn_ref[...]
        y_ref[...] = _pack_bf16_pairs(y[:, 0:half], y[:, half:])

    @pl.when(j >= nu_ref[0])
    def _():
        y_ref[...] = jnp.zeros_like(y_ref)


def _ffn_call(layer, blk_e, blk_slot, blk_next, n_used, xs, w_gu, b_gu, w_dn, b_dn):
    n_blocks = blk_e.shape[0]
    d = 2 * xs.shape[1]
    bm = MOE_BLOCK
    pre = lambda f: (lambda j, be, sl, nx, nu: f(j, be))
    grid_spec = pltpu.PrefetchScalarGridSpec(
        num_scalar_prefetch=4,
        grid=(n_blocks,),
        in_specs=[
            pl.BlockSpec((bm, d // 2), pre(lambda j, be: (j, 0))),
            pl.BlockSpec(memory_space=pl.ANY),
            pl.BlockSpec((None, None, 1, 2 * D_FF), pre(lambda j, be: (layer, be[j], 0, 0))),
            pl.BlockSpec(memory_space=pl.ANY),
            pl.BlockSpec((None, None, 1, d), pre(lambda j, be: (layer, be[j], 0, 0))),
        ],
        out_specs=pl.BlockSpec((bm, d // 2), pre(lambda j, be: (j, 0))),
        scratch_shapes=[pltpu.VMEM((2, d, 2 * D_FF), F32), pltpu.VMEM((2, D_FF, d), F32),
                        pltpu.VMEM((d, 2 * D_FF), BF16), pltpu.VMEM((D_FF, d), BF16),
                        pltpu.SemaphoreType.DMA((2, 2))],
    )
    return pl.pallas_call(
        functools.partial(_ffn_kernel, layer=layer),
        grid_spec=grid_spec,
        out_shape=jax.ShapeDtypeStruct((n_blocks * bm, d // 2), I32),
        compiler_params=_cparams(("arbitrary",)),
        name="expert_ffn",
    )(blk_e, blk_slot, blk_next, n_used, xs, w_gu, b_gu, w_dn, b_dn)


def _combine_kernel(y1_ref, yk_ref, tw_ref, mod_ref, gf_ref, o_ref, *, final):
    tw8 = tw_ref[...]
    tw = jnp.concatenate([tw8, jnp.zeros((SMALL_W - ROUTE_ROWS, tw8.shape[1]), F32)], axis=0).T
    acc_lo, acc_hi = None, None
    for kk in range(TOP_K):
        lo, hi = _unpack_bf16_pairs(yk_ref[kk], F32)
        wk = tw[:, kk:kk + 1]
        acc_lo = wk * lo if acc_lo is None else acc_lo + wk * lo
        acc_hi = wk * hi if acc_hi is None else acc_hi + wk * hi
    y2 = y1_ref[...] + mod_ref[5] * jnp.concatenate([acc_lo, acc_hi], axis=1)
    if final:
        y2 = y2 * lax.rsqrt(jnp.mean(y2 * y2, axis=-1, keepdims=True) + NORM_EPS) * gf_ref[...]
    o_ref[...] = y2


def _combine_call(y1, yk, tw, mod, g_final, tm, seq_len, y_row0, n_rows, yk_row0, tw_row0, final):
    n, d = y1.shape
    tiles_per_seq = max(seq_len // tm, 1)
    y_off, yk_off, tw_off = y_row0 // tm, yk_row0 // tm, tw_row0 // tm
    return pl.pallas_call(
        functools.partial(_combine_kernel, final=final),
        grid=(n_rows // tm,),
        in_specs=[
            pl.BlockSpec((tm, d), lambda i: (i + y_off, 0)),
            pl.BlockSpec((TOP_K, tm, d // 2), lambda i: (0, i + yk_off, 0)),
            pl.BlockSpec((ROUTE_ROWS, tm), lambda i: (0, i + tw_off)),
            _mod_spec(mod, tm, tiles_per_seq, tile_off=y_off),
            pl.BlockSpec((1, d), lambda i: (0, 0)),
        ],
        out_specs=pl.BlockSpec((tm, d), lambda i: (i + y_off, 0)),
        out_shape=jax.ShapeDtypeStruct((n, d), F32),
        input_output_aliases={0: 0},
        compiler_params=_cparams(("arbitrary",)),
        name="moe_combine",
    )(y1, yk, tw, mod, g_final)


def _block_diag(w):
    nb, bw, _ = w.shape
    eye = jnp.eye(nb, dtype=w.dtype)
    return (eye[:, None, :, None] * w[:, :, None, :]).reshape(nb * bw, nb * bw)


def _layer_weights(l, w_in, b_forget, conv_w, conv_b, lru_lambda, lru_wa, lru_ba, lru_wx, lru_bx,
                   gla_w2, gla_b2, gla_gnorm, w_out, w_router, b_router):
    wi = w_in[l]
    d = wi.shape[0]
    w_ff = wi[:, _O_FF:_O_LX]
    w_ga = wi[:, _O_GA:_O_GOG]
    w_rec = jnp.concatenate([wi[:, _O_LX:_O_GA], wi[:, _O_GOG:]], axis=1)
    w2 = jnp.zeros((SMALL_W, GLA_W), F32).at[FOX_HEADS:FOX_HEADS + GLA_RANK].set(gla_w2[l])
    inw = {
        "w_q": wi[:, _O_FQ:_O_FK].astype(BF16),
        "w_qx": jnp.pad(wi[:, _O_FQ:_O_FK].reshape(d, FOX_HEADS, HEAD_DIM),
                        ((0, 0), (0, 0), (0, FOX_BLK - HEAD_DIM))).reshape(d, FOX_QX).astype(BF16),
        "w_kvt": wi[:, _O_FK:_O_FF].T.astype(BF16),
        "w_rec": w_rec.astype(BF16),
        "w_sm": jnp.concatenate([w_ff, w_ga, jnp.zeros((d, SMALL_W - FOX_HEADS - GLA_RANK), F32)], axis=1).astype(BF16),
        "w_smt": jnp.concatenate([w_ff, w_ga, jnp.zeros((d, SMALL_T - FOX_HEADS - GLA_RANK), F32)], axis=1).T.astype(BF16),
        "b_sm": jnp.zeros((1, SMALL_W), F32).at[0, 0:FOX_HEADS].set(b_forget[l]),
        "b_fc": b_forget[l].reshape(FOX_HEADS, 1),
        "w2": w2.astype(BF16),
        "b2": gla_b2[l].reshape(1, GLA_W),
        "w_gt": w_rec[:, 2 * GLA_W:].T.astype(BF16),
        "w2t": w2[0:SMALL_T].T.astype(BF16),
        "b2c": gla_b2[l].reshape(GLA_W, 1),
    }
    lw = {
        "conv_w": conv_w[l], "conv_b": conv_b[l].reshape(1, LRU_W),
        "wa": _block_diag(lru_wa[l]).astype(BF16), "ba": lru_ba[l].reshape(1, LRU_W),
        "wx": _block_diag(lru_wx[l]).astype(BF16), "bx": lru_bx[l].reshape(1, LRU_W),
        "lam": lru_lambda[l].reshape(1, LRU_W),
    }
    ow = {
        "gg": gla_gnorm[l].reshape(1, GLA_W),
        "ggc": gla_gnorm[l].reshape(GLA_W, 1),
        "w_out": w_out[l].astype(BF16),
        "w_r": jnp.concatenate([w_router[l], jnp.zeros((d, SMALL_W - N_EXPERTS), F32)], axis=1).astype(BF16),
        "b_r": jnp.zeros((1, SMALL_W), F32).at[0, 0:N_EXPERTS].set(b_router[l]),
    }
    return inw, lw, ow


def kernel(x_prompt, x_sample, cache_fox_k, cache_fox_v, cache_fox_logf, state_conv, state_lru, state_gla, page_table, c_prompt, c_sample, w_ada, b_ada, g_norm1, g_norm2, w_in, b_forget, conv_w, conv_b, lru_lambda, lru_wa, lru_ba, lru_wx, lru_bx, gla_w2, gla_b2, gla_gnorm, w_out, w_router, b_router, w_gu, b_gu, w_down, b_down, g_final):
    n_layers = w_ada.shape[0]
    bp, seq, d = x_prompt.shape
    bs = x_sample.shape[0]
    n_p = bp * seq
    n_tot = n_p + bs
    n_pool = cache_fox_k.shape[1]
    bm = MOE_BLOCK

    mod = _ada_call(jnp.concatenate([c_prompt, c_sample], axis=0), w_ada, b_ada)
    mod_p = mod[:, :bp].reshape(n_layers, bp, 6, 1, d).transpose(0, 2, 1, 3, 4)
    mod_s = mod[:, bp:].reshape(n_layers, 1, bs, 6, d).transpose(0, 3, 1, 2, 4)

    ckt = cache_fox_k.transpose(0, 1, 3, 4, 2).reshape(n_layers, n_pool, FOX_W, PAGE_SIZE)
    cvt = cache_fox_v.transpose(0, 1, 3, 4, 2).reshape(n_layers, n_pool, FOX_W, PAGE_SIZE)
    cft = cache_fox_logf.transpose(0, 1, 3, 2)
    conv_t = state_conv.transpose(0, 2, 1, 3)
    s_view = state_gla.transpose(0, 2, 3, 4, 1)
    b_gu4 = b_gu.reshape(n_layers, N_EXPERTS, 1, 2 * D_FF)
    b_dn4 = b_down.reshape(n_layers, N_EXPERTS, 1, d)
    gf = g_final.reshape(1, d)

    row_quant = SC_WINDOW * SC_WORKERS // TOP_K
    n_buf = -(-n_tot // row_quant) * row_quant
    n_a = (n_p // 2) // row_quant * row_quant
    parts = ((0, n_a, n_a), (n_a, n_buf - n_a, n_tot - n_a))

    def moe_scatter(l, h2, ti, part):
        row0, n_part, n_valid = part
        n_blocks = -(-(n_valid * TOP_K + N_EXPERTS * (bm - 1)) // bm)
        trash = n_blocks * bm
        dest, cnt = _rank_call(ti, row0, n_part, n_valid, trash, 512)
        counts = cnt[:, 0].astype(I32)
        pad_end = jnp.cumsum((counts + bm - 1) // bm * bm)
        blk_e = jnp.minimum(jnp.sum((jnp.arange(n_blocks, dtype=I32)[:, None] * bm >= pad_end[None, :]).astype(I32),
                                    axis=1), N_EXPERTS - 1).astype(I32)
        n_used = (pad_end[-1] // bm).astype(I32).reshape(1)
        ids = jnp.arange(N_EXPERTS, dtype=I32)
        present = counts > 0
        e_slot = (jnp.cumsum(present.astype(I32)) - 1) % 2
        later = jnp.where(present[None, :] & (ids[None, :] > ids[:, None]), ids[None, :], N_EXPERTS)
        e_next = jnp.min(later, axis=1)
        e_next = jnp.where(e_next < N_EXPERTS, e_next, -1)
        onehot = (blk_e[:, None] == ids[None, :]).astype(I32)
        blk_slot = jnp.sum(onehot * e_slot[None, :], axis=1).astype(I32)
        blk_next = jnp.sum(onehot * e_next[None, :], axis=1).astype(I32)
        dest_km = dest[0:TOP_K]
        xs = _sc_scatter_rows(h2, dest_km, (n_blocks + 1) * bm, row0)
        return xs, (blk_e, blk_slot, blk_next, n_used), jnp.minimum(dest_km, trash - 1)

    def moe_experts(l, sc):
        xs, tables, src = sc
        ye = _ffn_call(l, *tables, xs, w_gu, b_gu4, w_down, b_dn4)
        return _sc_gather_rows(ye, src.reshape(-1)).reshape(TOP_K, src.shape[1], d // 2)

    yp = x_prompt.reshape(n_p, d)
    ys = x_sample.reshape(bs, d)
    kv_p = (jnp.zeros((n_layers, bp, FOX_W, seq), F32), jnp.zeros((n_layers, bp, FOX_W, seq), F32),
            jnp.zeros((n_layers, bp, FOX_HEADS, seq), F32))
    outs_p = [[] for _ in range(3)]
    outs_s = [[] for _ in range(6)]
    for l in range(n_layers):
        inw, lw, ow = _layer_weights(l, w_in, b_forget, conv_w, conv_b, lru_lambda, lru_wa, lru_ba, lru_wx,
                                     lru_bx, gla_w2, gla_b2, gla_gnorm, w_out, w_router, b_router)
        g1 = g_norm1[l].reshape(1, d)
        g2 = g_norm2[l].reshape(1, d)

        kt_p, vt_p, lft_p, rz, qx, kx, vx = _inproj_call(yp, mod_p[l], g1, inw, seq, 256, l, n_layers, kv_bufs=kv_p)
        kv_p = (kt_p, vt_p, lft_p)
        fo = _fox_call(qx, kx, vx, bp, seq, 512)
        lo, conv_p, hlast_p = _lru_call(rz, lw, bp, seq, 256)
        go, st_p = _gla_call(rz, ow["gg"], bp, seq, 256)
        y1p, h2, ti, tw = _outproj_call(yp, fo, lo, go, mod_p[l], g2, ow["w_out"], ow["w_r"], ow["b_r"],
                                        256, seq, n_buf, 0)
        sc_a = moe_scatter(l, h2, ti, parts[0])
        st_p = st_p.reshape(bp, GLA_HEADS, HEAD_DIM, GLA_HEADS, HEAD_DIM)
        st_p = jnp.stack([st_p[:, hh, :, hh, :] for hh in range(GLA_HEADS)], axis=1).transpose(0, 1, 3, 2)
        outs_p[0].append(conv_p)
        outs_p[1].append(hlast_p.reshape(bp, LRU_W))
        outs_p[2].append(st_p)

        kts, vts, lfts, rzs, qs, ks, vs, logfs, gts = _inproj_call(
            ys, mod_s[l], g1, inw, bs, bs, 0, 1, sample=True)
        fos = _fox_dec_call(l, page_table, qs, ks, vs, logfs, ckt, cvt, cft)
        los, conv_s, h_s = _lru_step_call(l, rzs, conv_t, state_lru, lw)
        gost, s_s = _gla_step_call(l, gts, s_view, ow["ggc"])
        y1s, h2, ti, tw = _outproj_call(ys, fos, los, gost.T, mod_s[l], g2, ow["w_out"], ow["w_r"], ow["b_r"],
                                        bs, bs, n_buf, n_p, prev=(h2, ti, tw))
        outs_s[0].append(kts[0, 0])
        outs_s[1].append(vts[0, 0])
        outs_s[2].append(lfts[0, 0])
        outs_s[3].append(conv_s)
        outs_s[4].append(h_s)
        outs_s[5].append(s_s)

        sc_b = moe_scatter(l, h2, ti, parts[1])
        yk_a = moe_experts(l, sc_a)
        yk_b = moe_experts(l, sc_b)
        final = l == n_layers - 1
        yp = _combine_call(y1p, yk_a, tw, mod_p[l], gf, 256, seq, 0, n_a, 0, 0, final)
        yp = _combine_call(yp, yk_b, tw, mod_p[l], gf, 256, seq, n_a, n_p - n_a, 0, n_a, final)
        ys = _combine_call(y1s, yk_b, tw, mod_s[l], gf, bs, bs, 0, bs, n_p - n_a, n_p, final)

    kt_p, vt_p, lft_p = kv_p
    fox_k_p = kt_p.reshape(n_layers, bp, FOX_HEADS, HEAD_DIM, seq).transpose(0, 1, 4, 2, 3)
    fox_v_p = vt_p.reshape(n_layers, bp, FOX_HEADS, HEAD_DIM, seq).transpose(0, 1, 4, 2, 3)
    fox_f_p = lft_p.transpose(0, 1, 3, 2)
    fox_k_s = jnp.stack(outs_s[0]).reshape(n_layers, FOX_HEADS, HEAD_DIM, bs).transpose(0, 3, 1, 2)[:, :, None]
    fox_v_s = jnp.stack(outs_s[1]).reshape(n_layers, FOX_HEADS, HEAD_DIM, bs).transpose(0, 3, 1, 2)[:, :, None]
    fox_f_s = jnp.stack(outs_s[2]).transpose(0, 2, 1)[:, :, None]
    return (yp.reshape(bp, seq, d), ys.reshape(bs, 1, d),
            fox_k_p, fox_v_p, fox_f_p,
            jnp.stack(outs_p[0]), jnp.stack(outs_p[1]), jnp.stack(outs_p[2]),
            fox_k_s, fox_v_s, fox_f_s,
            jnp.stack(outs_s[3]).transpose(0, 2, 1, 3), jnp.stack(outs_s[4]),
            jnp.stack(outs_s[5]).transpose(0, 4, 1, 2, 3))
```

```python
import functools

import jax
import jax.numpy as jnp
from jax import lax
from jax.experimental import pallas as pl
from jax.experimental.pallas import tpu as pltpu
from jax.experimental.pallas import tpu_sc as plsc

F32 = jnp.float32
BF16 = jnp.bfloat16
I32 = jnp.int32

D_MODEL = 1024
HEAD_DIM = 64
FOX_W = 512
FOX_HEADS = 8
LRU_W = 256
GLA_W = 256
GLA_HEADS = 4
GLA_RANK = 16
GLA_CHUNK = 64
GLA_TAU = 16.0
LRU_C = 8.0
CONV_W = 4
N_EXPERTS = 32
TOP_K = 4
D_FF = 1024
SWIGLU_LIMIT = 7.0
SWIGLU_ALPHA = 1.702
NORM_EPS = 1e-6
PAGE_SIZE = 128
QK_SCALE = HEAD_DIM ** -0.5

_O_FQ, _O_FK, _O_FV, _O_FF = 0, 512, 1024, 1536
_O_LX, _O_LG, _O_GQ, _O_GK, _O_GV, _O_GA, _O_GOG = 1544, 1800, 2056, 2312, 2568, 2824, 2840
REC_W = 1792
GLA_T_ROWS = 5 * GLA_W
SMALL_W = 128
SMALL_T = 32
ROUTE_ROWS = 8
FOX_BLK = 2 * HEAD_DIM
FOX_QX = FOX_HEADS * FOX_BLK

VMEM_LIMIT = 56 * 1024 * 1024
MOE_BLOCK = 256
SC_WINDOW = 128
SC_COLS = 256
SC_WORKERS = 32

_NT = (((1,), (1,)), ((), ()))
_TN = (((0,), (0,)), ((), ()))


def _cparams(sem, vmem=VMEM_LIMIT):
    return pltpu.CompilerParams(dimension_semantics=sem, vmem_limit_bytes=vmem)


def _log_sigmoid(x):
    return jnp.minimum(x, 0.0) - jnp.log1p(jnp.exp(-jnp.abs(x)))


def _softplus(x):
    return jnp.maximum(x, 0.0) + jnp.log1p(jnp.exp(-jnp.abs(x)))


def _cumsum(x, axis):
    n = x.shape[axis]
    idx = lax.broadcasted_iota(I32, x.shape, axis)
    s = 1
    while s < n:
        x = x + jnp.where(idx >= s, pltpu.roll(x, s, axis), 0.0)
        s *= 2
    return x


_HI16 = -65536


def _pack_bf16_pairs(lo, hi):
    lo_bits = lax.bitcast_convert_type(lo.astype(BF16).astype(F32), I32)
    hi_bits = lax.bitcast_convert_type(hi.astype(BF16).astype(F32), I32)
    return lax.shift_right_logical(lo_bits, jnp.full(lo_bits.shape, 16, I32)) | (hi_bits & _HI16)


def _unpack_bf16_pairs(packed, dtype=BF16):
    lo = lax.bitcast_convert_type(lax.shift_left(packed, jnp.full(packed.shape, 16, I32)), F32)
    hi = lax.bitcast_convert_type(packed & _HI16, F32)
    return lo.astype(dtype), hi.astype(dtype)


def _mod_spec(mod, tm, tiles_per_seq, last_tile=None, tile_off=0):
    d = mod.shape[-1]
    clamp = (lambda i: i + tile_off) if last_tile is None else (lambda i: jnp.minimum(i, last_tile))
    if mod.shape[2] != 1:
        return pl.BlockSpec((6, None, tm, d), lambda i: (0, 0, clamp(i), 0))
    return pl.BlockSpec((6, None, 1, d), lambda i: (0, clamp(i) // tiles_per_seq, 0, 0))


def _ada_kernel(c_ref, w_ref, b_ref, o_ref):
    c = c_ref[...]
    a = (c * jax.nn.sigmoid(c)).astype(BF16)
    o_ref[...] = jnp.dot(a, w_ref[...].astype(BF16), preferred_element_type=F32) + b_ref[...]


def _ada_call(c_all, w_ada, b_ada):
    n_layers, d, w = w_ada.shape
    r = c_all.shape[0]
    tn = 1536
    return pl.pallas_call(
        _ada_kernel,
        grid=(n_layers, w // tn),
        in_specs=[
            pl.BlockSpec((r, d), lambda l, j: (0, 0)),
            pl.BlockSpec((None, d, tn), lambda l, j: (l, 0, j)),
            pl.BlockSpec((None, 1, tn), lambda l, j: (l, 0, j)),
        ],
        out_specs=pl.BlockSpec((None, r, tn), lambda l, j: (l, 0, j)),
        out_shape=jax.ShapeDtypeStruct((n_layers, r, w), F32),
        compiler_params=_cparams(("arbitrary", "arbitrary")),
        name="ada_mod",
    )(c_all, w_ada, b_ada.reshape(n_layers, 1, w))


def _inproj_kernel(*refs, tiles_per_seq, sample, n_alias):
    (y_ref, mod_ref, g_ref, wq_ref, wkvt_ref, wrec_ref, wsm_ref, wsmt_ref,
     bsm_ref, bfc_ref, w2_ref, b2_ref) = refs[:12]
    if sample:
        wgt_ref, w2t_ref, b2c_ref = refs[12:15]
        pos = 15 + n_alias
    else:
        eq_ref, oneq_ref, ek_ref, onek_ref = refs[12:16]
        pos = 16 + n_alias
    kt_ref, vt_ref, lft_ref, rz_ref = refs[pos:pos + 4]
    pos += 4
    if sample:
        q_ref, k_ref, v_ref, logf_ref, gt_ref = refs[pos:pos + 5]
        pos += 5
    else:
        qx_ref, kx_ref, vx_ref = refs[pos:pos + 3]
        pos += 3
    carry_c, carry_r = refs[pos:pos + 2]
    i = pl.program_id(0)

    @pl.when(i % tiles_per_seq == 0)
    def _():
        carry_c[...] = jnp.zeros_like(carry_c)
        carry_r[...] = jnp.zeros_like(carry_r)

    x = y_ref[...]
    xn = x * lax.rsqrt(jnp.mean(x * x, axis=-1, keepdims=True) + NORM_EPS) * g_ref[...]
    h = (xn * (1.0 + mod_ref[1]) + mod_ref[0]).astype(BF16)

    kt = lax.dot_general(wkvt_ref[0:FOX_W, :], h, _NT, preferred_element_type=F32)
    vt = lax.dot_general(wkvt_ref[FOX_W:2 * FOX_W, :], h, _NT, preferred_element_type=F32)
    kt_ref[...] = kt
    vt_ref[...] = vt
    rz_ref[:, 0:REC_W - GLA_W] = jnp.dot(h, wrec_ref[...], preferred_element_type=F32)

    sm = jnp.dot(h, wsm_ref[...], preferred_element_type=F32)
    lane = lax.broadcasted_iota(I32, sm.shape, 1)
    logf = jnp.where(lane < FOX_HEADS, _log_sigmoid(sm + bsm_ref[...]), 0.0)
    cum = _cumsum(logf, 0) + carry_c[...]
    carry_c[...] = cum[cum.shape[0] - 1:, :]
    glin =jnp.dot(sm.astype(BF16), w2_ref[...], preferred_element_type=F32) + b2_ref[...]
    rz_ref[:, REC_W - GLA_W:REC_W] = _log_sigmoid(glin) * (1.0 / GLA_TAU)

    smt = lax.dot_general(wsmt_ref[...], h, _NT, preferred_element_type=F32)
    lft = _log_sigmoid(smt[0:FOX_HEADS, :] + bfc_ref[...])
    lft_ref[...] = lft
    cumt = _cumsum(lft, 1) + carry_r[...]
    carry_r[...] = cumt[:, cumt.shape[1] - 1:]

    q = jnp.dot(h, wq_ref[...], preferred_element_type=F32) * QK_SCALE
    if not sample:
        def split3(c):
            hi = c.astype(BF16)
            r1 = c - hi.astype(F32)
            mid = r1.astype(BF16)
            lo = (r1 - mid.astype(F32)).astype(BF16)
            return [hi, mid, lo]

        cq = jnp.concatenate(split3(cum), axis=1)
        qx_ref[...] = (q + jnp.dot(cq, eq_ref[...], preferred_element_type=F32) + oneq_ref[...]).astype(BF16)
        ck = jnp.concatenate(split3(cumt) + [jnp.zeros(cumt.shape, BF16)], axis=0)
        kb = jnp.dot(ek_ref[...], ck, preferred_element_type=F32) + onek_ref[...]
        parts = []
        for hh in range(FOX_HEADS):
            parts += [kt[hh * HEAD_DIM:(hh + 1) * HEAD_DIM, :], kb[hh * HEAD_DIM:(hh + 1) * HEAD_DIM, :]]
        kx_ref[...] = jnp.concatenate(parts, axis=0).astype(BF16)
        vx_ref[...] = vt.astype(BF16)

    if sample:
        q_ref[...] = q.astype(BF16)
        k_ref[...] = kt.T
        v_ref[...] = vt.T
        logf_ref[...] = logf[:, 0:FOX_HEADS]
        gt_ref[0:4 * GLA_W, :] = lax.dot_general(wgt_ref[...], h, _NT, preferred_element_type=F32)
        glt = jnp.dot(w2t_ref[...], smt.astype(BF16), preferred_element_type=F32) + b2c_ref[...]
        gt_ref[4 * GLA_W:GLA_T_ROWS, :] = _log_sigmoid(glt) * (1.0 / GLA_TAU)


def _inproj_call(y, mod, g, wts, seq_len, tm, layer, n_layers, kv_bufs=None, sample=False):
    n, d = y.shape
    tiles_per_seq = seq_len // tm
    n_seq = n // seq_len
    qw = FOX_W if sample else FOX_QX
    const = lambda i: (0, 0)
    row = lambda i: (i, 0)
    seq_t = lambda i: (layer, i // tiles_per_seq, 0, i % tiles_per_seq)
    seq_t3 = lambda i: (i // tiles_per_seq, 0, i % tiles_per_seq)
    in_specs = [
        pl.BlockSpec((tm, d), row),
        _mod_spec(mod, tm, tiles_per_seq),
        pl.BlockSpec((1, d), const),
        pl.BlockSpec((d, qw), const),
        pl.BlockSpec((2 * FOX_W, d), const),
        pl.BlockSpec((d, REC_W - GLA_W), const),
        pl.BlockSpec((d, SMALL_W), const),
        pl.BlockSpec((SMALL_T, d), const),
        pl.BlockSpec((1, SMALL_W), const),
        pl.BlockSpec((FOX_HEADS, 1), const),
        pl.BlockSpec((SMALL_W, GLA_W), const),
        pl.BlockSpec((1, GLA_W), const),
    ]
    args = [y, mod, g, wts["w_q" if sample else "w_qx"], wts["w_kvt"], wts["w_rec"], wts["w_sm"], wts["w_smt"],
            wts["b_sm"], wts["b_fc"], wts["w2"], wts["b2"]]
    if sample:
        in_specs += [pl.BlockSpec((4 * GLA_W, d), const), pl.BlockSpec((GLA_W, SMALL_T), const),
                     pl.BlockSpec((GLA_W, 1), const)]
        args += [wts["w_gt"], wts["w2t"], wts["b2c"]]
    else:
        in_specs += [pl.BlockSpec((3 * SMALL_W, FOX_QX), const), pl.BlockSpec((1, FOX_QX), const),
                     pl.BlockSpec((FOX_W, SMALL_T), const), pl.BlockSpec((FOX_W, 1), const)]
        args += list(_bias_fold_consts())
    aliases = {}
    n_alias = 0
    if kv_bufs is not None:
        n_alias = 3
        first = len(args)
        in_specs += [pl.BlockSpec(memory_space=pl.ANY)] * 3
        args += list(kv_bufs)
        aliases = {first: 0, first + 1: 1, first + 2: 2}
    out_specs = [
        pl.BlockSpec((None, None, FOX_W, tm), seq_t),
        pl.BlockSpec((None, None, FOX_W, tm), seq_t),
        pl.BlockSpec((None, None, FOX_HEADS, tm), seq_t),
        pl.BlockSpec((tm, REC_W), row),
    ]
    out_shape = [
        jax.ShapeDtypeStruct((n_layers, n_seq, FOX_W, seq_len), F32),
        jax.ShapeDtypeStruct((n_layers, n_seq, FOX_W, seq_len), F32),
        jax.ShapeDtypeStruct((n_layers, n_seq, FOX_HEADS, seq_len), F32),
        jax.ShapeDtypeStruct((n, REC_W), F32),
    ]
    if sample:
        out_specs += [pl.BlockSpec((tm, FOX_W), row), pl.BlockSpec((tm, FOX_W), row), pl.BlockSpec((tm, FOX_W), row),
                      pl.BlockSpec((tm, FOX_HEADS), row), pl.BlockSpec((GLA_T_ROWS, tm), lambda i: (0, i))]
        out_shape += [jax.ShapeDtypeStruct((n, FOX_W), BF16), jax.ShapeDtypeStruct((n, FOX_W), F32),
                      jax.ShapeDtypeStruct((n, FOX_W), F32), jax.ShapeDtypeStruct((n, FOX_HEADS), F32),
                      jax.ShapeDtypeStruct((GLA_T_ROWS, n), F32)]
    else:
        out_specs += [pl.BlockSpec((tm, FOX_QX), row), pl.BlockSpec((None, FOX_QX, tm), seq_t3),
                      pl.BlockSpec((None, FOX_W, tm), seq_t3)]
        out_shape += [jax.ShapeDtypeStruct((n, FOX_QX), BF16), jax.ShapeDtypeStruct((n_seq, FOX_QX, seq_len), BF16),
                      jax.ShapeDtypeStruct((n_seq, FOX_W, seq_len), BF16)]
    return pl.pallas_call(
        functools.partial(_inproj_kernel, tiles_per_seq=tiles_per_seq, sample=sample, n_alias=n_alias),
        grid=(n // tm,),
        in_specs=in_specs,
        out_specs=tuple(out_specs),
        out_shape=tuple(out_shape),
        input_output_aliases=aliases,
        scratch_shapes=[pltpu.VMEM((1, SMALL_W), F32), pltpu.VMEM((FOX_HEADS, 1), F32)],
        compiler_params=_cparams(("arbitrary",)),
        name="in_proj",
    )(*args)


def _bias_fold_consts():
    h = jnp.arange(FOX_HEADS)
    eq = jnp.zeros((3 * SMALL_W, FOX_QX), F32)
    ek = jnp.zeros((FOX_W, SMALL_T), F32)
    oneq = jnp.zeros((1, FOX_QX), F32)
    onek = jnp.zeros((FOX_W, 1), F32)
    for piece in range(3):
        eq = eq.at[piece * SMALL_W + h, h * FOX_BLK + HEAD_DIM + piece].set(1.0)
        oneq = oneq.at[0, h * FOX_BLK + HEAD_DIM + 3 + piece].set(1.0)
        onek = onek.at[h * HEAD_DIM + piece, 0].set(1.0)
        ek = ek.at[h * HEAD_DIM + 3 + piece, piece * FOX_HEADS + h].set(-1.0)
    return eq.astype(BF16), oneq, ek.astype(BF16), onek


def _fox_kernel(qi_ref, ki_ref, qx_ref, kx_ref, vx_ref, o_ref, m_sc, l_sc, acc_sc, *, tq, tk):
    qi = qi_ref[pl.program_id(2)]
    ki = ki_ref[pl.program_id(2)]
    w = 2 * HEAD_DIM
    nc = tk // w

    @pl.when(ki == 0)
    def _():
        m_sc[...] = jnp.full_like(m_sc, -jnp.inf)
        l_sc[...] = jnp.zeros_like(l_sc)
        acc_sc[...] = jnp.zeros_like(acc_sc)

    def step(masked):
        vt = vx_ref[...]
        if masked:
            causal = lax.broadcasted_iota(I32, (tq, tk), 1) <= lax.broadcasted_iota(I32, (tq, tk), 0)
        for j in range(2):
            s = jnp.dot(qx_ref[:, j * FOX_BLK:(j + 1) * FOX_BLK], kx_ref[j * FOX_BLK:(j + 1) * FOX_BLK, :],
                        preferred_element_type=F32)
            if masked:
                s = jnp.where(causal, s, -jnp.inf)
            sc = [s[:, c * w:(c + 1) * w] for c in range(nc)]
            mb = sc[0]
            for c in range(1, nc):
                mb = jnp.maximum(mb, sc[c])
            m_prev = m_sc[j]
            m_new = jnp.maximum(m_prev, jnp.broadcast_to(jnp.max(mb, axis=1, keepdims=True), (tq, w)))
            ps = [jnp.exp(sc[c] - m_new) for c in range(nc)]
            lsum = ps[0]
            for c in range(1, nc):
                lsum = lsum + ps[c]
            alpha = jnp.exp(m_prev - m_new)
            l_sc[j] = alpha * l_sc[j] + jnp.broadcast_to(jnp.sum(lsum, axis=1, keepdims=True), (tq, w))
            p = jnp.concatenate([pc.astype(BF16) for pc in ps], axis=1)
            acc_sc[j] = alpha * acc_sc[j] + lax.dot_general(p, vt, _NT, preferred_element_type=F32)
            m_sc[j] = m_new

    @pl.when(ki < qi)
    def _():
        step(False)

    @pl.when(ki == qi)
    def _():
        step(True)
        lane = lax.broadcasted_iota(I32, (tq, w), 1)
        o0 = acc_sc[0] / l_sc[0]
        o1 = acc_sc[1] / l_sc[1]
        o_ref[...] = jnp.where(lane < HEAD_DIM, o0, o1).astype(BF16)


def _fox_call(qx, kx, vx, n_seq, seq_len, tq):
    n = qx.shape[0]
    nt = seq_len // tq
    tk = tq
    hp = FOX_HEADS // 2
    w = 2 * HEAD_DIM
    pairs = [(i, j) for i in range(nt) for j in range(i + 1)]
    qi_tab = jnp.asarray([p[0] for p in pairs], I32)
    ki_tab = jnp.asarray([p[1] for p in pairs], I32)
    past = lambda b, h, s, qt, kt: (b, h, kt[s])
    grid_spec = pltpu.PrefetchScalarGridSpec(
        num_scalar_prefetch=2,
        grid=(n_seq, hp, len(pairs)),
        in_specs=[
            pl.BlockSpec((tq, 2 * FOX_BLK), lambda b, h, s, qt, kt: (b * nt + qt[s], h)),
            pl.BlockSpec((None, 2 * FOX_BLK, tk), past),
            pl.BlockSpec((None, w, tk), past),
        ],
        out_specs=pl.BlockSpec((tq, w), lambda b, h, s, qt, kt: (b * nt + qt[s], h)),
        scratch_shapes=[pltpu.VMEM((2, tq, w), F32), pltpu.VMEM((2, tq, w), F32), pltpu.VMEM((2, tq, w), F32)],
    )
    return pl.pallas_call(
        functools.partial(_fox_kernel, tq=tq, tk=tk),
        grid_spec=grid_spec,
        out_shape=jax.ShapeDtypeStruct((n, FOX_W), BF16),
        compiler_params=_cparams(("arbitrary", "arbitrary", "arbitrary")),
        name="fox_prompt",
    )(qi_tab, ki_tab, qx, kx, vx)


def _lru_gates(xc, wa_ref, ba_ref, wx_ref, bx_ref, lam_ref):
    xb = xc.astype(BF16)
    r = jax.nn.sigmoid(jnp.dot(xb, wa_ref[...], preferred_element_type=F32) + ba_ref[...])
    gi = jax.nn.sigmoid(jnp.dot(xb, wx_ref[...], preferred_element_type=F32) + bx_ref[...])
    log_a = -LRU_C * r * _softplus(-lam_ref[...])
    a = jnp.exp(log_a)
    mult = jnp.sqrt(-jnp.tanh(log_a) * (a * a + 1.0))
    return a, mult, gi


def _lru_kernel(lx_ref, lg_ref, cw_ref, cb_ref, wa_ref, ba_ref, wx_ref, bx_ref, lam_ref,
                lo_ref, conv_ref, hlast_ref, xbuf, hcar, *, tt):
    ti = pl.program_id(1)
    nt = pl.num_programs(1)

    @pl.when(ti == 0)
    def _():
        xbuf[0:8, :] = jnp.zeros((8, LRU_W), F32)
        hcar[...] = jnp.zeros_like(hcar)

    x = lx_ref[...]
    xbuf[8:8 + tt, :] = x
    xc = cb_ref[...] + cw_ref[3:4, :] * x
    for j in range(CONV_W - 1):
        xc = xc + cw_ref[j:j + 1, :] * xbuf[5 + j:5 + j + tt, :]
    xbuf[0:8, :] = x[tt - 8:tt, :]

    a, mult, gi = _lru_gates(xc, wa_ref, ba_ref, wx_ref, bx_ref, lam_ref)
    row = lax.broadcasted_iota(I32, (tt, LRU_W), 0)
    mult = jnp.where((row == 0) & (ti == 0), 1.0, mult)
    b = mult * gi * xc
    s = 1
    while s < tt:
        keep = row >= s
        a_sh = jnp.where(keep, pltpu.roll(a, s, 0), 1.0)
        b_sh = jnp.where(keep, pltpu.roll(b, s, 0), 0.0)
        b = a * b_sh + b
        a = a * a_sh
        s *= 2
    h = a * hcar[...] + b
    hcar[...] = h[tt - 1:tt, :]
    lo_ref[...] = (h * jax.nn.gelu(lg_ref[...])).astype(BF16)

    @pl.when(ti == nt - 1)
    def _():
        conv_ref[...] = x[tt - (CONV_W - 1):tt, :]
        hlast_ref[...] = h[tt - 1:tt, :]


def _lru_weight_specs(const):
    return [
        pl.BlockSpec((CONV_W, LRU_W), const), pl.BlockSpec((1, LRU_W), const),
        pl.BlockSpec((LRU_W, LRU_W), const), pl.BlockSpec((1, LRU_W), const),
        pl.BlockSpec((LRU_W, LRU_W), const), pl.BlockSpec((1, LRU_W), const),
        pl.BlockSpec((1, LRU_W), const),
    ]


def _lru_weight_args(lw):
    return [lw["conv_w"], lw["conv_b"], lw["wa"], lw["ba"], lw["wx"], lw["bx"], lw["lam"]]


def _lru_call(rz, lw, n_seq, seq_len, tt):
    n = rz.shape[0]
    nt = seq_len // tt
    return pl.pallas_call(
        functools.partial(_lru_kernel, tt=tt),
        grid=(n_seq, nt),
        in_specs=[
            pl.BlockSpec((tt, LRU_W), lambda b, t: (b * nt + t, 0)),
            pl.BlockSpec((tt, LRU_W), lambda b, t: (b * nt + t, 1)),
        ] + _lru_weight_specs(lambda b, t: (0, 0)),
        out_specs=(
            pl.BlockSpec((tt, LRU_W), lambda b, t: (b * nt + t, 0)),
            pl.BlockSpec((None, CONV_W - 1, LRU_W), lambda b, t: (b, 0, 0)),
            pl.BlockSpec((None, 1, LRU_W), lambda b, t: (b, 0, 0)),
        ),
        out_shape=(
            jax.ShapeDtypeStruct((n, LRU_W), BF16),
            jax.ShapeDtypeStruct((n_seq, CONV_W - 1, LRU_W), F32),
            jax.ShapeDtypeStruct((n_seq, 1, LRU_W), F32),
        ),
        scratch_shapes=[pltpu.VMEM((tt + 8, LRU_W), F32), pltpu.VMEM((1, LRU_W), F32)],
        compiler_params=_cparams(("arbitrary", "arbitrary")),
        name="lru_prompt",
    )(rz, rz, *_lru_weight_args(lw))


def _head_rms_gate(o, gg_ref, gog):
    lane = lax.broadcasted_iota(I32, o.shape, 1)
    o2 = o * o
    rs = jnp.zeros_like(o)
    for hh in range(GLA_HEADS):
        mh = lane // HEAD_DIM == hh
        ms = jnp.sum(jnp.where(mh, o2, 0.0), axis=1, keepdims=True) * (1.0 / HEAD_DIM)
        rs = jnp.where(mh, lax.rsqrt(ms + NORM_EPS), rs)
    return o * rs * gg_ref[...] * (gog * jax.nn.sigmoid(gog))


def _gla_kernel(gq_ref, gk_ref, gv_ref, gog_ref, gl_ref, gg_ref, go_ref, st_ref, s_sc, *, tt):
    ti = pl.program_id(1)
    nt = pl.num_programs(1)
    c = GLA_CHUNK

    @pl.when(ti == 0)
    def _():
        s_sc[...] = jnp.zeros_like(s_sc)

    lane = lax.broadcasted_iota(I32, (c, GLA_W), 1)
    r2 = lax.broadcasted_iota(I32, (GLA_W, GLA_W), 0)
    c2 = lax.broadcasted_iota(I32, (GLA_W, GLA_W), 1)
    same_head = (r2 // HEAD_DIM) == (c2 // HEAD_DIM)
    tril = lax.broadcasted_iota(I32, (c, c), 1) <= lax.broadcasted_iota(I32, (c, c), 0)

    for ci in range(tt // c):
        sl = slice(ci * c, (ci + 1) * c)
        q = gq_ref[sl, :] * QK_SCALE
        k = gk_ref[sl, :]
        v = gv_ref[sl, :].astype(BF16)
        bc = _cumsum(gl_ref[sl, :], 0)
        b_last = bc[c - 1:c, :]
        qd = (q * jnp.exp(bc)).astype(BF16)
        kinv = (k * jnp.exp(-bc)).astype(BF16)
        kdec = (k * jnp.exp(b_last - bc)).astype(BF16)
        s_prev = s_sc[...]
        o = lax.dot_general(qd, s_prev.astype(BF16), _NT, preferred_element_type=F32)
        for hh in range(GLA_HEADS):
            mh = lane // HEAD_DIM == hh
            att = lax.dot_general(jnp.where(mh, qd, jnp.zeros_like(qd)), kinv, _NT, preferred_element_type=F32)
            att = jnp.where(tril, att, 0.0).astype(BF16)
            o = o + jnp.dot(att, jnp.where(mh, v, jnp.zeros_like(v)), preferred_element_type=F32)
        ut = lax.dot_general(v, kdec, _TN, preferred_element_type=F32)
        s_sc[...] = s_prev * jnp.exp(b_last) + jnp.where(same_head, ut, 0.0)
        go_ref[sl, :] = _head_rms_gate(o, gg_ref, gog_ref[sl, :]).astype(BF16)

    @pl.when(ti == nt - 1)
    def _():
        st_ref[...] = s_sc[...]


def _gla_call(rz, gg, n_seq, seq_len, tt):
    n = rz.shape[0]
    nt = seq_len // tt

    def col(j):
        return pl.BlockSpec((tt, GLA_W), lambda b, t: (b * nt + t, j))

    return pl.pallas_call(
        functools.partial(_gla_kernel, tt=tt),
        grid=(n_seq, nt),
        in_specs=[col(2), col(3), col(4), col(5), col(6), pl.BlockSpec((1, GLA_W), lambda b, t: (0, 0))],
        out_specs=(
            pl.BlockSpec((tt, GLA_W), lambda b, t: (b * nt + t, 0)),
            pl.BlockSpec((None, GLA_W, GLA_W), lambda b, t: (b, 0, 0)),
        ),
        out_shape=(
            jax.ShapeDtypeStruct((n, GLA_W), BF16),
            jax.ShapeDtypeStruct((n_seq, GLA_W, GLA_W), F32),
        ),
        scratch_shapes=[pltpu.VMEM((GLA_W, GLA_W), F32)],
        compiler_params=_cparams(("arbitrary", "arbitrary")),
        name="gla_prompt",
    )(rz, rz, rz, rz, rz, gg)


def _fox_dec_kernel(pt_ref, q_ref, kn_ref, vn_ref, dn_ref, *refs, n_pages):
    del pt_ref
    k_refs = refs[0:n_pages]
    v_refs = refs[n_pages:2 * n_pages]
    f_refs = refs[2 * n_pages:3 * n_pages]
    o_ref = refs[3 * n_pages]
    w = FOX_W
    hrow = lax.broadcasted_iota(I32, (FOX_HEADS, w), 0)
    hlane = lax.broadcasted_iota(I32, (FOX_HEADS, w), 1) // HEAD_DIM
    diag = hrow == hlane
    q = q_ref[...].astype(F32)
    qbd = jnp.where(diag, jnp.broadcast_to(q, (FOX_HEADS, w)), 0.0).astype(BF16)
    s = jnp.concatenate(
        [jnp.dot(qbd, k_refs[p][...].astype(BF16), preferred_element_type=F32) for p in range(n_pages)], axis=1)
    lf = jnp.concatenate([f_refs[p][...] for p in range(n_pages)], axis=1)
    cs = _cumsum(lf, 1)
    suffix = cs[:, cs.shape[1] - 1:] - cs
    s = s + dn_ref[...] + suffix
    s_new = jnp.sum(qbd.astype(F32) * kn_ref[...], axis=1, keepdims=True)
    m = jnp.maximum(jnp.max(s, axis=1, keepdims=True), s_new)
    p_past = jnp.exp(s - m)
    p_new = jnp.exp(s_new - m)
    denom = jnp.sum(p_past, axis=1, keepdims=True) + p_new
    acc = p_new * vn_ref[...]
    pb = p_past.astype(BF16)
    for p in range(n_pages):
        acc = acc + lax.dot_general(pb[:, p * PAGE_SIZE:(p + 1) * PAGE_SIZE], v_refs[p][...].astype(BF16), _NT,
                                    preferred_element_type=F32)
    out = jnp.where(diag, acc / denom, 0.0)
    o_ref[...] = jnp.sum(out, axis=0, keepdims=True).astype(BF16)


def _fox_dec_call(layer, page_table, q, k_new, v_new, logf_new, cache_kt, cache_vt, cache_ft):
    bd, n_pages = page_table.shape
    w = FOX_W

    def page_spec(rows, j):
        return pl.BlockSpec((None, None, rows, PAGE_SIZE), lambda b, pt, j=j: (layer, pt[b, j], 0, 0))

    row = lambda b, pt: (b, 0, 0)
    in_specs = [
        pl.BlockSpec((None, 1, w), row),
        pl.BlockSpec((None, 1, w), row),
        pl.BlockSpec((None, 1, w), row),
        pl.BlockSpec((None, FOX_HEADS, 1), row),
    ]
    in_specs += [page_spec(w, j) for j in range(n_pages)]
    in_specs += [page_spec(w, j) for j in range(n_pages)]
    in_specs += [page_spec(FOX_HEADS, j) for j in range(n_pages)]
    grid_spec = pltpu.PrefetchScalarGridSpec(
        num_scalar_prefetch=1,
        grid=(bd,),
        in_specs=in_specs,
        out_specs=pl.BlockSpec((None, 1, w), row),
    )
    out = pl.pallas_call(
        functools.partial(_fox_dec_kernel, n_pages=n_pages),
        grid_spec=grid_spec,
        out_shape=jax.ShapeDtypeStruct((bd, 1, w), BF16),
        compiler_params=_cparams(("arbitrary",)),
        name="fox_sample",
    )(page_table, q.reshape(bd, 1, w), k_new.reshape(bd, 1, w), v_new.reshape(bd, 1, w),
      logf_new.reshape(bd, FOX_HEADS, 1),
      *([cache_kt] * n_pages), *([cache_vt] * n_pages), *([cache_ft] * n_pages))
    return out.reshape(bd, w)


def _lru_step_kernel(lx_ref, lg_ref, conv_ref, h0_ref, cw_ref, cb_ref, wa_ref, ba_ref, wx_ref, bx_ref, lam_ref,
                     lo_ref, convn_ref, hn_ref):
    x = lx_ref[...]
    xc = cb_ref[...] + cw_ref[3:4, :] * x
    for j in range(CONV_W - 1):
        xc = xc + cw_ref[j:j + 1, :] * conv_ref[j]
    convn_ref[0] = conv_ref[1]
    convn_ref[1] = conv_ref[2]
    convn_ref[2] = x
    a, mult, gi = _lru_gates(xc, wa_ref, ba_ref, wx_ref, bx_ref, lam_ref)
    h = a * h0_ref[...] + mult * gi * xc
    hn_ref[...] = h
    lo_ref[...] = (h * jax.nn.gelu(lg_ref[...])).astype(BF16)


def _lru_step_call(layer, rz, conv_t, h0, lw):
    bd = rz.shape[0]
    return pl.pallas_call(
        _lru_step_kernel,
        grid=(1,),
        in_specs=[
            pl.BlockSpec((bd, LRU_W), lambda i: (0, 0)),
            pl.BlockSpec((bd, LRU_W), lambda i: (0, 1)),
            pl.BlockSpec((None, CONV_W - 1, bd, LRU_W), lambda i: (layer, 0, 0, 0)),
            pl.BlockSpec((None, bd, LRU_W), lambda i: (layer, 0, 0)),
        ] + _lru_weight_specs(lambda i: (0, 0)),
        out_specs=(
            pl.BlockSpec((bd, LRU_W), lambda i: (0, 0)),
            pl.BlockSpec((CONV_W - 1, bd, LRU_W), lambda i: (0, 0, 0)),
            pl.BlockSpec((bd, LRU_W), lambda i: (0, 0)),
        ),
        out_shape=(
            jax.ShapeDtypeStruct((bd, LRU_W), BF16),
            jax.ShapeDtypeStruct((CONV_W - 1, bd, LRU_W), F32),
            jax.ShapeDtypeStruct((bd, LRU_W), F32),
        ),
        compiler_params=_cparams(("arbitrary",)),
        name="lru_step",
    )(rz, rz, conv_t, h0, *_lru_weight_args(lw))


def _gla_step_kernel(q_ref, k_ref, v_ref, gog_ref, gl_ref, gg_ref, s_ref, go_ref, sn_ref):
    eg = jnp.exp(gl_ref[...])
    kt = k_ref[...]
    qt = q_ref[...] * QK_SCALE
    vt = v_ref[...]
    o = jnp.zeros_like(vt)
    for kk in range(HEAD_DIM):
        s_new = eg[kk:kk + 1, :] * s_ref[kk] + kt[kk:kk + 1, :] * vt
        sn_ref[kk] = s_new
        o = o + qt[kk:kk + 1, :] * s_new
    ms = jnp.mean(o * o, axis=0, keepdims=True)
    gog = gog_ref[...]
    go_ref[...] = o * lax.rsqrt(ms + NORM_EPS) * gg_ref[...] * (gog * jax.nn.sigmoid(gog))


def _gla_step_call(layer, gt, s_view, ggc):
    bd = gt.shape[1]
    hd = HEAD_DIM

    def part(j):
        return pl.BlockSpec((hd, bd), lambda h, j=j: (j * GLA_HEADS + h, 0))

    return pl.pallas_call(
        _gla_step_kernel,
        grid=(GLA_HEADS,),
        in_specs=[part(0), part(1), part(2), part(3), part(4),
                  pl.BlockSpec((hd, 1), lambda h: (h, 0)),
                  pl.BlockSpec((None, None, hd, hd, bd), lambda h: (layer, h, 0, 0, 0))],
        out_specs=(
            pl.BlockSpec((hd, bd), lambda h: (h, 0)),
            pl.BlockSpec((None, hd, hd, bd), lambda h: (h, 0, 0, 0)),
        ),
        out_shape=(
            jax.ShapeDtypeStruct((GLA_W, bd), F32),
            jax.ShapeDtypeStruct((GLA_HEADS, hd, hd, bd), F32),
        ),
        compiler_params=_cparams(("arbitrary",)),
        name="gla_step",
    )(gt, gt, gt, gt, gt, ggc, s_view)


def _outproj_kernel(*refs, n_alias, n_tiles):
    h2_ref, ti_ref, tw_ref = refs[10 + n_alias:13 + n_alias]
    i = pl.program_id(0)

    @pl.when(i < n_tiles)
    def _():
        _outproj_tile(*refs[0:9], *refs[9 + n_alias:13 + n_alias])

    @pl.when(i >= n_tiles)
    def _():
        h2_ref[...] = jnp.zeros_like(h2_ref)
        ti_ref[...] = jnp.zeros_like(ti_ref)
        tw_ref[...] = jnp.zeros_like(tw_ref)


def _outproj_tile(y_ref, fo_ref, lo_ref, go_ref, mod_ref, g2_ref, wo_ref, wr_ref, br_ref,
                  y1_ref, h2_ref, ti_ref, tw_ref):
    m = jnp.dot(fo_ref[...], wo_ref[0:FOX_W, :], preferred_element_type=F32)
    m = m + jnp.dot(lo_ref[...], wo_ref[FOX_W:FOX_W + LRU_W, :], preferred_element_type=F32)
    m = m + jnp.dot(go_ref[...].astype(BF16), wo_ref[FOX_W + LRU_W:, :], preferred_element_type=F32)
    y1 = y_ref[...] + mod_ref[2] * m
    y1_ref[...] = y1
    xn = y1 * lax.rsqrt(jnp.mean(y1 * y1, axis=-1, keepdims=True) + NORM_EPS) * g2_ref[...]
    h2 = xn * (1.0 + mod_ref[4]) + mod_ref[3]
    half = h2.shape[1] // 2
    h2_ref[...] = _pack_bf16_pairs(h2[:, 0:half], h2[:, half:])
    logits = jnp.dot(h2.astype(BF16), wr_ref[...], preferred_element_type=F32) + br_ref[...]
    tm = logits.shape[0]
    lt = logits.T[0:N_EXPERTS, :]
    eidx = lax.broadcasted_iota(I32, lt.shape, 0)
    sels, vals = [], []
    for kk in range(TOP_K):
        mx = jnp.max(lt, axis=0, keepdims=True)
        sel = jnp.min(jnp.where(lt == mx, eidx, N_EXPERTS), axis=0, keepdims=True)
        sels.append(sel)
        vals.append(mx)
        lt = jnp.where(eidx == sel, -jnp.inf, lt)
    es = [jnp.exp(vv - vals[0]) for vv in vals]
    tot = es[0] + es[1] + es[2] + es[3]
    row = lax.broadcasted_iota(I32, (ROUTE_ROWS, tm), 0)
    idx_out = jnp.zeros((ROUTE_ROWS, tm), I32)
    val_out = jnp.zeros((ROUTE_ROWS, tm), F32)
    for kk in range(TOP_K):
        idx_out = jnp.where(row == kk, sels[kk], idx_out)
        val_out = jnp.where(row == kk, es[kk] / tot, val_out)
    ti_ref[...] = idx_out
    tw_ref[...] = val_out


def _outproj_call(y, fo, lo, go, mod, g2, w_out, w_r, b_r, tm, seq_len, n_buf, row_off, prev=None):
    n, d = y.shape
    tiles_per_seq = max(seq_len // tm, 1)
    n_tiles = n // tm
    n_steps = n_tiles if prev is not None else n_buf // tm
    const = lambda i: (0, 0)
    row = lambda i: (jnp.minimum(i, n_tiles - 1), 0)
    off = row_off // tm
    orow = lambda i: (i + off, 0)
    in_specs = [
        pl.BlockSpec((tm, d), row),
        pl.BlockSpec((tm, FOX_W), row),
        pl.BlockSpec((tm, LRU_W), row),
        pl.BlockSpec((tm, GLA_W), row),
        _mod_spec(mod, tm, tiles_per_seq, n_tiles - 1),
        pl.BlockSpec((1, d), const),
        pl.BlockSpec((d, d), const),
        pl.BlockSpec((d, SMALL_W), const),
        pl.BlockSpec((1, SMALL_W), const),
    ]
    args = [y, fo, lo, go, mod, g2, w_out, w_r, b_r]
    aliases = {}
    if prev is not None:
        in_specs += [pl.BlockSpec(memory_space=pl.ANY)] * 3
        args += list(prev)
        aliases = {9: 1, 10: 2, 11: 3}
    return pl.pallas_call(
        functools.partial(_outproj_kernel, n_alias=0 if prev is None else 3, n_tiles=n_tiles),
        grid=(n_steps,),
        in_specs=in_specs,
        out_specs=(
            pl.BlockSpec((tm, d), row),
            pl.BlockSpec((tm, d // 2), orow),
            pl.BlockSpec((ROUTE_ROWS, tm), lambda i: (0, i + off)),
            pl.BlockSpec((ROUTE_ROWS, tm), lambda i: (0, i + off)),
        ),
        out_shape=(
            jax.ShapeDtypeStruct((n, d), F32),
            jax.ShapeDtypeStruct((n_buf, d // 2), I32),
            jax.ShapeDtypeStruct((ROUTE_ROWS, n_buf), I32),
            jax.ShapeDtypeStruct((ROUTE_ROWS, n_buf), F32),
        ),
        input_output_aliases=aliases,
        compiler_params=_cparams(("arbitrary",)),
        name="out_proj_router",
    )(*args)


def _rank_kernel(ti_ref, dest_ref, cnt_ref, carry, *, tm, n_valid, trash):
    p = pl.program_id(0)
    i = pl.program_id(1)
    w = SMALL_W

    @pl.when((p == 0) & (i == 0))
    def _():
        carry[...] = jnp.zeros_like(carry)

    @pl.when((p == 1) & (i == 0))
    def _():
        cnt = carry[...]
        cnt_ref[...] = cnt
        padded = jnp.floor((cnt + (MOE_BLOCK - 1.0)) * (1.0 / MOE_BLOCK)) * MOE_BLOCK
        carry[...] = _cumsum(padded, 0) - padded

    t = ti_ref[...]
    eidx = lax.broadcasted_iota(I32, (N_EXPERTS, tm), 0)
    valid = (lax.broadcasted_iota(I32, (1, tm), 1) + i * tm) < n_valid
    ohs = [jnp.where(valid & (eidx == t[kk:kk + 1, :]), 1.0, 0.0) for kk in range(TOP_K)]

    @pl.when(p == 0)
    def _():
        tile_cnt = jnp.sum(sum(ohs[1:], ohs[0]), axis=1, keepdims=True)
        carry[...] = carry[...] + jnp.broadcast_to(tile_cnt, (N_EXPERTS, w))

    @pl.when(p == 1)
    def _():
        earlier = (lax.broadcasted_iota(I32, (tm, tm), 0) < lax.broadcasted_iota(I32, (tm, tm), 1)).astype(BF16)
        stack = jnp.concatenate(ohs, axis=0).astype(BF16)
        pre = jnp.dot(stack, earlier, preferred_element_type=F32)
        base = carry[:, 0:1]
        row = lax.broadcasted_iota(I32, (ROUTE_ROWS, tm), 0)
        out = jnp.zeros((ROUTE_ROWS, tm), F32)
        for kk in range(TOP_K):
            oh = ohs[kk]
            slot = jnp.sum(oh * (pre[kk * N_EXPERTS:(kk + 1) * N_EXPERTS, :] + base), axis=0, keepdims=True)
            out = jnp.where(row == kk, jnp.where(valid, slot, float(trash)), out)
            base = base + jnp.sum(oh, axis=1, keepdims=True)
        carry[...] = jnp.broadcast_to(base, (N_EXPERTS, w))
        dest_ref[...] = out.astype(I32)


def _rank_call(ti, col0, n_cols, n_valid, trash, tm):
    off = col0 // tm
    return pl.pallas_call(
        functools.partial(_rank_kernel, tm=tm, n_valid=n_valid, trash=trash),
        grid=(2, n_cols // tm),
        in_specs=[pl.BlockSpec((ROUTE_ROWS, tm), lambda p, i: (0, i + off))],
        out_specs=(
            pl.BlockSpec((ROUTE_ROWS, tm), lambda p, i: (0, i * p)),
            pl.BlockSpec((N_EXPERTS, SMALL_W), lambda p, i: (0, 0)),
        ),
        out_shape=(
            jax.ShapeDtypeStruct((ROUTE_ROWS, n_cols), I32),
            jax.ShapeDtypeStruct((N_EXPERTS, SMALL_W), F32),
        ),
        scratch_shapes=[pltpu.VMEM((N_EXPERTS, SMALL_W), F32)],
        compiler_params=_cparams(("arbitrary", "arbitrary")),
        name="moe_rank",
    )(ti)


def _sc_mesh():
    return plsc.VectorSubcoreMesh(core_axis_name="core", subcore_axis_name="subcore")


def _sc_scatter_rows(x, dest_km, n_out, row0=0):
    kk, n = dest_km.shape
    d = x.shape[1]
    nb = n // SC_WINDOW
    off = row0 // SC_WINDOW
    assert (kk * nb) % SC_WORKERS == 0 and d % SC_COLS == 0 and row0 % SC_WINDOW == 0

    @pl.kernel(out_type=jax.ShapeDtypeStruct((n_out, d), x.dtype), mesh=_sc_mesh())
    def scatter_kernel(x_hbm, i_hbm, o_hbm):
        def body(x_vmem, i_vmem):
            j = pl.program_id(1)
            pltpu.sync_copy(x_vmem, o_hbm.at[i_vmem.at[0], pl.ds(j * SC_COLS, SC_COLS)])

        pltpu.emit_pipeline(
            body,
            grid=(kk * nb, d // SC_COLS),
            in_specs=[pl.BlockSpec((SC_WINDOW, SC_COLS), lambda g, j: (g % nb + off, j)),
                      pl.BlockSpec((1, SC_WINDOW), lambda g, j: (g // nb, g % nb))],
            out_specs=[],
            core_axis_name=("core", "subcore"),
            dimension_semantics=(pltpu.PARALLEL, pltpu.ARBITRARY),
        )(x_hbm, i_hbm)

    return scatter_kernel(x, dest_km)


def _sc_gather_rows(x, idx):
    n = idx.shape[0]
    d = x.shape[1]
    assert (n // SC_WINDOW) % SC_WORKERS == 0 and d % SC_COLS == 0

    @pl.kernel(out_type=jax.ShapeDtypeStruct((n, d), x.dtype), mesh=_sc_mesh())
    def gather_kernel(x_hbm, i_hbm, o_hbm):
        def body(i_vmem, o_vmem):
            j = pl.program_id(1)
            pltpu.sync_copy(x_hbm.at[i_vmem.at[0], pl.ds(j * SC_COLS, SC_COLS)], o_vmem)

        pltpu.emit_pipeline(
            body,
            grid=(n // SC_WINDOW, d // SC_COLS),
            in_specs=[pl.BlockSpec((1, SC_WINDOW), lambda i, j: (0, i))],
            out_specs=[pl.BlockSpec((SC_WINDOW, SC_COLS), lambda i, j: (i, j))],
            core_axis_name=("core", "subcore"),
            dimension_semantics=(pltpu.PARALLEL, pltpu.ARBITRARY),
        )(i_hbm, o_hbm)

    return gather_kernel(x, idx.reshape(1, n))


def _ffn_kernel(be_ref, slot_ref, nxt_ref, valid_ref, nu_ref, x_ref, wgu_hbm, bgu_ref, wdn_hbm, bdn_ref, y_ref,
                wgu_f, wdn_f, wgu_b, wdn_b, sem, *, layer):
    j = pl.program_id(0)
    e = be_ref[j]
    slot = slot_ref[j]
    used = j < nu_ref[0]
    first = used & ((j == 0) | (e != be_ref[jnp.maximum(j - 1, 0)]))

    def weight_copies(expert, s):
        return (pltpu.make_async_copy(wgu_hbm.at[layer, expert], wgu_f.at[s], sem.at[0, s]),
                pltpu.make_async_copy(wdn_hbm.at[layer, expert], wdn_f.at[s], sem.at[1, s]))

    @pl.when(first)
    def _():
        @pl.when(j == 0)
        def _():
            for cp in weight_copies(e, slot):
                cp.start()

        for cp in weight_copies(e, slot):
            cp.wait()
        wgu_b[...] = wgu_f[slot].astype(BF16)
        wdn_b[...] = wdn_f[slot].astype(BF16)
        nxt = nxt_ref[j]

        @pl.when(nxt >= 0)
        def _():
            for cp in weight_copies(nxt, 1 - slot):
                cp.start()

    def expert_rows(xp):
        half = xp.shape[1]
        x_lo, x_hi = _unpack_bf16_pairs(xp)
        gu = (jnp.dot(x_lo, wgu_b[0:half, :], preferred_element_type=F32)
              + jnp.dot(x_hi, wgu_b[half:, :], preferred_element_type=F32) + bgu_ref[...])
        g = jnp.minimum(gu[:, 0:D_FF], SWIGLU_LIMIT)
        u = jnp.clip(gu[:, D_FF:], -SWIGLU_LIMIT, SWIGLU_LIMIT)
        act = g * jax.nn.sigmoid(SWIGLU_ALPHA * g)
        hmid = ((u + 1.0) * act).astype(BF16)
        y = jnp.dot(hmid, wdn_b[...], preferred_element_type=F32) + bdn_ref[...]
        return _pack_bf16_pairs(y[:, 0:half], y[:, half:])

    bm = x_ref.shape[0]
    top = bm // 2
    n_real = valid_ref[j]

    @pl.when(used & (n_real > top))
    def _():
        y_ref[...] = expert_rows(x_ref[...])

    @pl.when(used & (n_real <= top))
    def _():
        y_ref[0:top, :] = expert_rows(x_ref[0:top, :])
        y_ref[top:bm, :] = jnp.zeros((bm - top, y_ref.shape[1]), y_ref.dtype)

    @pl.when(j >= nu_ref[0])
    def _():
        y_ref[...] = jnp.zeros_like(y_ref)


def _ffn_call(layer, blk_e, blk_slot, blk_next, blk_valid, n_used, xs, w_gu, b_gu, w_dn, b_dn):
    n_blocks = blk_e.shape[0]
    d = 2 * xs.shape[1]
    bm = MOE_BLOCK
    pre = lambda f: (lambda j, be, sl, nx, va, nu: f(j, be))
    grid_spec = pltpu.PrefetchScalarGridSpec(
        num_scalar_prefetch=5,
        grid=(n_blocks,),
        in_specs=[
            pl.BlockSpec((bm, d // 2), pre(lambda j, be: (j, 0))),
            pl.BlockSpec(memory_space=pl.ANY),
            pl.BlockSpec((None, None, 1, 2 * D_FF), pre(lambda j, be: (layer, be[j], 0, 0))),
            pl.BlockSpec(memory_space=pl.ANY),
            pl.BlockSpec((None, None, 1, d), pre(lambda j, be: (layer, be[j], 0, 0))),
        ],
        out_specs=pl.BlockSpec((bm, d // 2), pre(lambda j, be: (j, 0))),
        scratch_shapes=[pltpu.VMEM((2, d, 2 * D_FF), F32), pltpu.VMEM((2, D_FF, d), F32),
                        pltpu.VMEM((d, 2 * D_FF), BF16), pltpu.VMEM((D_FF, d), BF16),
                        pltpu.SemaphoreType.DMA((2, 2))],
    )
    return pl.pallas_call(
        functools.partial(_ffn_kernel, layer=layer),
        grid_spec=grid_spec,
        out_shape=jax.ShapeDtypeStruct((n_blocks * bm, d // 2), I32),
        compiler_params=_cparams(("arbitrary",)),
        name="expert_ffn",
    )(blk_e, blk_slot, blk_next, blk_valid, n_used, xs, w_gu, b_gu, w_dn, b_dn)


def _combine_kernel(y1_ref, yk_ref, tw_ref, mod_ref, gf_ref, o_ref, *, final):
    tw8 = tw_ref[...]
    tw = jnp.concatenate([tw8, jnp.zeros((SMALL_W - ROUTE_ROWS, tw8.shape[1]), F32)], axis=0).T
    acc_lo, acc_hi = None, None
    for kk in range(TOP_K):
        lo, hi = _unpack_bf16_pairs(yk_ref[kk], F32)
        wk = tw[:, kk:kk + 1]
        acc_lo = wk * lo if acc_lo is None else acc_lo + wk * lo
        acc_hi = wk * hi if acc_hi is None else acc_hi + wk * hi
    y2 = y1_ref[...] + mod_ref[5] * jnp.concatenate([acc_lo, acc_hi], axis=1)
    if final:
        y2 = y2 * lax.rsqrt(jnp.mean(y2 * y2, axis=-1, keepdims=True) + NORM_EPS) * gf_ref[...]
    o_ref[...] = y2


def _combine_call(y1, yk, tw, mod, g_final, tm, seq_len, y_row0, n_rows, yk_row0, tw_row0, final):
    n, d = y1.shape
    tiles_per_seq = max(seq_len // tm, 1)
    y_off, yk_off, tw_off = y_row0 // tm, yk_row0 // tm, tw_row0 // tm
    return pl.pallas_call(
        functools.partial(_combine_kernel, final=final),
        grid=(n_rows // tm,),
        in_specs=[
            pl.BlockSpec((tm, d), lambda i: (i + y_off, 0)),
            pl.BlockSpec((TOP_K, tm, d // 2), lambda i: (0, i + yk_off, 0)),
            pl.BlockSpec((ROUTE_ROWS, tm), lambda i: (0, i + tw_off)),
            _mod_spec(mod, tm, tiles_per_seq, tile_off=y_off),
            pl.BlockSpec((1, d), lambda i: (0, 0)),
        ],
        out_specs=pl.BlockSpec((tm, d), lambda i: (i + y_off, 0)),
        out_shape=jax.ShapeDtypeStruct((n, d), F32),
        input_output_aliases={0: 0},
        compiler_params=_cparams(("arbitrary",)),
        name="moe_combine",
    )(y1, yk, tw, mod, g_final)


def _block_diag(w):
    nb, bw, _ = w.shape
    eye = jnp.eye(nb, dtype=w.dtype)
    return (eye[:, None, :, None] * w[:, :, None, :]).reshape(nb * bw, nb * bw)


def _layer_weights(l, w_in, b_forget, conv_w, conv_b, lru_lambda, lru_wa, lru_ba, lru_wx, lru_bx,
                   gla_w2, gla_b2, gla_gnorm, w_out, w_router, b_router):
    wi = w_in[l]
    d = wi.shape[0]
    w_ff = wi[:, _O_FF:_O_LX]
    w_ga = wi[:, _O_GA:_O_GOG]
    w_rec = jnp.concatenate([wi[:, _O_LX:_O_GA], wi[:, _O_GOG:]], axis=1)
    w2 = jnp.zeros((SMALL_W, GLA_W), F32).at[FOX_HEADS:FOX_HEADS + GLA_RANK].set(gla_w2[l])
    inw = {
        "w_q": wi[:, _O_FQ:_O_FK].astype(BF16),
        "w_qx": jnp.pad(wi[:, _O_FQ:_O_FK].reshape(d, FOX_HEADS, HEAD_DIM),
                        ((0, 0), (0, 0), (0, FOX_BLK - HEAD_DIM))).reshape(d, FOX_QX).astype(BF16),
        "w_kvt": wi[:, _O_FK:_O_FF].T.astype(BF16),
        "w_rec": w_rec.astype(BF16),
        "w_sm": jnp.concatenate([w_ff, w_ga, jnp.zeros((d, SMALL_W - FOX_HEADS - GLA_RANK), F32)], axis=1).astype(BF16),
        "w_smt": jnp.concatenate([w_ff, w_ga, jnp.zeros((d, SMALL_T - FOX_HEADS - GLA_RANK), F32)], axis=1).T.astype(BF16),
        "b_sm": jnp.zeros((1, SMALL_W), F32).at[0, 0:FOX_HEADS].set(b_forget[l]),
        "b_fc": b_forget[l].reshape(FOX_HEADS, 1),
        "w2": w2.astype(BF16),
        "b2": gla_b2[l].reshape(1, GLA_W),
        "w_gt": w_rec[:, 2 * GLA_W:].T.astype(BF16),
        "w2t": w2[0:SMALL_T].T.astype(BF16),
        "b2c": gla_b2[l].reshape(GLA_W, 1),
    }
    lw = {
        "conv_w": conv_w[l], "conv_b": conv_b[l].reshape(1, LRU_W),
        "wa": _block_diag(lru_wa[l]).astype(BF16), "ba": lru_ba[l].reshape(1, LRU_W),
        "wx": _block_diag(lru_wx[l]).astype(BF16), "bx": lru_bx[l].reshape(1, LRU_W),
        "lam": lru_lambda[l].reshape(1, LRU_W),
    }
    ow = {
        "gg": gla_gnorm[l].reshape(1, GLA_W),
        "ggc": gla_gnorm[l].reshape(GLA_W, 1),
        "w_out": w_out[l].astype(BF16),
        "w_r": jnp.concatenate([w_router[l], jnp.zeros((d, SMALL_W - N_EXPERTS), F32)], axis=1).astype(BF16),
        "b_r": jnp.zeros((1, SMALL_W), F32).at[0, 0:N_EXPERTS].set(b_router[l]),
    }
    return inw, lw, ow


def kernel(x_prompt, x_sample, cache_fox_k, cache_fox_v, cache_fox_logf, state_conv, state_lru, state_gla, page_table, c_prompt, c_sample, w_ada, b_ada, g_norm1, g_norm2, w_in, b_forget, conv_w, conv_b, lru_lambda, lru_wa, lru_ba, lru_wx, lru_bx, gla_w2, gla_b2, gla_gnorm, w_out, w_router, b_router, w_gu, b_gu, w_down, b_down, g_final):
    n_layers = w_ada.shape[0]
    bp, seq, d = x_prompt.shape
    bs = x_sample.shape[0]
    n_p = bp * seq
    n_tot = n_p + bs
    n_pool = cache_fox_k.shape[1]
    bm = MOE_BLOCK

    mod = _ada_call(jnp.concatenate([c_prompt, c_sample], axis=0), w_ada, b_ada)
    mod_p = mod[:, :bp].reshape(n_layers, bp, 6, 1, d).transpose(0, 2, 1, 3, 4)
    mod_s = mod[:, bp:].reshape(n_layers, 1, bs, 6, d).transpose(0, 3, 1, 2, 4)

    ckt = cache_fox_k.transpose(0, 1, 3, 4, 2).reshape(n_layers, n_pool, FOX_W, PAGE_SIZE)
    cvt = cache_fox_v.transpose(0, 1, 3, 4, 2).reshape(n_layers, n_pool, FOX_W, PAGE_SIZE)
    cft = cache_fox_logf.transpose(0, 1, 3, 2)
    conv_t = state_conv.transpose(0, 2, 1, 3)
    s_view = state_gla.transpose(0, 2, 3, 4, 1)
    b_gu4 = b_gu.reshape(n_layers, N_EXPERTS, 1, 2 * D_FF)
    b_dn4 = b_down.reshape(n_layers, N_EXPERTS, 1, d)
    gf = g_final.reshape(1, d)

    row_quant = SC_WINDOW * SC_WORKERS // TOP_K
    n_buf = -(-n_tot // row_quant) * row_quant
    n_a = (n_p // 2) // row_quant * row_quant
    parts = ((0, n_a, n_a), (n_a, n_buf - n_a, n_tot - n_a))

    def moe_scatter(l, h2, ti, part):
        row0, n_part, n_valid = part
        n_blocks = -(-(n_valid * TOP_K + N_EXPERTS * (bm - 1)) // bm)
        trash = n_blocks * bm
        dest, cnt = _rank_call(ti, row0, n_part, n_valid, trash, 512)
        counts = cnt[:, 0].astype(I32)
        pad_end = jnp.cumsum((counts + bm - 1) // bm * bm)
        blk_e = jnp.minimum(jnp.sum((jnp.arange(n_blocks, dtype=I32)[:, None] * bm >= pad_end[None, :]).astype(I32),
                                    axis=1), N_EXPERTS - 1).astype(I32)
        n_used = (pad_end[-1] // bm).astype(I32).reshape(1)
        ids = jnp.arange(N_EXPERTS, dtype=I32)
        present = counts > 0
        e_slot = (jnp.cumsum(present.astype(I32)) - 1) % 2
        later = jnp.where(present[None, :] & (ids[None, :] > ids[:, None]), ids[None, :], N_EXPERTS)
        e_next = jnp.min(later, axis=1)
        e_next = jnp.where(e_next < N_EXPERTS, e_next, -1)
        onehot = (blk_e[:, None] == ids[None, :]).astype(I32)
        blk_slot = jnp.sum(onehot * e_slot[None, :], axis=1).astype(I32)
        blk_next = jnp.sum(onehot * e_next[None, :], axis=1).astype(I32)
        e_first = (pad_end - (counts + bm - 1) // bm * bm) // bm
        blk_idx = jnp.arange(n_blocks, dtype=I32)
        blk_valid = jnp.clip(jnp.sum(onehot * counts[None, :], axis=1)
                             - (blk_idx - jnp.sum(onehot * e_first[None, :], axis=1)) * bm, 0, bm).astype(I32)
        dest_km = dest[0:TOP_K]
        xs = _sc_scatter_rows(h2, dest_km, (n_blocks + 1) * bm, row0)
        return xs, (blk_e, blk_slot, blk_next, blk_valid, n_used), jnp.minimum(dest_km, trash - 1)

    def moe_experts(l, sc):
        xs, tables, src = sc
        ye = _ffn_call(l, *tables, xs, w_gu, b_gu4, w_down, b_dn4)
        return _sc_gather_rows(ye, src.reshape(-1)).reshape(TOP_K, src.shape[1], d // 2)

    yp = x_prompt.reshape(n_p, d)
    ys = x_sample.reshape(bs, d)
    kv_p = (jnp.zeros((n_layers, bp, FOX_W, seq), F32), jnp.zeros((n_layers, bp, FOX_W, seq), F32),
            jnp.zeros((n_layers, bp, FOX_HEADS, seq), F32))
    outs_p = [[] for _ in range(3)]
    outs_s = [[] for _ in range(6)]
    for l in range(n_layers):
        inw, lw, ow = _layer_weights(l, w_in, b_forget, conv_w, conv_b, lru_lambda, lru_wa, lru_ba, lru_wx,
                                     lru_bx, gla_w2, gla_b2, gla_gnorm, w_out, w_router, b_router)
        g1 = g_norm1[l].reshape(1, d)
        g2 = g_norm2[l].reshape(1, d)

        kt_p, vt_p, lft_p, rz, qx, kx, vx = _inproj_call(yp, mod_p[l], g1, inw, seq, 256, l, n_layers, kv_bufs=kv_p)
        kv_p = (kt_p, vt_p, lft_p)
        fo = _fox_call(qx, kx, vx, bp, seq, 512)
        lo, conv_p, hlast_p = _lru_call(rz, lw, bp, seq, 256)
        go, st_p = _gla_call(rz, ow["gg"], bp, seq, 256)
        y1p, h2, ti, tw = _outproj_call(yp, fo, lo, go, mod_p[l], g2, ow["w_out"], ow["w_r"], ow["b_r"],
                                        256, seq, n_buf, 0)
        sc_a = moe_scatter(l, h2, ti, parts[0])
        st_p = st_p.reshape(bp, GLA_HEADS, HEAD_DIM, GLA_HEADS, HEAD_DIM)
        st_p = jnp.stack([st_p[:, hh, :, hh, :] for hh in range(GLA_HEADS)], axis=1).transpose(0, 1, 3, 2)
        outs_p[0].append(conv_p)
        outs_p[1].append(hlast_p.reshape(bp, LRU_W))
        outs_p[2].append(st_p)

        kts, vts, lfts, rzs, qs, ks, vs, logfs, gts = _inproj_call(
            ys, mod_s[l], g1, inw, bs, bs, 0, 1, sample=True)
        fos = _fox_dec_call(l, page_table, qs, ks, vs, logfs, ckt, cvt, cft)
        los, conv_s, h_s = _lru_step_call(l, rzs, conv_t, state_lru, lw)
        gost, s_s = _gla_step_call(l, gts, s_view, ow["ggc"])
        y1s, h2, ti, tw = _outproj_call(ys, fos, los, gost.T, mod_s[l], g2, ow["w_out"], ow["w_r"], ow["b_r"],
                                        bs, bs, n_buf, n_p, prev=(h2, ti, tw))
        outs_s[0].append(kts[0, 0])
        outs_s[1].append(vts[0, 0])
        outs_s[2].append(lfts[0, 0])
        outs_s[3].append(conv_s)
        outs_s[4].append(h_s)
        outs_s[5].append(s_s)

        sc_b = moe_scatter(l, h2, ti, parts[1])
        yk_a = moe_experts(l, sc_a)
        yk_b = moe_experts(l, sc_b)
        final = l == n_layers - 1
        yp = _combine_call(y1p, yk_a, tw, mod_p[l], gf, 256, seq, 0, n_a, 0, 0, final)
        yp = _combine_call(yp, yk_b, tw, mod_p[l], gf, 256, seq, n_a, n_p - n_a, 0, n_a, final)
        ys = _combine_call(y1s, yk_b, tw, mod_s[l], gf, bs, bs, 0, bs, n_p - n_a, n_p, final)

    kt_p, vt_p, lft_p = kv_p
    fox_k_p = kt_p.reshape(n_layers, bp, FOX_HEADS, HEAD_DIM, seq).transpose(0, 1, 4, 2, 3)
    fox_v_p = vt_p.reshape(n_layers, bp, FOX_HEADS, HEAD_DIM, seq).transpose(0, 1, 4, 2, 3)
    fox_f_p = lft_p.transpose(0, 1, 3, 2)
    fox_k_s = jnp.stack(outs_s[0]).reshape(n_layers, FOX_HEADS, HEAD_DIM, bs).transpose(0, 3, 1, 2)[:, :, None]
    fox_v_s = jnp.stack(outs_s[1]).reshape(n_layers, FOX_HEADS, HEAD_DIM, bs).transpose(0, 3, 1, 2)[:, :, None]
    fox_f_s = jnp.stack(outs_s[2]).transpose(0, 2, 1)[:, :, None]
    return (yp.reshape(bp, seq, d), ys.reshape(bs, 1, d),
            fox_k_p, fox_v_p, fox_f_p,
            jnp.stack(outs_p[0]), jnp.stack(outs_p[1]), jnp.stack(outs_p[2]),
            fox_k_s, fox_v_s, fox_f_s,
            jnp.stack(outs_s[3]).transpose(0, 2, 1, 3), jnp.stack(outs_s[4]),
            jnp.stack(outs_s[5]).transpose(0, 4, 1, 2, 3))
```

```python
import functools

import jax
import jax.numpy as jnp
from jax import lax
from jax.experimental import pallas as pl
from jax.experimental.pallas import tpu as pltpu
from jax.experimental.pallas import tpu_sc as plsc

F32 = jnp.float32
BF16 = jnp.bfloat16
I32 = jnp.int32

D_MODEL = 1024
HEAD_DIM = 64
FOX_W = 512
FOX_HEADS = 8
LRU_W = 256
GLA_W = 256
GLA_HEADS = 4
GLA_RANK = 16
GLA_CHUNK = 64
GLA_TAU = 16.0
LRU_C = 8.0
CONV_W = 4
N_EXPERTS = 32
TOP_K = 4
D_FF = 1024
SWIGLU_LIMIT = 7.0
SWIGLU_ALPHA = 1.702
NORM_EPS = 1e-6
PAGE_SIZE = 128
QK_SCALE = HEAD_DIM ** -0.5

_O_FQ, _O_FK, _O_FV, _O_FF = 0, 512, 1024, 1536
_O_LX, _O_LG, _O_GQ, _O_GK, _O_GV, _O_GA, _O_GOG = 1544, 1800, 2056, 2312, 2568, 2824, 2840
REC_W = 1792
GLA_T_ROWS = 5 * GLA_W
SMALL_W = 128
SMALL_T = 32
ROUTE_ROWS = 8
FOX_BLK = 2 * HEAD_DIM
FOX_QX = FOX_HEADS * FOX_BLK

VMEM_LIMIT = 56 * 1024 * 1024
MOE_BLOCK = 256
SC_WINDOW = 128
SC_COLS = 256
SC_WORKERS = 32

_NT = (((1,), (1,)), ((), ()))
_TN = (((0,), (0,)), ((), ()))


def _cparams(sem, vmem=VMEM_LIMIT):
    return pltpu.CompilerParams(dimension_semantics=sem, vmem_limit_bytes=vmem)


def _log_sigmoid(x):
    return jnp.minimum(x, 0.0) - jnp.log1p(jnp.exp(-jnp.abs(x)))


def _softplus(x):
    return jnp.maximum(x, 0.0) + jnp.log1p(jnp.exp(-jnp.abs(x)))


def _cumsum(x, axis):
    n = x.shape[axis]
    idx = lax.broadcasted_iota(I32, x.shape, axis)
    s = 1
    while s < n:
        x = x + jnp.where(idx >= s, pltpu.roll(x, s, axis), 0.0)
        s *= 2
    return x


_HI16 = -65536


def _pack_bf16_pairs(lo, hi):
    lo_bits = lax.bitcast_convert_type(lo.astype(BF16).astype(F32), I32)
    hi_bits = lax.bitcast_convert_type(hi.astype(BF16).astype(F32), I32)
    return lax.shift_right_logical(lo_bits, jnp.full(lo_bits.shape, 16, I32)) | (hi_bits & _HI16)


def _unpack_bf16_pairs(packed, dtype=BF16):
    lo = lax.bitcast_convert_type(lax.shift_left(packed, jnp.full(packed.shape, 16, I32)), F32)
    hi = lax.bitcast_convert_type(packed & _HI16, F32)
    return lo.astype(dtype), hi.astype(dtype)


def _mod_spec(mod, tm, tiles_per_seq, last_tile=None, tile_off=0):
    d = mod.shape[-1]
    clamp = (lambda i: i + tile_off) if last_tile is None else (lambda i: jnp.minimum(i, last_tile))
    if mod.shape[2] != 1:
        return pl.BlockSpec((6, None, tm, d), lambda i: (0, 0, clamp(i), 0))
    return pl.BlockSpec((6, None, 1, d), lambda i: (0, clamp(i) // tiles_per_seq, 0, 0))


def _ada_kernel(c_ref, w_ref, b_ref, o_ref):
    c = c_ref[...]
    a = (c * jax.nn.sigmoid(c)).astype(BF16)
    o_ref[...] = jnp.dot(a, w_ref[...].astype(BF16), preferred_element_type=F32) + b_ref[...]


def _ada_call(c_all, w_ada, b_ada):
    n_layers, d, w = w_ada.shape
    r = c_all.shape[0]
    tn = 1536
    return pl.pallas_call(
        _ada_kernel,
        grid=(n_layers, w // tn),
        in_specs=[
            pl.BlockSpec((r, d), lambda l, j: (0, 0)),
            pl.BlockSpec((None, d, tn), lambda l, j: (l, 0, j)),
            pl.BlockSpec((None, 1, tn), lambda l, j: (l, 0, j)),
        ],
        out_specs=pl.BlockSpec((None, r, tn), lambda l, j: (l, 0, j)),
        out_shape=jax.ShapeDtypeStruct((n_layers, r, w), F32),
        compiler_params=_cparams(("arbitrary", "arbitrary")),
        name="ada_mod",
    )(c_all, w_ada, b_ada.reshape(n_layers, 1, w))


def _inproj_kernel(*refs, tiles_per_seq, sample, n_alias):
    (y_ref, mod_ref, g_ref, wq_ref, wkvt_ref, wrec_ref, wsm_ref, wsmt_ref,
     bsm_ref, bfc_ref, w2_ref, b2_ref) = refs[:12]
    if sample:
        wgt_ref, w2t_ref, b2c_ref = refs[12:15]
        pos = 15 + n_alias
    else:
        eq_ref, oneq_ref, ek_ref, onek_ref = refs[12:16]
        pos = 16 + n_alias
    kt_ref, vt_ref, lft_ref, rz_ref = refs[pos:pos + 4]
    pos += 4
    if sample:
        q_ref, k_ref, v_ref, logf_ref, gt_ref = refs[pos:pos + 5]
        pos += 5
    else:
        qx_ref, kx_ref, vx_ref = refs[pos:pos + 3]
        pos += 3
    carry_c, carry_r = refs[pos:pos + 2]
    i = pl.program_id(0)

    @pl.when(i % tiles_per_seq == 0)
    def _():
        carry_c[...] = jnp.zeros_like(carry_c)
        carry_r[...] = jnp.zeros_like(carry_r)

    x = y_ref[...]
    xn = x * lax.rsqrt(jnp.mean(x * x, axis=-1, keepdims=True) + NORM_EPS) * g_ref[...]
    h = (xn * (1.0 + mod_ref[1]) + mod_ref[0]).astype(BF16)

    kt = lax.dot_general(wkvt_ref[0:FOX_W, :], h, _NT, preferred_element_type=F32)
    vt = lax.dot_general(wkvt_ref[FOX_W:2 * FOX_W, :], h, _NT, preferred_element_type=F32)
    kt_ref[...] = kt
    vt_ref[...] = vt
    rz_ref[:, 0:REC_W - GLA_W] = jnp.dot(h, wrec_ref[...], preferred_element_type=F32)

    sm = jnp.dot(h, wsm_ref[...], preferred_element_type=F32)
    lane = lax.broadcasted_iota(I32, sm.shape, 1)
    logf = jnp.where(lane < FOX_HEADS, _log_sigmoid(sm + bsm_ref[...]), 0.0)
    cum = _cumsum(logf, 0) + carry_c[...]
    carry_c[...] = cum[cum.shape[0] - 1:, :]
    glin =jnp.dot(sm.astype(BF16), w2_ref[...], preferred_element_type=F32) + b2_ref[...]
    rz_ref[:, REC_W - GLA_W:REC_W] = _log_sigmoid(glin) * (1.0 / GLA_TAU)

    smt = lax.dot_general(wsmt_ref[...], h, _NT, preferred_element_type=F32)
    lft = _log_sigmoid(smt[0:FOX_HEADS, :] + bfc_ref[...])
    lft_ref[...] = lft
    cumt = _cumsum(lft, 1) + carry_r[...]
    carry_r[...] = cumt[:, cumt.shape[1] - 1:]

    q = jnp.dot(h, wq_ref[...], preferred_element_type=F32) * QK_SCALE
    if not sample:
        def split3(c):
            hi = c.astype(BF16)
            r1 = c - hi.astype(F32)
            mid = r1.astype(BF16)
            lo = (r1 - mid.astype(F32)).astype(BF16)
            return [hi, mid, lo]

        cq = jnp.concatenate(split3(cum), axis=1)
        qx_ref[...] = (q + jnp.dot(cq, eq_ref[...], preferred_element_type=F32) + oneq_ref[...]).astype(BF16)
        ck = jnp.concatenate(split3(cumt) + [jnp.zeros(cumt.shape, BF16)], axis=0)
        kb = jnp.dot(ek_ref[...], ck, preferred_element_type=F32) + onek_ref[...]
        parts = []
        for hh in range(FOX_HEADS):
            parts += [kt[hh * HEAD_DIM:(hh + 1) * HEAD_DIM, :], kb[hh * HEAD_DIM:(hh + 1) * HEAD_DIM, :]]
        kx_ref[...] = jnp.concatenate(parts, axis=0).astype(BF16)
        vx_ref[...] = vt.astype(BF16)

    if sample:
        q_ref[...] = q.astype(BF16)
        k_ref[...] = kt.T
        v_ref[...] = vt.T
        logf_ref[...] = logf[:, 0:FOX_HEADS]
        gt_ref[0:4 * GLA_W, :] = lax.dot_general(wgt_ref[...], h, _NT, preferred_element_type=F32)
        glt = jnp.dot(w2t_ref[...], smt.astype(BF16), preferred_element_type=F32) + b2c_ref[...]
        gt_ref[4 * GLA_W:GLA_T_ROWS, :] = _log_sigmoid(glt) * (1.0 / GLA_TAU)


def _inproj_call(y, mod, g, wts, seq_len, tm, layer, n_layers, kv_bufs=None, sample=False):
    n, d = y.shape
    tiles_per_seq = seq_len // tm
    n_seq = n // seq_len
    qw = FOX_W if sample else FOX_QX
    const = lambda i: (0, 0)
    row = lambda i: (i, 0)
    seq_t = lambda i: (layer, i // tiles_per_seq, 0, i % tiles_per_seq)
    seq_t3 = lambda i: (i // tiles_per_seq, 0, i % tiles_per_seq)
    in_specs = [
        pl.BlockSpec((tm, d), row),
        _mod_spec(mod, tm, tiles_per_seq),
        pl.BlockSpec((1, d), const),
        pl.BlockSpec((d, qw), const),
        pl.BlockSpec((2 * FOX_W, d), const),
        pl.BlockSpec((d, REC_W - GLA_W), const),
        pl.BlockSpec((d, SMALL_W), const),
        pl.BlockSpec((SMALL_T, d), const),
        pl.BlockSpec((1, SMALL_W), const),
        pl.BlockSpec((FOX_HEADS, 1), const),
        pl.BlockSpec((SMALL_W, GLA_W), const),
        pl.BlockSpec((1, GLA_W), const),
    ]
    args = [y, mod, g, wts["w_q" if sample else "w_qx"], wts["w_kvt"], wts["w_rec"], wts["w_sm"], wts["w_smt"],
            wts["b_sm"], wts["b_fc"], wts["w2"], wts["b2"]]
    if sample:
        in_specs += [pl.BlockSpec((4 * GLA_W, d), const), pl.BlockSpec((GLA_W, SMALL_T), const),
                     pl.BlockSpec((GLA_W, 1), const)]
        args += [wts["w_gt"], wts["w2t"], wts["b2c"]]
    else:
        in_specs += [pl.BlockSpec((3 * SMALL_W, FOX_QX), const), pl.BlockSpec((1, FOX_QX), const),
                     pl.BlockSpec((FOX_W, SMALL_T), const), pl.BlockSpec((FOX_W, 1), const)]
        args += list(_bias_fold_consts())
    aliases = {}
    n_alias = 0
    if kv_bufs is not None:
        n_alias = 3
        first = len(args)
        in_specs += [pl.BlockSpec(memory_space=pl.ANY)] * 3
        args += list(kv_bufs)
        aliases = {first: 0, first + 1: 1, first + 2: 2}
    out_specs = [
        pl.BlockSpec((None, None, FOX_W, tm), seq_t),
        pl.BlockSpec((None, None, FOX_W, tm), seq_t),
        pl.BlockSpec((None, None, FOX_HEADS, tm), seq_t),
        pl.BlockSpec((tm, REC_W), row),
    ]
    out_shape = [
        jax.ShapeDtypeStruct((n_layers, n_seq, FOX_W, seq_len), F32),
        jax.ShapeDtypeStruct((n_layers, n_seq, FOX_W, seq_len), F32),
        jax.ShapeDtypeStruct((n_layers, n_seq, FOX_HEADS, seq_len), F32),
        jax.ShapeDtypeStruct((n, REC_W), F32),
    ]
    if sample:
        out_specs += [pl.BlockSpec((tm, FOX_W), row), pl.BlockSpec((tm, FOX_W), row), pl.BlockSpec((tm, FOX_W), row),
                      pl.BlockSpec((tm, FOX_HEADS), row), pl.BlockSpec((GLA_T_ROWS, tm), lambda i: (0, i))]
        out_shape += [jax.ShapeDtypeStruct((n, FOX_W), BF16), jax.ShapeDtypeStruct((n, FOX_W), F32),
                      jax.ShapeDtypeStruct((n, FOX_W), F32), jax.ShapeDtypeStruct((n, FOX_HEADS), F32),
                      jax.ShapeDtypeStruct((GLA_T_ROWS, n), F32)]
    else:
        out_specs += [pl.BlockSpec((tm, FOX_QX), row), pl.BlockSpec((None, FOX_QX, tm), seq_t3),
                      pl.BlockSpec((None, FOX_W, tm), seq_t3)]
        out_shape += [jax.ShapeDtypeStruct((n, FOX_QX), BF16), jax.ShapeDtypeStruct((n_seq, FOX_QX, seq_len), BF16),
                      jax.ShapeDtypeStruct((n_seq, FOX_W, seq_len), BF16)]
    return pl.pallas_call(
        functools.partial(_inproj_kernel, tiles_per_seq=tiles_per_seq, sample=sample, n_alias=n_alias),
        grid=(n // tm,),
        in_specs=in_specs,
        out_specs=tuple(out_specs),
        out_shape=tuple(out_shape),
        input_output_aliases=aliases,
        scratch_shapes=[pltpu.VMEM((1, SMALL_W), F32), pltpu.VMEM((FOX_HEADS, 1), F32)],
        compiler_params=_cparams(("arbitrary",)),
        name="in_proj",
    )(*args)


def _bias_fold_consts():
    h = jnp.arange(FOX_HEADS)
    eq = jnp.zeros((3 * SMALL_W, FOX_QX), F32)
    ek = jnp.zeros((FOX_W, SMALL_T), F32)
    oneq = jnp.zeros((1, FOX_QX), F32)
    onek = jnp.zeros((FOX_W, 1), F32)
    for piece in range(3):
        eq = eq.at[piece * SMALL_W + h, h * FOX_BLK + HEAD_DIM + piece].set(1.0)
        oneq = oneq.at[0, h * FOX_BLK + HEAD_DIM + 3 + piece].set(1.0)
        onek = onek.at[h * HEAD_DIM + piece, 0].set(1.0)
        ek = ek.at[h * HEAD_DIM + 3 + piece, piece * FOX_HEADS + h].set(-1.0)
    return eq.astype(BF16), oneq, ek.astype(BF16), onek


def _fox_kernel(qi_ref, ki_ref, qx_ref, kx_ref, vx_ref, o_ref, m_sc, l_sc, acc_sc, *, tq, tk):
    qi = qi_ref[pl.program_id(2)]
    ki = ki_ref[pl.program_id(2)]
    w = 2 * HEAD_DIM
    nc = tk // w

    @pl.when(ki == 0)
    def _():
        m_sc[...] = jnp.full_like(m_sc, -jnp.inf)
        l_sc[...] = jnp.zeros_like(l_sc)
        acc_sc[...] = jnp.zeros_like(acc_sc)

    def step(masked):
        vt = vx_ref[...]
        if masked:
            causal = lax.broadcasted_iota(I32, (tq, tk), 1) <= lax.broadcasted_iota(I32, (tq, tk), 0)
        for j in range(2):
            s = jnp.dot(qx_ref[:, j * FOX_BLK:(j + 1) * FOX_BLK], kx_ref[j * FOX_BLK:(j + 1) * FOX_BLK, :],
                        preferred_element_type=F32)
            if masked:
                s = jnp.where(causal, s, -jnp.inf)
            sc = [s[:, c * w:(c + 1) * w] for c in range(nc)]
            mb = sc[0]
            for c in range(1, nc):
                mb = jnp.maximum(mb, sc[c])
            m_prev = m_sc[j]
            m_new = jnp.maximum(m_prev, jnp.broadcast_to(jnp.max(mb, axis=1, keepdims=True), (tq, w)))
            ps = [jnp.exp(sc[c] - m_new) for c in range(nc)]
            lsum = ps[0]
            for c in range(1, nc):
                lsum = lsum + ps[c]
            alpha = jnp.exp(m_prev - m_new)
            l_sc[j] = alpha * l_sc[j] + jnp.broadcast_to(jnp.sum(lsum, axis=1, keepdims=True), (tq, w))
            p = jnp.concatenate([pc.astype(BF16) for pc in ps], axis=1)
            acc_sc[j] = alpha * acc_sc[j] + lax.dot_general(p, vt, _NT, preferred_element_type=F32)
            m_sc[j] = m_new

    @pl.when(ki < qi)
    def _():
        step(False)

    @pl.when(ki == qi)
    def _():
        step(True)
        lane = lax.broadcasted_iota(I32, (tq, w), 1)
        o0 = acc_sc[0] / l_sc[0]
        o1 = acc_sc[1] / l_sc[1]
        o_ref[...] = jnp.where(lane < HEAD_DIM, o0, o1).astype(BF16)


def _fox_call(qx, kx, vx, n_seq, seq_len, tq):
    n = qx.shape[0]
    nt = seq_len // tq
    tk = tq
    hp = FOX_HEADS // 2
    w = 2 * HEAD_DIM
    pairs = [(i, j) for i in range(nt) for j in range(i + 1)]
    qi_tab = jnp.asarray([p[0] for p in pairs], I32)
    ki_tab = jnp.asarray([p[1] for p in pairs], I32)
    past = lambda b, h, s, qt, kt: (b, h, kt[s])
    grid_spec = pltpu.PrefetchScalarGridSpec(
        num_scalar_prefetch=2,
        grid=(n_seq, hp, len(pairs)),
        in_specs=[
            pl.BlockSpec((tq, 2 * FOX_BLK), lambda b, h, s, qt, kt: (b * nt + qt[s], h)),
            pl.BlockSpec((None, 2 * FOX_BLK, tk), past),
            pl.BlockSpec((None, w, tk), past),
        ],
        out_specs=pl.BlockSpec((tq, w), lambda b, h, s, qt, kt: (b * nt + qt[s], h)),
        scratch_shapes=[pltpu.VMEM((2, tq, w), F32), pltpu.VMEM((2, tq, w), F32), pltpu.VMEM((2, tq, w), F32)],
    )
    return pl.pallas_call(
        functools.partial(_fox_kernel, tq=tq, tk=tk),
        grid_spec=grid_spec,
        out_shape=jax.ShapeDtypeStruct((n, FOX_W), BF16),
        compiler_params=_cparams(("arbitrary", "arbitrary", "arbitrary")),
        name="fox_prompt",
    )(qi_tab, ki_tab, qx, kx, vx)


def _lru_gates(xc, wa_ref, ba_ref, wx_ref, bx_ref, lam_ref):
    xb = xc.astype(BF16)
    r = jax.nn.sigmoid(jnp.dot(xb, wa_ref[...], preferred_element_type=F32) + ba_ref[...])
    gi = jax.nn.sigmoid(jnp.dot(xb, wx_ref[...], preferred_element_type=F32) + bx_ref[...])
    log_a = -LRU_C * r * _softplus(-lam_ref[...])
    a = jnp.exp(log_a)
    mult = jnp.sqrt(-jnp.tanh(log_a) * (a * a + 1.0))
    return a, mult, gi


def _lru_kernel(lx_ref, lg_ref, cw_ref, cb_ref, wa_ref, ba_ref, wx_ref, bx_ref, lam_ref,
                lo_ref, conv_ref, hlast_ref, xbuf, hcar, *, tt):
    ti = pl.program_id(1)
    nt = pl.num_programs(1)

    @pl.when(ti == 0)
    def _():
        xbuf[0:8, :] = jnp.zeros((8, LRU_W), F32)
        hcar[...] = jnp.zeros_like(hcar)

    x = lx_ref[...]
    xbuf[8:8 + tt, :] = x
    xc = cb_ref[...] + cw_ref[3:4, :] * x
    for j in range(CONV_W - 1):
        xc = xc + cw_ref[j:j + 1, :] * xbuf[5 + j:5 + j + tt, :]
    xbuf[0:8, :] = x[tt - 8:tt, :]

    a, mult, gi = _lru_gates(xc, wa_ref, ba_ref, wx_ref, bx_ref, lam_ref)
    row = lax.broadcasted_iota(I32, (tt, LRU_W), 0)
    mult = jnp.where((row == 0) & (ti == 0), 1.0, mult)
    b = mult * gi * xc
    s = 1
    while s < tt:
        keep = row >= s
        a_sh = jnp.where(keep, pltpu.roll(a, s, 0), 1.0)
        b_sh = jnp.where(keep, pltpu.roll(b, s, 0), 0.0)
        b = a * b_sh + b
        a = a * a_sh
        s *= 2
    h = a * hcar[...] + b
    hcar[...] = h[tt - 1:tt, :]
    lo_ref[...] = (h * jax.nn.gelu(lg_ref[...])).astype(BF16)

    @pl.when(ti == nt - 1)
    def _():
        conv_ref[...] = x[tt - (CONV_W - 1):tt, :]
        hlast_ref[...] = h[tt - 1:tt, :]


def _lru_weight_specs(const):
    return [
        pl.BlockSpec((CONV_W, LRU_W), const), pl.BlockSpec((1, LRU_W), const),
        pl.BlockSpec((LRU_W, LRU_W), const), pl.BlockSpec((1, LRU_W), const),
        pl.BlockSpec((LRU_W, LRU_W), const), pl.BlockSpec((1, LRU_W), const),
        pl.BlockSpec((1, LRU_W), const),
    ]


def _lru_weight_args(lw):
    return [lw["conv_w"], lw["conv_b"], lw["wa"], lw["ba"], lw["wx"], lw["bx"], lw["lam"]]


def _lru_call(rz, lw, n_seq, seq_len, tt):
    n = rz.shape[0]
    nt = seq_len // tt
    return pl.pallas_call(
        functools.partial(_lru_kernel, tt=tt),
        grid=(n_seq, nt),
        in_specs=[
            pl.BlockSpec((tt, LRU_W), lambda b, t: (b * nt + t, 0)),
            pl.BlockSpec((tt, LRU_W), lambda b, t: (b * nt + t, 1)),
        ] + _lru_weight_specs(lambda b, t: (0, 0)),
        out_specs=(
            pl.BlockSpec((tt, LRU_W), lambda b, t: (b * nt + t, 0)),
            pl.BlockSpec((None, CONV_W - 1, LRU_W), lambda b, t: (b, 0, 0)),
            pl.BlockSpec((None, 1, LRU_W), lambda b, t: (b, 0, 0)),
        ),
        out_shape=(
            jax.ShapeDtypeStruct((n, LRU_W), BF16),
            jax.ShapeDtypeStruct((n_seq, CONV_W - 1, LRU_W), F32),
            jax.ShapeDtypeStruct((n_seq, 1, LRU_W), F32),
        ),
        scratch_shapes=[pltpu.VMEM((tt + 8, LRU_W), F32), pltpu.VMEM((1, LRU_W), F32)],
        compiler_params=_cparams(("arbitrary", "arbitrary")),
        name="lru_prompt",
    )(rz, rz, *_lru_weight_args(lw))


def _head_rms_gate(o, gg_ref, gog):
    lane = lax.broadcasted_iota(I32, o.shape, 1)
    o2 = o * o
    rs = jnp.zeros_like(o)
    for hh in range(GLA_HEADS):
        mh = lane // HEAD_DIM == hh
        ms = jnp.sum(jnp.where(mh, o2, 0.0), axis=1, keepdims=True) * (1.0 / HEAD_DIM)
        rs = jnp.where(mh, lax.rsqrt(ms + NORM_EPS), rs)
    return o * rs * gg_ref[...] * (gog * jax.nn.sigmoid(gog))


def _gla_kernel(gq_ref, gk_ref, gv_ref, gog_ref, gl_ref, gg_ref, go_ref, st_ref, s_sc, *, tt, group):
    ti = pl.program_id(1)
    nt = pl.num_programs(1)
    c = GLA_CHUNK

    @pl.when(ti == 0)
    def _():
        s_sc[...] = jnp.zeros_like(s_sc)

    lane = lax.broadcasted_iota(I32, (c, GLA_W), 1)
    r2 = lax.broadcasted_iota(I32, (GLA_W, GLA_W), 0)
    c2 = lax.broadcasted_iota(I32, (GLA_W, GLA_W), 1)
    same_head = (r2 // HEAD_DIM) == (c2 // HEAD_DIM)
    tril = lax.broadcasted_iota(I32, (c, c), 1) <= lax.broadcasted_iota(I32, (c, c), 0)

    for ci in range(tt // c):
        sl = slice(ci * c, (ci + 1) * c)
        for gi in range(group):
            q = gq_ref[gi, sl, :] * QK_SCALE
            k = gk_ref[gi, sl, :]
            v = gv_ref[gi, sl, :].astype(BF16)
            bc = _cumsum(gl_ref[gi, sl, :], 0)
            b_last = bc[c - 1:c, :]
            qd = (q * jnp.exp(bc)).astype(BF16)
            kinv = (k * jnp.exp(-bc)).astype(BF16)
            kdec = (k * jnp.exp(b_last - bc)).astype(BF16)
            s_prev = s_sc[gi]
            o = lax.dot_general(qd, s_prev.astype(BF16), _NT, preferred_element_type=F32)
            for hh in range(GLA_HEADS):
                mh = lane // HEAD_DIM == hh
                att = lax.dot_general(jnp.where(mh, qd, jnp.zeros_like(qd)), kinv, _NT, preferred_element_type=F32)
                att = jnp.where(tril, att, 0.0).astype(BF16)
                o = o + jnp.dot(att, jnp.where(mh, v, jnp.zeros_like(v)), preferred_element_type=F32)
            ut = lax.dot_general(v, kdec, _TN, preferred_element_type=F32)
            s_sc[gi] = s_prev * jnp.exp(b_last) + jnp.where(same_head, ut, 0.0)
            go_ref[gi, sl, :] = _head_rms_gate(o, gg_ref, gog_ref[gi, sl, :]).astype(BF16)

    @pl.when(ti == nt - 1)
    def _():
        st_ref[...] = s_sc[...]


def _gla_call(rz, gg, n_seq, seq_len, tt, group=2):
    n, w = rz.shape
    nt = seq_len // tt
    rz4 = rz.reshape(n_seq // group, group, seq_len, w)

    def col(j):
        return pl.BlockSpec((None, group, tt, GLA_W), lambda b, t: (b, 0, t, j))

    go, st = pl.pallas_call(
        functools.partial(_gla_kernel, tt=tt, group=group),
        grid=(n_seq // group, nt),
        in_specs=[col(2), col(3), col(4), col(5), col(6), pl.BlockSpec((1, GLA_W), lambda b, t: (0, 0))],
        out_specs=(
            pl.BlockSpec((None, group, tt, GLA_W), lambda b, t: (b, 0, t, 0)),
            pl.BlockSpec((group, GLA_W, GLA_W), lambda b, t: (b, 0, 0)),
        ),
        out_shape=(
            jax.ShapeDtypeStruct((n_seq // group, group, seq_len, GLA_W), BF16),
            jax.ShapeDtypeStruct((n_seq, GLA_W, GLA_W), F32),
        ),
        scratch_shapes=[pltpu.VMEM((group, GLA_W, GLA_W), F32)],
        compiler_params=_cparams(("arbitrary", "arbitrary")),
        name="gla_prompt",
    )(rz4, rz4, rz4, rz4, rz4, gg)
    return go.reshape(n, GLA_W), st


def _fox_dec_kernel(pt_ref, q_ref, kn_ref, vn_ref, dn_ref, *refs, n_pages):
    del pt_ref
    k_refs = refs[0:n_pages]
    v_refs = refs[n_pages:2 * n_pages]
    f_refs = refs[2 * n_pages:3 * n_pages]
    o_ref = refs[3 * n_pages]
    w = FOX_W
    hrow = lax.broadcasted_iota(I32, (FOX_HEADS, w), 0)
    hlane = lax.broadcasted_iota(I32, (FOX_HEADS, w), 1) // HEAD_DIM
    diag = hrow == hlane
    q = q_ref[...].astype(F32)
    qbd = jnp.where(diag, jnp.broadcast_to(q, (FOX_HEADS, w)), 0.0).astype(BF16)
    s = jnp.concatenate(
        [jnp.dot(qbd, k_refs[p][...].astype(BF16), preferred_element_type=F32) for p in range(n_pages)], axis=1)
    lf = jnp.concatenate([f_refs[p][...] for p in range(n_pages)], axis=1)
    cs = _cumsum(lf, 1)
    suffix = cs[:, cs.shape[1] - 1:] - cs
    s = s + dn_ref[...] + suffix
    s_new = jnp.sum(qbd.astype(F32) * kn_ref[...], axis=1, keepdims=True)
    m = jnp.maximum(jnp.max(s, axis=1, keepdims=True), s_new)
    p_past = jnp.exp(s - m)
    p_new = jnp.exp(s_new - m)
    denom = jnp.sum(p_past, axis=1, keepdims=True) + p_new
    acc = p_new * vn_ref[...]
    pb = p_past.astype(BF16)
    for p in range(n_pages):
        acc = acc + lax.dot_general(pb[:, p * PAGE_SIZE:(p + 1) * PAGE_SIZE], v_refs[p][...].astype(BF16), _NT,
                                    preferred_element_type=F32)
    out = jnp.where(diag, acc / denom, 0.0)
    o_ref[...] = jnp.sum(out, axis=0, keepdims=True).astype(BF16)


def _fox_dec_call(layer, page_table, q, k_new, v_new, logf_new, cache_kt, cache_vt, cache_ft):
    bd, n_pages = page_table.shape
    w = FOX_W

    def page_spec(rows, j):
        return pl.BlockSpec((None, None, rows, PAGE_SIZE), lambda b, pt, j=j: (layer, pt[b, j], 0, 0))

    row = lambda b, pt: (b, 0, 0)
    in_specs = [
        pl.BlockSpec((None, 1, w), row),
        pl.BlockSpec((None, 1, w), row),
        pl.BlockSpec((None, 1, w), row),
        pl.BlockSpec((None, FOX_HEADS, 1), row),
    ]
    in_specs += [page_spec(w, j) for j in range(n_pages)]
    in_specs += [page_spec(w, j) for j in range(n_pages)]
    in_specs += [page_spec(FOX_HEADS, j) for j in range(n_pages)]
    grid_spec = pltpu.PrefetchScalarGridSpec(
        num_scalar_prefetch=1,
        grid=(bd,),
        in_specs=in_specs,
        out_specs=pl.BlockSpec((None, 1, w), row),
    )
    out = pl.pallas_call(
        functools.partial(_fox_dec_kernel, n_pages=n_pages),
        grid_spec=grid_spec,
        out_shape=jax.ShapeDtypeStruct((bd, 1, w), BF16),
        compiler_params=_cparams(("arbitrary",)),
        name="fox_sample",
    )(page_table, q.reshape(bd, 1, w), k_new.reshape(bd, 1, w), v_new.reshape(bd, 1, w),
      logf_new.reshape(bd, FOX_HEADS, 1),
      *([cache_kt] * n_pages), *([cache_vt] * n_pages), *([cache_ft] * n_pages))
    return out.reshape(bd, w)


def _lru_step_kernel(lx_ref, lg_ref, conv_ref, h0_ref, cw_ref, cb_ref, wa_ref, ba_ref, wx_ref, bx_ref, lam_ref,
                     lo_ref, convn_ref, hn_ref):
    x = lx_ref[...]
    xc = cb_ref[...] + cw_ref[3:4, :] * x
    for j in range(CONV_W - 1):
        xc = xc + cw_ref[j:j + 1, :] * conv_ref[j]
    convn_ref[0] = conv_ref[1]
    convn_ref[1] = conv_ref[2]
    convn_ref[2] = x
    a, mult, gi = _lru_gates(xc, wa_ref, ba_ref, wx_ref, bx_ref, lam_ref)
    h = a * h0_ref[...] + mult * gi * xc
    hn_ref[...] = h
    lo_ref[...] = (h * jax.nn.gelu(lg_ref[...])).astype(BF16)


def _lru_step_call(layer, rz, conv_t, h0, lw):
    bd = rz.shape[0]
    return pl.pallas_call(
        _lru_step_kernel,
        grid=(1,),
        in_specs=[
            pl.BlockSpec((bd, LRU_W), lambda i: (0, 0)),
            pl.BlockSpec((bd, LRU_W), lambda i: (0, 1)),
            pl.BlockSpec((None, CONV_W - 1, bd, LRU_W), lambda i: (layer, 0, 0, 0)),
            pl.BlockSpec((None, bd, LRU_W), lambda i: (layer, 0, 0)),
        ] + _lru_weight_specs(lambda i: (0, 0)),
        out_specs=(
            pl.BlockSpec((bd, LRU_W), lambda i: (0, 0)),
            pl.BlockSpec((CONV_W - 1, bd, LRU_W), lambda i: (0, 0, 0)),
            pl.BlockSpec((bd, LRU_W), lambda i: (0, 0)),
        ),
        out_shape=(
            jax.ShapeDtypeStruct((bd, LRU_W), BF16),
            jax.ShapeDtypeStruct((CONV_W - 1, bd, LRU_W), F32),
            jax.ShapeDtypeStruct((bd, LRU_W), F32),
        ),
        compiler_params=_cparams(("arbitrary",)),
        name="lru_step",
    )(rz, rz, conv_t, h0, *_lru_weight_args(lw))


def _gla_step_kernel(q_ref, k_ref, v_ref, gog_ref, gl_ref, gg_ref, s_ref, go_ref, sn_ref):
    eg = jnp.exp(gl_ref[...])
    kt = k_ref[...]
    qt = q_ref[...] * QK_SCALE
    vt = v_ref[...]
    o = jnp.zeros_like(vt)
    for kk in range(HEAD_DIM):
        s_new = eg[kk:kk + 1, :] * s_ref[kk] + kt[kk:kk + 1, :] * vt
        sn_ref[kk] = s_new
        o = o + qt[kk:kk + 1, :] * s_new
    ms = jnp.mean(o * o, axis=0, keepdims=True)
    gog = gog_ref[...]
    go_ref[...] = o * lax.rsqrt(ms + NORM_EPS) * gg_ref[...] * (gog * jax.nn.sigmoid(gog))


def _gla_step_call(layer, gt, s_view, ggc):
    bd = gt.shape[1]
    hd = HEAD_DIM

    def part(j):
        return pl.BlockSpec((hd, bd), lambda h, j=j: (j * GLA_HEADS + h, 0))

    return pl.pallas_call(
        _gla_step_kernel,
        grid=(GLA_HEADS,),
        in_specs=[part(0), part(1), part(2), part(3), part(4),
                  pl.BlockSpec((hd, 1), lambda h: (h, 0)),
                  pl.BlockSpec((None, None, hd, hd, bd), lambda h: (layer, h, 0, 0, 0))],
        out_specs=(
            pl.BlockSpec((hd, bd), lambda h: (h, 0)),
            pl.BlockSpec((None, hd, hd, bd), lambda h: (h, 0, 0, 0)),
        ),
        out_shape=(
            jax.ShapeDtypeStruct((GLA_W, bd), F32),
            jax.ShapeDtypeStruct((GLA_HEADS, hd, hd, bd), F32),
        ),
        compiler_params=_cparams(("arbitrary",)),
        name="gla_step",
    )(gt, gt, gt, gt, gt, ggc, s_view)


def _outproj_kernel(*refs, n_alias, n_tiles):
    h2_ref, ti_ref, tw_ref = refs[10 + n_alias:13 + n_alias]
    i = pl.program_id(0)

    @pl.when(i < n_tiles)
    def _():
        _outproj_tile(*refs[0:9], *refs[9 + n_alias:13 + n_alias])

    @pl.when(i >= n_tiles)
    def _():
        h2_ref[...] = jnp.zeros_like(h2_ref)
        ti_ref[...] = jnp.zeros_like(ti_ref)
        tw_ref[...] = jnp.zeros_like(tw_ref)


def _outproj_tile(y_ref, fo_ref, lo_ref, go_ref, mod_ref, g2_ref, wo_ref, wr_ref, br_ref,
                  y1_ref, h2_ref, ti_ref, tw_ref):
    m = jnp.dot(fo_ref[...], wo_ref[0:FOX_W, :], preferred_element_type=F32)
    m = m + jnp.dot(lo_ref[...], wo_ref[FOX_W:FOX_W + LRU_W, :], preferred_element_type=F32)
    m = m + jnp.dot(go_ref[...].astype(BF16), wo_ref[FOX_W + LRU_W:, :], preferred_element_type=F32)
    y1 = y_ref[...] + mod_ref[2] * m
    y1_ref[...] = y1
    xn = y1 * lax.rsqrt(jnp.mean(y1 * y1, axis=-1, keepdims=True) + NORM_EPS) * g2_ref[...]
    h2 = xn * (1.0 + mod_ref[4]) + mod_ref[3]
    half = h2.shape[1] // 2
    h2_ref[...] = _pack_bf16_pairs(h2[:, 0:half], h2[:, half:])
    logits = jnp.dot(h2.astype(BF16), wr_ref[...], preferred_element_type=F32) + br_ref[...]
    tm = logits.shape[0]
    lt = logits.T[0:N_EXPERTS, :]
    eidx = lax.broadcasted_iota(I32, lt.shape, 0)
    sels, vals = [], []
    for kk in range(TOP_K):
        mx = jnp.max(lt, axis=0, keepdims=True)
        sel = jnp.min(jnp.where(lt == mx, eidx, N_EXPERTS), axis=0, keepdims=True)
        sels.append(sel)
        vals.append(mx)
        lt = jnp.where(eidx == sel, -jnp.inf, lt)
    es = [jnp.exp(vv - vals[0]) for vv in vals]
    tot = es[0] + es[1] + es[2] + es[3]
    row = lax.broadcasted_iota(I32, (ROUTE_ROWS, tm), 0)
    idx_out = jnp.zeros((ROUTE_ROWS, tm), I32)
    val_out = jnp.zeros((ROUTE_ROWS, tm), F32)
    for kk in range(TOP_K):
        idx_out = jnp.where(row == kk, sels[kk], idx_out)
        val_out = jnp.where(row == kk, es[kk] / tot, val_out)
    ti_ref[...] = idx_out
    tw_ref[...] = val_out


def _outproj_call(y, fo, lo, go, mod, g2, w_out, w_r, b_r, tm, seq_len, n_buf, row_off, prev=None):
    n, d = y.shape
    tiles_per_seq = max(seq_len // tm, 1)
    n_tiles = n // tm
    n_steps = n_tiles if prev is not None else n_buf // tm
    const = lambda i: (0, 0)
    row = lambda i: (jnp.minimum(i, n_tiles - 1), 0)
    off = row_off // tm
    orow = lambda i: (i + off, 0)
    in_specs = [
        pl.BlockSpec((tm, d), row),
        pl.BlockSpec((tm, FOX_W), row),
        pl.BlockSpec((tm, LRU_W), row),
        pl.BlockSpec((tm, GLA_W), row),
        _mod_spec(mod, tm, tiles_per_seq, n_tiles - 1),
        pl.BlockSpec((1, d), const),
        pl.BlockSpec((d, d), const),
        pl.BlockSpec((d, SMALL_W), const),
        pl.BlockSpec((1, SMALL_W), const),
    ]
    args = [y, fo, lo, go, mod, g2, w_out, w_r, b_r]
    aliases = {}
    if prev is not None:
        in_specs += [pl.BlockSpec(memory_space=pl.ANY)] * 3
        args += list(prev)
        aliases = {9: 1, 10: 2, 11: 3}
    return pl.pallas_call(
        functools.partial(_outproj_kernel, n_alias=0 if prev is None else 3, n_tiles=n_tiles),
        grid=(n_steps,),
        in_specs=in_specs,
        out_specs=(
            pl.BlockSpec((tm, d), row),
            pl.BlockSpec((tm, d // 2), orow),
            pl.BlockSpec((ROUTE_ROWS, tm), lambda i: (0, i + off)),
            pl.BlockSpec((ROUTE_ROWS, tm), lambda i: (0, i + off)),
        ),
        out_shape=(
            jax.ShapeDtypeStruct((n, d), F32),
            jax.ShapeDtypeStruct((n_buf, d // 2), I32),
            jax.ShapeDtypeStruct((ROUTE_ROWS, n_buf), I32),
            jax.ShapeDtypeStruct((ROUTE_ROWS, n_buf), F32),
        ),
        input_output_aliases=aliases,
        compiler_params=_cparams(("arbitrary",)),
        name="out_proj_router",
    )(*args)


def _rank_kernel(ti_ref, dest_ref, cnt_ref, carry, *, tm, n_valid, trash):
    p = pl.program_id(0)
    i = pl.program_id(1)
    w = SMALL_W

    @pl.when((p == 0) & (i == 0))
    def _():
        carry[...] = jnp.zeros_like(carry)

    @pl.when((p == 1) & (i == 0))
    def _():
        cnt = carry[...]
        cnt_ref[...] = cnt
        padded = jnp.floor((cnt + (MOE_BLOCK - 1.0)) * (1.0 / MOE_BLOCK)) * MOE_BLOCK
        carry[...] = _cumsum(padded, 0) - padded

    t = ti_ref[...]
    eidx = lax.broadcasted_iota(I32, (N_EXPERTS, tm), 0)
    valid = (lax.broadcasted_iota(I32, (1, tm), 1) + i * tm) < n_valid
    ohs = [jnp.where(valid & (eidx == t[kk:kk + 1, :]), 1.0, 0.0) for kk in range(TOP_K)]

    @pl.when(p == 0)
    def _():
        tile_cnt = jnp.sum(sum(ohs[1:], ohs[0]), axis=1, keepdims=True)
        carry[...] = carry[...] + jnp.broadcast_to(tile_cnt, (N_EXPERTS, w))

    @pl.when(p == 1)
    def _():
        earlier = (lax.broadcasted_iota(I32, (tm, tm), 0) < lax.broadcasted_iota(I32, (tm, tm), 1)).astype(BF16)
        stack = jnp.concatenate(ohs, axis=0).astype(BF16)
        pre = jnp.dot(stack, earlier, preferred_element_type=F32)
        base = carry[:, 0:1]
        row = lax.broadcasted_iota(I32, (ROUTE_ROWS, tm), 0)
        out = jnp.zeros((ROUTE_ROWS, tm), F32)
        for kk in range(TOP_K):
            oh = ohs[kk]
            slot = jnp.sum(oh * (pre[kk * N_EXPERTS:(kk + 1) * N_EXPERTS, :] + base), axis=0, keepdims=True)
            out = jnp.where(row == kk, jnp.where(valid, slot, float(trash)), out)
            base = base + jnp.sum(oh, axis=1, keepdims=True)
        carry[...] = jnp.broadcast_to(base, (N_EXPERTS, w))
        dest_ref[...] = out.astype(I32)


def _rank_call(ti, col0, n_cols, n_valid, trash, tm):
    off = col0 // tm
    return pl.pallas_call(
        functools.partial(_rank_kernel, tm=tm, n_valid=n_valid, trash=trash),
        grid=(2, n_cols // tm),
        in_specs=[pl.BlockSpec((ROUTE_ROWS, tm), lambda p, i: (0, i + off))],
        out_specs=(
            pl.BlockSpec((ROUTE_ROWS, tm), lambda p, i: (0, i * p)),
            pl.BlockSpec((N_EXPERTS, SMALL_W), lambda p, i: (0, 0)),
        ),
        out_shape=(
            jax.ShapeDtypeStruct((ROUTE_ROWS, n_cols), I32),
            jax.ShapeDtypeStruct((N_EXPERTS, SMALL_W), F32),
        ),
        scratch_shapes=[pltpu.VMEM((N_EXPERTS, SMALL_W), F32)],
        compiler_params=_cparams(("arbitrary", "arbitrary")),
        name="moe_rank",
    )(ti)


def _sc_mesh():
    return plsc.VectorSubcoreMesh(core_axis_name="core", subcore_axis_name="subcore")


def _sc_scatter_rows(x, dest_km, n_out, row0=0):
    kk, n = dest_km.shape
    d = x.shape[1]
    nb = n // SC_WINDOW
    off = row0 // SC_WINDOW
    assert (kk * nb) % SC_WORKERS == 0 and d % SC_COLS == 0 and row0 % SC_WINDOW == 0

    @pl.kernel(out_type=jax.ShapeDtypeStruct((n_out, d), x.dtype), mesh=_sc_mesh())
    def scatter_kernel(x_hbm, i_hbm, o_hbm):
        def body(x_vmem, i_vmem):
            j = pl.program_id(1)
            pltpu.sync_copy(x_vmem, o_hbm.at[i_vmem.at[0], pl.ds(j * SC_COLS, SC_COLS)])

        pltpu.emit_pipeline(
            body,
            grid=(kk * nb, d // SC_COLS),
            in_specs=[pl.BlockSpec((SC_WINDOW, SC_COLS), lambda g, j: (g % nb + off, j)),
                      pl.BlockSpec((1, SC_WINDOW), lambda g, j: (g // nb, g % nb))],
            out_specs=[],
            core_axis_name=("core", "subcore"),
            dimension_semantics=(pltpu.PARALLEL, pltpu.ARBITRARY),
        )(x_hbm, i_hbm)

    return scatter_kernel(x, dest_km)


def _sc_gather_rows(x, idx):
    n = idx.shape[0]
    d = x.shape[1]
    assert (n // SC_WINDOW) % SC_WORKERS == 0 and d % SC_COLS == 0

    @pl.kernel(out_type=jax.ShapeDtypeStruct((n, d), x.dtype), mesh=_sc_mesh())
    def gather_kernel(x_hbm, i_hbm, o_hbm):
        def body(i_vmem, o_vmem):
            j = pl.program_id(1)
            pltpu.sync_copy(x_hbm.at[i_vmem.at[0], pl.ds(j * SC_COLS, SC_COLS)], o_vmem)

        pltpu.emit_pipeline(
            body,
            grid=(n // SC_WINDOW, d // SC_COLS),
            in_specs=[pl.BlockSpec((1, SC_WINDOW), lambda i, j: (0, i))],
            out_specs=[pl.BlockSpec((SC_WINDOW, SC_COLS), lambda i, j: (i, j))],
            core_axis_name=("core", "subcore"),
            dimension_semantics=(pltpu.PARALLEL, pltpu.ARBITRARY),
        )(i_hbm, o_hbm)

    return gather_kernel(x, idx.reshape(1, n))


def _ffn_kernel(be_ref, slot_ref, nxt_ref, valid_ref, nu_ref, x_ref, wgu_hbm, bgu_ref, wdn_hbm, bdn_ref, y_ref,
                wgu_f, wdn_f, wgu_b, wdn_b, sem, *, layer):
    j = pl.program_id(0)
    e = be_ref[j]
    slot = slot_ref[j]
    used = j < nu_ref[0]
    first = used & ((j == 0) | (e != be_ref[jnp.maximum(j - 1, 0)]))

    def weight_copies(expert, s):
        return (pltpu.make_async_copy(wgu_hbm.at[layer, expert], wgu_f.at[s], sem.at[0, s]),
                pltpu.make_async_copy(wdn_hbm.at[layer, expert], wdn_f.at[s], sem.at[1, s]))

    @pl.when(first)
    def _():
        @pl.when(j == 0)
        def _():
            for cp in weight_copies(e, slot):
                cp.start()

        for cp in weight_copies(e, slot):
            cp.wait()
        wgu_b[...] = wgu_f[slot].astype(BF16)
        wdn_b[...] = wdn_f[slot].astype(BF16)
        nxt = nxt_ref[j]

        @pl.when(nxt >= 0)
        def _():
            for cp in weight_copies(nxt, 1 - slot):
                cp.start()

    def expert_rows(xp):
        half = xp.shape[1]
        x_lo, x_hi = _unpack_bf16_pairs(xp)
        gu = (jnp.dot(x_lo, wgu_b[0:half, :], preferred_element_type=F32)
              + jnp.dot(x_hi, wgu_b[half:, :], preferred_element_type=F32) + bgu_ref[...])
        g = jnp.minimum(gu[:, 0:D_FF], SWIGLU_LIMIT)
        u = jnp.clip(gu[:, D_FF:], -SWIGLU_LIMIT, SWIGLU_LIMIT)
        act = g * jax.nn.sigmoid(SWIGLU_ALPHA * g)
        hmid = ((u + 1.0) * act).astype(BF16)
        y = jnp.dot(hmid, wdn_b[...], preferred_element_type=F32) + bdn_ref[...]
        return _pack_bf16_pairs(y[:, 0:half], y[:, half:])

    bm = x_ref.shape[0]
    top = bm // 2
    n_real = valid_ref[j]

    @pl.when(used & (n_real > top))
    def _():
        y_ref[...] = expert_rows(x_ref[...])

    @pl.when(used & (n_real <= top))
    def _():
        y_ref[0:top, :] = expert_rows(x_ref[0:top, :])
        y_ref[top:bm, :] = jnp.zeros((bm - top, y_ref.shape[1]), y_ref.dtype)

    @pl.when(j >= nu_ref[0])
    def _():
        y_ref[...] = jnp.zeros_like(y_ref)


def _ffn_call(layer, blk_e, blk_slot, blk_next, blk_valid, n_used, xs, w_gu, b_gu, w_dn, b_dn):
    n_blocks = blk_e.shape[0]
    d = 2 * xs.shape[1]
    bm = MOE_BLOCK
    pre = lambda f: (lambda j, be, sl, nx, va, nu: f(j, be))
    grid_spec = pltpu.PrefetchScalarGridSpec(
        num_scalar_prefetch=5,
        grid=(n_blocks,),
        in_specs=[
            pl.BlockSpec((bm, d // 2), pre(lambda j, be: (j, 0))),
            pl.BlockSpec(memory_space=pl.ANY),
            pl.BlockSpec((None, None, 1, 2 * D_FF), pre(lambda j, be: (layer, be[j], 0, 0))),
            pl.BlockSpec(memory_space=pl.ANY),
            pl.BlockSpec((None, None, 1, d), pre(lambda j, be: (layer, be[j], 0, 0))),
        ],
        out_specs=pl.BlockSpec((bm, d // 2), pre(lambda j, be: (j, 0))),
        scratch_shapes=[pltpu.VMEM((2, d, 2 * D_FF), F32), pltpu.VMEM((2, D_FF, d), F32),
                        pltpu.VMEM((d, 2 * D_FF), BF16), pltpu.VMEM((D_FF, d), BF16),
                        pltpu.SemaphoreType.DMA((2, 2))],
    )
    return pl.pallas_call(
        functools.partial(_ffn_kernel, layer=layer),
        grid_spec=grid_spec,
        out_shape=jax.ShapeDtypeStruct((n_blocks * bm, d // 2), I32),
        compiler_params=_cparams(("arbitrary",)),
        name="expert_ffn",
    )(blk_e, blk_slot, blk_next, blk_valid, n_used, xs, w_gu, b_gu, w_dn, b_dn)


def _combine_kernel(y1_ref, yk_ref, tw_ref, mod_ref, gf_ref, o_ref, *, final):
    tw8 = tw_ref[...]
    tw = jnp.concatenate([tw8, jnp.zeros((SMALL_W - ROUTE_ROWS, tw8.shape[1]), F32)], axis=0).T
    acc_lo, acc_hi = None, None
    for kk in range(TOP_K):
        lo, hi = _unpack_bf16_pairs(yk_ref[kk], F32)
        wk = tw[:, kk:kk + 1]
        acc_lo = wk * lo if acc_lo is None else acc_lo + wk * lo
        acc_hi = wk * hi if acc_hi is None else acc_hi + wk * hi
    y2 = y1_ref[...] + mod_ref[5] * jnp.concatenate([acc_lo, acc_hi], axis=1)
    if final:
        y2 = y2 * lax.rsqrt(jnp.mean(y2 * y2, axis=-1, keepdims=True) + NORM_EPS) * gf_ref[...]
    o_ref[...] = y2


def _combine_call(y1, yk, tw, mod, g_final, tm, seq_len, y_row0, n_rows, yk_row0, tw_row0, final):
    n, d = y1.shape
    tiles_per_seq = max(seq_len // tm, 1)
    y_off, yk_off, tw_off = y_row0 // tm, yk_row0 // tm, tw_row0 // tm
    return pl.pallas_call(
        functools.partial(_combine_kernel, final=final),
        grid=(n_rows // tm,),
        in_specs=[
            pl.BlockSpec((tm, d), lambda i: (i + y_off, 0)),
            pl.BlockSpec((TOP_K, tm, d // 2), lambda i: (0, i + yk_off, 0)),
            pl.BlockSpec((ROUTE_ROWS, tm), lambda i: (0, i + tw_off)),
            _mod_spec(mod, tm, tiles_per_seq, tile_off=y_off),
            pl.BlockSpec((1, d), lambda i: (0, 0)),
        ],
        out_specs=pl.BlockSpec((tm, d), lambda i: (i + y_off, 0)),
        out_shape=jax.ShapeDtypeStruct((n, d), F32),
        input_output_aliases={0: 0},
        compiler_params=_cparams(("arbitrary",)),
        name="moe_combine",
    )(y1, yk, tw, mod, g_final)


def _block_diag(w):
    nb, bw, _ = w.shape
    eye = jnp.eye(nb, dtype=w.dtype)
    return (eye[:, None, :, None] * w[:, :, None, :]).reshape(nb * bw, nb * bw)


def _layer_weights(l, w_in, b_forget, conv_w, conv_b, lru_lambda, lru_wa, lru_ba, lru_wx, lru_bx,
                   gla_w2, gla_b2, gla_gnorm, w_out, w_router, b_router):
    wi = w_in[l]
    d = wi.shape[0]
    w_ff = wi[:, _O_FF:_O_LX]
    w_ga = wi[:, _O_GA:_O_GOG]
    w_rec = jnp.concatenate([wi[:, _O_LX:_O_GA], wi[:, _O_GOG:]], axis=1)
    w2 = jnp.zeros((SMALL_W, GLA_W), F32).at[FOX_HEADS:FOX_HEADS + GLA_RANK].set(gla_w2[l])
    inw = {
        "w_q": wi[:, _O_FQ:_O_FK].astype(BF16),
        "w_qx": jnp.pad(wi[:, _O_FQ:_O_FK].reshape(d, FOX_HEADS, HEAD_DIM),
                        ((0, 0), (0, 0), (0, FOX_BLK - HEAD_DIM))).reshape(d, FOX_QX).astype(BF16),
        "w_kvt": wi[:, _O_FK:_O_FF].T.astype(BF16),
        "w_rec": w_rec.astype(BF16),
        "w_sm": jnp.concatenate([w_ff, w_ga, jnp.zeros((d, SMALL_W - FOX_HEADS - GLA_RANK), F32)], axis=1).astype(BF16),
        "w_smt": jnp.concatenate([w_ff, w_ga, jnp.zeros((d, SMALL_T - FOX_HEADS - GLA_RANK), F32)], axis=1).T.astype(BF16),
        "b_sm": jnp.zeros((1, SMALL_W), F32).at[0, 0:FOX_HEADS].set(b_forget[l]),
        "b_fc": b_forget[l].reshape(FOX_HEADS, 1),
        "w2": w2.astype(BF16),
        "b2": gla_b2[l].reshape(1, GLA_W),
        "w_gt": w_rec[:, 2 * GLA_W:].T.astype(BF16),
        "w2t": w2[0:SMALL_T].T.astype(BF16),
        "b2c": gla_b2[l].reshape(GLA_W, 1),
    }
    lw = {
        "conv_w": conv_w[l], "conv_b": conv_b[l].reshape(1, LRU_W),
        "wa": _block_diag(lru_wa[l]).astype(BF16), "ba": lru_ba[l].reshape(1, LRU_W),
        "wx": _block_diag(lru_wx[l]).astype(BF16), "bx": lru_bx[l].reshape(1, LRU_W),
        "lam": lru_lambda[l].reshape(1, LRU_W),
    }
    ow = {
        "gg": gla_gnorm[l].reshape(1, GLA_W),
        "ggc": gla_gnorm[l].reshape(GLA_W, 1),
        "w_out": w_out[l].astype(BF16),
        "w_r": jnp.concatenate([w_router[l], jnp.zeros((d, SMALL_W - N_EXPERTS), F32)], axis=1).astype(BF16),
        "b_r": jnp.zeros((1, SMALL_W), F32).at[0, 0:N_EXPERTS].set(b_router[l]),
    }
    return inw, lw, ow


def kernel(x_prompt, x_sample, cache_fox_k, cache_fox_v, cache_fox_logf, state_conv, state_lru, state_gla, page_table, c_prompt, c_sample, w_ada, b_ada, g_norm1, g_norm2, w_in, b_forget, conv_w, conv_b, lru_lambda, lru_wa, lru_ba, lru_wx, lru_bx, gla_w2, gla_b2, gla_gnorm, w_out, w_router, b_router, w_gu, b_gu, w_down, b_down, g_final):
    n_layers = w_ada.shape[0]
    bp, seq, d = x_prompt.shape
    bs = x_sample.shape[0]
    n_p = bp * seq
    n_tot = n_p + bs
    n_pool = cache_fox_k.shape[1]
    bm = MOE_BLOCK

    mod = _ada_call(jnp.concatenate([c_prompt, c_sample], axis=0), w_ada, b_ada)
    mod_p = mod[:, :bp].reshape(n_layers, bp, 6, 1, d).transpose(0, 2, 1, 3, 4)
    mod_s = mod[:, bp:].reshape(n_layers, 1, bs, 6, d).transpose(0, 3, 1, 2, 4)

    ckt = cache_fox_k.transpose(0, 1, 3, 4, 2).reshape(n_layers, n_pool, FOX_W, PAGE_SIZE)
    cvt = cache_fox_v.transpose(0, 1, 3, 4, 2).reshape(n_layers, n_pool, FOX_W, PAGE_SIZE)
    cft = cache_fox_logf.transpose(0, 1, 3, 2)
    conv_t = state_conv.transpose(0, 2, 1, 3)
    s_view = state_gla.transpose(0, 2, 3, 4, 1)
    b_gu4 = b_gu.reshape(n_layers, N_EXPERTS, 1, 2 * D_FF)
    b_dn4 = b_down.reshape(n_layers, N_EXPERTS, 1, d)
    gf = g_final.reshape(1, d)

    row_quant = SC_WINDOW * SC_WORKERS // TOP_K
    n_buf = -(-n_tot // row_quant) * row_quant
    n_a = (n_p // 2) // row_quant * row_quant
    parts = ((0, n_a, n_a), (n_a, n_buf - n_a, n_tot - n_a))

    def moe_scatter(l, h2, ti, part):
        row0, n_part, n_valid = part
        n_blocks = -(-(n_valid * TOP_K + N_EXPERTS * (bm - 1)) // bm)
        trash = n_blocks * bm
        dest, cnt = _rank_call(ti, row0, n_part, n_valid, trash, 512)
        counts = cnt[:, 0].astype(I32)
        pad_end = jnp.cumsum((counts + bm - 1) // bm * bm)
        blk_e = jnp.minimum(jnp.sum((jnp.arange(n_blocks, dtype=I32)[:, None] * bm >= pad_end[None, :]).astype(I32),
                                    axis=1), N_EXPERTS - 1).astype(I32)
        n_used = (pad_end[-1] // bm).astype(I32).reshape(1)
        ids = jnp.arange(N_EXPERTS, dtype=I32)
        present = counts > 0
        e_slot = (jnp.cumsum(present.astype(I32)) - 1) % 2
        later = jnp.where(present[None, :] & (ids[None, :] > ids[:, None]), ids[None, :], N_EXPERTS)
        e_next = jnp.min(later, axis=1)
        e_next = jnp.where(e_next < N_EXPERTS, e_next, -1)
        onehot = (blk_e[:, None] == ids[None, :]).astype(I32)
        blk_slot = jnp.sum(onehot * e_slot[None, :], axis=1).astype(I32)
        blk_next = jnp.sum(onehot * e_next[None, :], axis=1).astype(I32)
        e_first = (pad_end - (counts + bm - 1) // bm * bm) // bm
        blk_idx = jnp.arange(n_blocks, dtype=I32)
        blk_valid = jnp.clip(jnp.sum(onehot * counts[None, :], axis=1)
                             - (blk_idx - jnp.sum(onehot * e_first[None, :], axis=1)) * bm, 0, bm).astype(I32)
        dest_km = dest[0:TOP_K]
        xs = _sc_scatter_rows(h2, dest_km, (n_blocks + 1) * bm, row0)
        return xs, (blk_e, blk_slot, blk_next, blk_valid, n_used), jnp.minimum(dest_km, trash - 1)

    def moe_experts(l, sc):
        xs, tables, src = sc
        ye = _ffn_call(l, *tables, xs, w_gu, b_gu4, w_down, b_dn4)
        return _sc_gather_rows(ye, src.reshape(-1)).reshape(TOP_K, src.shape[1], d // 2)

    yp = x_prompt.reshape(n_p, d)
    ys = x_sample.reshape(bs, d)
    kv_p = (jnp.zeros((n_layers, bp, FOX_W, seq), F32), jnp.zeros((n_layers, bp, FOX_W, seq), F32),
            jnp.zeros((n_layers, bp, FOX_HEADS, seq), F32))
    outs_p = [[] for _ in range(3)]
    outs_s = [[] for _ in range(6)]
    for l in range(n_layers):
        inw, lw, ow = _layer_weights(l, w_in, b_forget, conv_w, conv_b, lru_lambda, lru_wa, lru_ba, lru_wx,
                                     lru_bx, gla_w2, gla_b2, gla_gnorm, w_out, w_router, b_router)
        g1 = g_norm1[l].reshape(1, d)
        g2 = g_norm2[l].reshape(1, d)

        kt_p, vt_p, lft_p, rz, qx, kx, vx = _inproj_call(yp, mod_p[l], g1, inw, seq, 256, l, n_layers, kv_bufs=kv_p)
        kv_p = (kt_p, vt_p, lft_p)
        fo = _fox_call(qx, kx, vx, bp, seq, 512)
        lo, conv_p, hlast_p = _lru_call(rz, lw, bp, seq, 256)
        go, st_p = _gla_call(rz, ow["gg"], bp, seq, 256)
        y1p, h2, ti, tw = _outproj_call(yp, fo, lo, go, mod_p[l], g2, ow["w_out"], ow["w_r"], ow["b_r"],
                                        256, seq, n_buf, 0)
        sc_a = moe_scatter(l, h2, ti, parts[0])
        st_p = st_p.reshape(bp, GLA_HEADS, HEAD_DIM, GLA_HEADS, HEAD_DIM)
        st_p = jnp.stack([st_p[:, hh, :, hh, :] for hh in range(GLA_HEADS)], axis=1).transpose(0, 1, 3, 2)
        outs_p[0].append(conv_p)
        outs_p[1].append(hlast_p.reshape(bp, LRU_W))
        outs_p[2].append(st_p)

        kts, vts, lfts, rzs, qs, ks, vs, logfs, gts = _inproj_call(
            ys, mod_s[l], g1, inw, bs, bs, 0, 1, sample=True)
        fos = _fox_dec_call(l, page_table, qs, ks, vs, logfs, ckt, cvt, cft)
        los, conv_s, h_s = _lru_step_call(l, rzs, conv_t, state_lru, lw)
        gost, s_s = _gla_step_call(l, gts, s_view, ow["ggc"])
        y1s, h2, ti, tw = _outproj_call(ys, fos, los, gost.T, mod_s[l], g2, ow["w_out"], ow["w_r"], ow["b_r"],
                                        bs, bs, n_buf, n_p, prev=(h2, ti, tw))
        outs_s[0].append(kts[0, 0])
        outs_s[1].append(vts[0, 0])
        outs_s[2].append(lfts[0, 0])
        outs_s[3].append(conv_s)
        outs_s[4].append(h_s)
        outs_s[5].append(s_s)

        sc_b = moe_scatter(l, h2, ti, parts[1])
        yk_a = moe_experts(l, sc_a)
        yk_b = moe_experts(l, sc_b)
        final = l == n_layers - 1
        yp = _combine_call(y1p, yk_a, tw, mod_p[l], gf, 256, seq, 0, n_a, 0, 0, final)
        yp = _combine_call(yp, yk_b, tw, mod_p[l], gf, 256, seq, n_a, n_p - n_a, 0, n_a, final)
        ys = _combine_call(y1s, yk_b, tw, mod_s[l], gf, bs, bs, 0, bs, n_p - n_a, n_p, final)

    kt_p, vt_p, lft_p = kv_p
    fox_k_p = kt_p.reshape(n_layers, bp, FOX_HEADS, HEAD_DIM, seq).transpose(0, 1, 4, 2, 3)
    fox_v_p = vt_p.reshape(n_layers, bp, FOX_HEADS, HEAD_DIM, seq).transpose(0, 1, 4, 2, 3)
    fox_f_p = lft_p.transpose(0, 1, 3, 2)
    fox_k_s = jnp.stack(outs_s[0]).reshape(n_layers, FOX_HEADS, HEAD_DIM, bs).transpose(0, 3, 1, 2)[:, :, None]
    fox_v_s = jnp.stack(outs_s[1]).reshape(n_layers, FOX_HEADS, HEAD_DIM, bs).transpose(0, 3, 1, 2)[:, :, None]
    fox_f_s = jnp.stack(outs_s[2]).transpose(0, 2, 1)[:, :, None]
    return (yp.reshape(bp, seq, d), ys.reshape(bs, 1, d),
            fox_k_p, fox_v_p, fox_f_p,
            jnp.stack(outs_p[0]), jnp.stack(outs_p[1]), jnp.stack(outs_p[2]),
            fox_k_s, fox_v_s, fox_f_s,
            jnp.stack(outs_s[3]).transpose(0, 2, 1, 3), jnp.stack(outs_s[4]),
            jnp.stack(outs_s[5]).transpose(0, 4, 1, 2, 3))
```
